```python
import math
import jax, jax.numpy as jnp
from jax import lax
import numpy as np

D_MODEL = 1024
BATCH = 32
SEQ = 256
DEPTH = 1
DEC_BATCH = 2
DEC_SEQ = 2048
PAST_LEN = 256

GRID_W = 64
MIX_W = D_MODEL
DN_W = MIX_W // 2
CV_W = MIX_W - DN_W
DN_HEADS = 4
DN_DK = DN_W // DN_HEADS
DN_DV = DN_DK
SHORT_CONV = 3
CHUNK = 64
CONV_K = 31
D_FF = 2816
N_MOD = 9
EPS = 1e-6
SPLITS = (3 * DN_W, 3 * DN_W + 2 * DN_HEADS, 3 * DN_W + 4 * DN_HEADS, 4 * DN_W + 4 * DN_HEADS)
IN_COLS = 4 * DN_W + 4 * DN_HEADS + 2 * CV_W

kernel_name = "hybrid_gdn_conformer_diffusion_step"


def _rmsnorm(x, g):
    xf = x.astype(jnp.float32)
    y = xf * lax.rsqrt(jnp.mean(xf * xf, axis=-1, keepdims=True) + EPS)
    return (y * g.astype(jnp.float32)).astype(x.dtype)


def _layernorm(x, g, b):
    xf = x.astype(jnp.float32)
    mu = jnp.mean(xf, axis=-1, keepdims=True)
    var = jnp.mean(jnp.square(xf - mu), axis=-1, keepdims=True)
    y = (xf - mu) * lax.rsqrt(var + EPS) * g.astype(jnp.float32) + b.astype(jnp.float32)
    return y.astype(x.dtype)


def _l2norm(x):
    return x * lax.rsqrt(jnp.sum(x * x, axis=-1, keepdims=True) + EPS)


def _dwconv(x, w, n_rows):
    b, t, ch = x.shape
    k = w.shape[0]
    xr = x.reshape(b * n_rows, t // n_rows, ch)
    y = lax.conv_general_dilated(xr, w.astype(x.dtype)[:, None, :], window_strides=(1,),
                                 padding=[(k // 2, k // 2)],
                                 dimension_numbers=("NWC", "WIO", "NWC"),
                                 feature_group_count=ch)
    return y.reshape(b, t, ch)


def _gated_delta_chunked(q, k, v, g, beta, s0):
    b, t, h, dk = q.shape
    dv = v.shape[-1]
    n = t // CHUNK

    def blocks(a):
        a = a.reshape(b, n, CHUNK, h, *a.shape[3:])
        return jnp.moveaxis(a, (1, 3), (0, 2))

    qc, kc, vc, bc = blocks(q), blocks(k), blocks(v), blocks(beta)
    gc = jnp.cumsum(blocks(g), axis=-1)
    idx = jnp.arange(CHUNK)
    incl = idx[:, None] >= idx[None, :]
    strict = idx[:, None] > idx[None, :]
    decay = jnp.exp(jnp.where(incl, gc[..., :, None] - gc[..., None, :], -jnp.inf))
    kb = kc * bc[..., None]
    m = jnp.where(strict, jnp.einsum("nbhcd,nbhed->nbhce", kb, kc) * decay, 0.0)
    eye = jnp.eye(CHUNK, dtype=jnp.float32)
    tinv = lax.linalg.triangular_solve(m + eye, jnp.broadcast_to(eye, m.shape), left_side=True,
                                       lower=True, unit_diagonal=True)
    u = tinv @ (vc * bc[..., None])
    w = tinv @ (kb * jnp.exp(gc)[..., None])
    qk = jnp.einsum("nbhcd,nbhed->nbhce", qc, kc) * decay
    q_dec = qc * jnp.exp(gc)[..., None]
    k_dec = kc * jnp.exp(gc[..., -1:] - gc)[..., None]
    g_last = jnp.exp(gc[..., -1])

    def step(s, xs):
        u_n, w_n, qk_n, qd_n, kd_n, gl_n = xs
        v_new = u_n - w_n @ s
        o = qd_n @ s + qk_n @ v_new
        s = s * gl_n[..., None, None] + jnp.swapaxes(kd_n, -1, -2) @ v_new
        return s, o

    s_fin, o = lax.scan(step, s0, (u, w, qk, q_dec, k_dec, g_last))
    o = jnp.moveaxis(o, (0, 2), (1, 3)).reshape(b, t, h, dv)
    return o, s_fin


def _bidir_gated_delta(q, k, v, g, beta, s0):
    rev = lambda a: jnp.flip(a, axis=1)
    o_f, s_f = _gated_delta_chunked(q, k, v, g[:, :, 0], beta[:, :, 0], s0[:, 0])
    o_b, s_b = _gated_delta_chunked(rev(q), rev(k), rev(v), rev(g[:, :, 1]), rev(beta[:, :, 1]), s0[:, 1])
    return o_f + rev(o_b), jnp.stack([s_f, s_b], axis=1)


def _token_mixer(h, s0, n_rows, w_in, dn_conv_w, dn_a_log, dn_dt_bias, dn_norm_g,
                 cv_dw_w, cv_dw_b, cv_ln_g, cv_ln_b, w_out):
    b, t, _ = h.shape
    proj = h @ w_in
    qkv, a, bb, z, glu = jnp.split(proj, SPLITS, axis=-1)
    qkv = jax.nn.silu(_dwconv(qkv, dn_conv_w, n_rows)).astype(jnp.float32)
    qkv = qkv.reshape(b, t, 3, DN_HEADS, DN_DK)
    q = _l2norm(qkv[:, :, 0]) * (DN_DK ** -0.5)
    k = _l2norm(qkv[:, :, 1])
    v = qkv[:, :, 2]
    a = a.astype(jnp.float32).reshape(b, t, 2, DN_HEADS)
    g = -jnp.exp(dn_a_log.astype(jnp.float32)) * jax.nn.softplus(a + dn_dt_bias.astype(jnp.float32))
    beta = jax.nn.sigmoid(bb.astype(jnp.float32).reshape(b, t, 2, DN_HEADS))
    o, s = _bidir_gated_delta(q, k, v, g, beta, s0.astype(jnp.float32))
    o = _rmsnorm(o, dn_norm_g) * jax.nn.silu(z.astype(jnp.float32).reshape(b, t, DN_HEADS, DN_DV))
    o = o.reshape(b, t, DN_W).astype(h.dtype)
    gv, gg = jnp.split(glu, 2, axis=-1)
    cv = gv * jax.nn.sigmoid(gg)
    cv = _dwconv(cv, cv_dw_w, n_rows) + cv_dw_b
    cv = jax.nn.silu(_layernorm(cv, cv_ln_g, cv_ln_b))
    y = jnp.concatenate([o, cv.astype(h.dtype)], axis=-1) @ w_out
    return y, s


def _swiglu(h, w_gu, w_down):
    gt, up = jnp.split(h @ w_gu, 2, axis=-1)
    return (jax.nn.silu(gt) * up) @ w_down


def _modulation(cond, w_mod, b_mod):
    m = jax.nn.silu(cond) @ w_mod + b_mod
    return m.reshape(cond.shape[0], 1, N_MOD, D_MODEL)


def _layer(x, mod, s0, n_rows, norm_g, ffn1_w_in, ffn1_w_out, w_in, dn_conv_w, dn_a_log,
           dn_dt_bias, dn_norm_g, cv_dw_w, cv_dw_b, cv_ln_g, cv_ln_b, w_out, ffn2_w_in, ffn2_w_out):
    def pre(i, xx):
        return _rmsnorm(xx, norm_g[2 * i]) * (1.0 + mod[:, :, 3 * i + 1]) + mod[:, :, 3 * i]

    def post(i, yy):
        return mod[:, :, 3 * i + 2] * _rmsnorm(yy, norm_g[2 * i + 1])

    x = x + 0.5 * post(0, _swiglu(pre(0, x), ffn1_w_in, ffn1_w_out))
    y, s = _token_mixer(pre(1, x), s0, n_rows, w_in, dn_conv_w, dn_a_log, dn_dt_bias, dn_norm_g,
                        cv_dw_w, cv_dw_b, cv_ln_g, cv_ln_b, w_out)
    x = x + post(1, y)
    x = x + 0.5 * post(2, _swiglu(pre(2, x), ffn2_w_in, ffn2_w_out))
    return x, s


def setup_inputs(seed: int = 0) -> dict:
    key = jax.random.key(seed)
    ks = jax.random.split(key, 24)
    f32 = jnp.float32
    nrm = lambda k, shape, scale: scale * jax.random.normal(k, shape, f32)
    dt = jnp.exp(jax.random.uniform(ks[13], (DEPTH, 2, DN_HEADS), f32, math.log(1e-3), math.log(1e-1)))
    return {
        "x_prompt": nrm(ks[0], (BATCH, SEQ, D_MODEL), 1.0),
        "x_sample": nrm(ks[1], (DEC_BATCH, DEC_SEQ, D_MODEL), 1.0),
        "state_delta": nrm(ks[2], (DEC_BATCH, DEPTH, 2, DN_HEADS, DN_DK, DN_DV), 0.1),
        "c": nrm(ks[3], (DEC_BATCH, D_MODEL), 1.0),
        "c_ctx": nrm(ks[4], (D_MODEL,), 1.0),
        "w_mod": nrm(ks[5], (DEPTH, D_MODEL, N_MOD * D_MODEL), 0.3 * D_MODEL ** -0.5),
        "b_mod": nrm(ks[6], (DEPTH, N_MOD * D_MODEL), 0.02),
        "norm_g": 1.0 + nrm(ks[7], (DEPTH, 6, D_MODEL), 0.02),
        "ffn1_w_in": nrm(ks[8], (DEPTH, D_MODEL, 2 * D_FF), D_MODEL ** -0.5),
        "ffn1_w_out": nrm(ks[9], (DEPTH, D_FF, D_MODEL), D_FF ** -0.5),
        "w_in": nrm(ks[10], (DEPTH, D_MODEL, IN_COLS), D_MODEL ** -0.5),
        "dn_conv_w": nrm(ks[11], (DEPTH, SHORT_CONV, 3 * DN_W), SHORT_CONV ** -0.5),
        "dn_a_log": jnp.log(jax.random.uniform(ks[12], (DEPTH, 2, DN_HEADS), f32, 1.0, 16.0)),
        "dn_dt_bias": dt + jnp.log(-jnp.expm1(-dt)),
        "dn_norm_g": 1.0 + nrm(ks[14], (DEPTH, DN_DV), 0.02),
        "cv_dw_w": nrm(ks[15], (DEPTH, CONV_K, CV_W), CONV_K ** -0.5),
        "cv_dw_b": nrm(ks[16], (DEPTH, CV_W), 0.02),
        "cv_ln_g": 1.0 + nrm(ks[17], (DEPTH, CV_W), 0.02),
        "cv_ln_b": nrm(ks[18], (DEPTH, CV_W), 0.02),
        "w_out": nrm(ks[19], (DEPTH, MIX_W, D_MODEL), MIX_W ** -0.5),
        "ffn2_w_in": nrm(ks[20], (DEPTH, D_MODEL, 2 * D_FF), D_MODEL ** -0.5),
        "ffn2_w_out": nrm(ks[21], (DEPTH, D_FF, D_MODEL), D_FF ** -0.5),
    }


def reference(x_prompt, x_sample, state_delta, c, c_ctx, w_mod, b_mod, norm_g, ffn1_w_in,
              ffn1_w_out, w_in, dn_conv_w, dn_a_log, dn_dt_bias, dn_norm_g, cv_dw_w, cv_dw_b,
              cv_ln_g, cv_ln_b, w_out, ffn2_w_in, ffn2_w_out):
    grid_rows = x_sample.shape[1] // GRID_W
    y_p, y_s = x_prompt, x_sample
    s_zero = jnp.zeros((x_prompt.shape[0], 2, DN_HEADS, DN_DK, DN_DV), jnp.float32)
    new_states = []
    for l in range(DEPTH):
        p = dict(norm_g=norm_g[l], ffn1_w_in=ffn1_w_in[l], ffn1_w_out=ffn1_w_out[l], w_in=w_in[l],
                 dn_conv_w=dn_conv_w[l], dn_a_log=dn_a_log[l], dn_dt_bias=dn_dt_bias[l],
                 dn_norm_g=dn_norm_g[l], cv_dw_w=cv_dw_w[l], cv_dw_b=cv_dw_b[l], cv_ln_g=cv_ln_g[l],
                 cv_ln_b=cv_ln_b[l], w_out=w_out[l], ffn2_w_in=ffn2_w_in[l], ffn2_w_out=ffn2_w_out[l])
        mod_ctx = _modulation(c_ctx[None, :], w_mod[l], b_mod[l])
        mod_lat = _modulation(c, w_mod[l], b_mod[l])
        y_p, s_ctx = _layer(y_p, mod_ctx, s_zero, 1, **p)
        new_states.append(s_ctx.astype(x_prompt.dtype))
        y_s, _ = _layer(y_s, mod_lat, state_delta[:, l], grid_rows, **p)
    new_state_delta = jnp.stack(new_states, axis=1)
    return (y_p, y_s, new_state_delta)
```

```python
import functools

import jax
import jax.numpy as jnp
from jax import lax
from jax.experimental import pallas as pl
from jax.experimental.pallas import tpu as pltpu

F32 = jnp.float32
BF16 = jnp.bfloat16

D_MODEL = 1024
D_FF = 2816
N_MOD = 9
GRID_W = 64
DN_W = 512
CV_W = 512
DN_HEADS = 4
DN_DK = 128
CHUNK = 64
SHORT_CONV = 3
CONV_K = 31
EPS = 1e-6

LANES = 128
BLOCK_T = 256
TOKEN_TILE = 256
CONV_HALO = 16
VMEM_LIMIT_BYTES = 56 * 1024 * 1024
FF_CHUNKS = ((0, 768), (768, 768), (1536, 768), (2304, 512))
QKV_W = 3 * DN_W
IN_QKV = (0, QKV_W)
IN_Z = (QKV_W, QKV_W + DN_W)
IN_GLU = (QKV_W + DN_W, QKV_W + DN_W + 2 * CV_W)
IN_AB = (QKV_W + DN_W + 2 * CV_W, QKV_W + DN_W + 2 * CV_W + LANES)
IN_COLS_PADDED = IN_AB[1]


def _dot(a, b):
    return jnp.dot(a, b, preferred_element_type=F32)


def _dot_nt(a, b):
    return lax.dot_general(a, b, (((1,), (1,)), ((), ())), preferred_element_type=F32)


def _dot_tn(a, b):
    return lax.dot_general(a, b, (((0,), (0,)), ((), ())), preferred_element_type=F32)


def _silu(x):
    return x * jax.nn.sigmoid(x)


def _rms(x, g):
    return x * lax.rsqrt(jnp.mean(x * x, axis=-1, keepdims=True) + EPS) * g


def _const_spec(shape):
    nd = len(shape)
    return pl.BlockSpec(shape, lambda *_: (0,) * nd, pipeline_mode=pl.Buffered(1))


def _params(n_grid_dims):
    return pltpu.CompilerParams(dimension_semantics=("arbitrary",) * n_grid_dims,
                                vmem_limit_bytes=VMEM_LIMIT_BYTES)


def _mod_kernel(c_ref, w_ref, b_ref, o_ref):
    s = _silu(c_ref[...]).astype(BF16)
    o_ref[...] = _dot(s, w_ref[...].astype(BF16)) + b_ref[...]


def _modulation(cond, w_mod, b_mod):
    n_out = w_mod.shape[1]
    tn = D_MODEL
    return pl.pallas_call(
        _mod_kernel,
        grid=(n_out // tn,),
        in_specs=[pl.BlockSpec((8, D_MODEL), lambda j: (0, 0)),
                  pl.BlockSpec((D_MODEL, tn), lambda j: (0, j)),
                  pl.BlockSpec((1, tn), lambda j: (0, j))],
        out_specs=pl.BlockSpec((8, tn), lambda j: (0, j)),
        out_shape=jax.ShapeDtypeStruct((8, n_out), F32),
        compiler_params=_params(1),
        name="modulation",
    )(cond, w_mod, b_mod.reshape(1, n_out))


def _swiglu(hb, wgu_ref, wd_ref):
    acc = None
    for s, n in FF_CHUNKS:
        gt = _dot(hb, wgu_ref[:, s:s + n])
        up = _dot(hb, wgu_ref[:, D_FF + s:D_FF + s + n])
        a = (_silu(gt) * up).astype(BF16)
        p = _dot(a, wd_ref[s:s + n, :])
        acc = p if acc is None else acc + p
    return acc


def _ffn1_proj_kernel(x_ref, mod_ref, g_ref, wgu_ref, wd_ref, win_ref,
                      x1_ref, qkv_ref, z_ref, glu_ref, ab_ref):
    x = x_ref[...]
    m = mod_ref[...]
    g = g_ref[...]
    h = _rms(x, g[0:1]) * (1.0 + m[1:2]) + m[0:1]
    f = _swiglu(h.astype(BF16), wgu_ref, wd_ref)
    x1 = x + 0.5 * (m[2:3] * _rms(f, g[1:2]))
    x1_ref[...] = x1
    h1 = (_rms(x1, g[2:3]) * (1.0 + m[4:5]) + m[3:4]).astype(BF16)
    qkv_ref[...] = _dot(h1, win_ref[:, IN_QKV[0]:IN_QKV[1]])
    z_ref[...] = _dot(h1, win_ref[:, IN_Z[0]:IN_Z[1]])
    glu_ref[...] = _dot(h1, win_ref[:, IN_GLU[0]:IN_GLU[1]])
    ab_ref[...] = _dot(h1, win_ref[:, IN_AB[0]:IN_AB[1]])


def _ffn1_proj(x, mod, norm_g, wgu, wd, win, tiles_per_mod):
    n = x.shape[0]
    tm = TOKEN_TILE
    row = lambda w: pl.BlockSpec((tm, w), lambda i: (i, 0))
    return pl.pallas_call(
        _ffn1_proj_kernel,
        grid=(n // tm,),
        in_specs=[row(D_MODEL),
                  pl.BlockSpec((None, N_MOD, D_MODEL), lambda i: (i // tiles_per_mod, 0, 0)),
                  _const_spec(norm_g.shape), _const_spec(wgu.shape), _const_spec(wd.shape),
                  _const_spec(win.shape)],
        out_specs=[row(D_MODEL), row(QKV_W), row(DN_W), row(2 * CV_W), row(LANES)],
        out_shape=[jax.ShapeDtypeStruct((n, D_MODEL), F32), jax.ShapeDtypeStruct((n, QKV_W), F32),
                   jax.ShapeDtypeStruct((n, DN_W), F32), jax.ShapeDtypeStruct((n, 2 * CV_W), F32),
                   jax.ShapeDtypeStruct((n, LANES), F32)],
        compiler_params=_params(1),
        name="ffn1_proj",
    )(x, mod, norm_g, wgu, wd, win)


def _out_ffn2_kernel(x1_ref, o_ref, z_ref, cv_ref, mod_ref, g_ref, dng_ref, wout_ref, wgu_ref, wd_ref,
                     y_ref):
    x1 = x1_ref[...]
    m = mod_ref[...]
    g = g_ref[...]
    o = o_ref[...]
    z = z_ref[...]
    dng = dng_ref[...]
    heads = []
    for h in range(DN_HEADS):
        oh = o[:, h * DN_DK:(h + 1) * DN_DK]
        zh = z[:, h * DN_DK:(h + 1) * DN_DK]
        heads.append((_rms(oh, dng) * _silu(zh)).astype(BF16))
    og = jnp.concatenate(heads, axis=-1)
    y = _dot(og, wout_ref[0:DN_W, :]) + _dot(cv_ref[...].astype(BF16), wout_ref[DN_W:, :])
    x2 = x1 + m[5:6] * _rms(y, g[3:4])
    h2 = _rms(x2, g[4:5]) * (1.0 + m[7:8]) + m[6:7]
    f = _swiglu(h2.astype(BF16), wgu_ref, wd_ref)
    y_ref[...] = x2 + 0.5 * (m[8:9] * _rms(f, g[5:6]))


def _out_ffn2(x1, o, z, cv, mod, norm_g, dn_norm_g, wout, wgu, wd, tiles_per_mod):
    n = x1.shape[0]
    tm = TOKEN_TILE
    row = lambda w: pl.BlockSpec((tm, w), lambda i: (i, 0))
    return pl.pallas_call(
        _out_ffn2_kernel,
        grid=(n // tm,),
        in_specs=[row(D_MODEL), row(DN_W), row(DN_W), row(CV_W),
                  pl.BlockSpec((None, N_MOD, D_MODEL), lambda i: (i // tiles_per_mod, 0, 0)),
                  _const_spec(norm_g.shape), _const_spec(dn_norm_g.shape), _const_spec(wout.shape),
                  _const_spec(wgu.shape), _const_spec(wd.shape)],
        out_specs=row(D_MODEL),
        out_shape=jax.ShapeDtypeStruct((n, D_MODEL), F32),
        compiler_params=_params(1),
        name="out_ffn2",
    )(x1, o, z, cv, mod, norm_g, dn_norm_g, wout, wgu, wd)


def _conv_module_kernel(glu_ref, w_ref, b_ref, lg_ref, lb_ref, o_ref, xp_ref, *, row_len):
    n_rows = TOKEN_TILE // row_len
    pitch = row_len + 2 * CONV_HALO
    x = glu_ref[:, 0:CV_W] * jax.nn.sigmoid(glu_ref[:, CV_W:2 * CV_W])
    zeros = jnp.zeros((CONV_HALO, CV_W), F32)
    for r in range(n_rows):
        base = r * pitch
        xp_ref[base:base + CONV_HALO, :] = zeros
        xp_ref[base + CONV_HALO:base + CONV_HALO + row_len, :] = x[r * row_len:(r + 1) * row_len, :]
        xp_ref[base + CONV_HALO + row_len:base + pitch, :] = zeros
    sub = CHUNK
    for r in range(n_rows):
        for sb in range(row_len // sub):
            first = r * pitch + CONV_HALO - CONV_K // 2 + sb * sub
            cols = []
            for lg in range(CV_W // LANES):
                ls = slice(lg * LANES, (lg + 1) * LANES)
                acc = jnp.zeros((sub, LANES), F32) + b_ref[:, ls]
                for j in range(CONV_K):
                    acc = acc + w_ref[j:j + 1, ls] * xp_ref[first + j:first + j + sub, ls]
                cols.append(acc)
            y = jnp.concatenate(cols, axis=-1)
            mu = jnp.mean(y, axis=-1, keepdims=True)
            var = jnp.mean(jnp.square(y - mu), axis=-1, keepdims=True)
            yn = (y - mu) * lax.rsqrt(var + EPS) * lg_ref[...] + lb_ref[...]
            o_ref[r * row_len + sb * sub:r * row_len + (sb + 1) * sub, :] = _silu(yn)


def _conv_module(glu, w, b, ln_g, ln_b, row_len):
    n = glu.shape[0]
    tm = TOKEN_TILE
    pitch = row_len + 2 * CONV_HALO
    return pl.pallas_call(
        functools.partial(_conv_module_kernel, row_len=row_len),
        grid=(n // tm,),
        in_specs=[pl.BlockSpec((tm, 2 * CV_W), lambda i: (i, 0)),
                  _const_spec(w.shape), _const_spec(b.shape), _const_spec(ln_g.shape),
                  _const_spec(ln_b.shape)],
        out_specs=pl.BlockSpec((tm, CV_W), lambda i: (i, 0)),
        out_shape=jax.ShapeDtypeStruct((n, CV_W), F32),
        scratch_shapes=[pltpu.VMEM(((tm // row_len) * pitch, CV_W), F32)],
        compiler_params=_params(1),
        name="conv_module",
    )(glu, w, b, ln_g, ln_b)


def _deltanet_kernel(qkv_ref, ab_ref, s0_ref, cw_ref, alog_ref, dtb_ref, o_ref, sfin_ref,
                     act_ref, gc_ref, tot_ref, beta_ref, st_ref, *, seq_len, row_len):
    n_blk = seq_len // BLOCK_T
    bt = BLOCK_T
    f32_inf = jnp.float32(jnp.inf)

    rowi = lax.broadcasted_iota(jnp.int32, (bt, LANES), 0)
    lanei = lax.broadcasted_iota(jnp.int32, (bt, LANES), 1)
    pos_row = rowi % row_len
    keep_prev = pos_row != 0
    keep_next = pos_row != row_len - 1
    pos_chunk = rowi % CHUNK

    def prep_block(b, carry):
        r0 = pl.multiple_of(b * bt, bt)
        rows = pl.ds(r0, bt)
        for s in range(QKV_W // LANES):
            ls = slice(s * LANES, (s + 1) * LANES)
            xs = qkv_ref[rows, ls]
            w = cw_ref[:, ls]
            xp = jnp.where(keep_prev, pltpu.roll(xs, 1, 0), 0.0)
            xn = jnp.where(keep_next, pltpu.roll(xs, bt - 1, 0), 0.0)
            y = _silu(w[0:1] * xp + w[1:2] * xs + w[2:3] * xn)
            if s < 2 * DN_HEADS:
                y = y * lax.rsqrt(jnp.sum(y * y, axis=-1, keepdims=True) + EPS)
            if s < DN_HEADS:
                y = y * (DN_DK ** -0.5)
            act_ref[rows, ls] = y
        ab = ab_ref[rows, :]
        t = ab + dtb_ref[...]
        softplus = jnp.maximum(t, 0.0) + jnp.log1p(jnp.exp(-jnp.abs(t)))
        g = -jnp.exp(alog_ref[...]) * softplus
        pre = g
        suf = g
        for sh in (1, 2, 4, 8, 16, 32):
            pre = pre + jnp.where(pos_chunk >= sh, pltpu.roll(pre, sh, 0), 0.0)
            suf = suf + jnp.where(pos_chunk < CHUNK - sh, pltpu.roll(suf, bt - sh, 0), 0.0)
        gc_ref[rows, :] = jnp.where(lanei < DN_HEADS, pre, suf)
        tot_ref[rows, :] = pre + suf - g
        beta_ref[rows, :] = jax.nn.sigmoid(ab)
        return carry

    lax.fori_loop(0, n_blk, prep_block, 0)

    st_ref[...] = s0_ref[...]
    if n_blk > 1:
        o_ref[...] = jnp.zeros((seq_len, DN_W), F32)

    ri = lax.broadcasted_iota(jnp.int32, (bt, bt), 0)
    ci = lax.broadcasted_iota(jnp.int32, (bt, bt), 1)
    same = (ri // CHUNK) == (ci // CHUNK)
    incl = (same & (ri >= ci), same & (ri <= ci))
    strict = (same & (ri > ci), same & (ri < ci))

    def direction_block(d, r0):
        rows = pl.ds(r0, bt)
        gc = gc_ref[rows, :]
        tot = tot_ref[rows, :]
        beta = beta_ref[rows, :]
        gc_t = jnp.transpose(gc)
        outs = []
        for h in range(DN_HEADS):
            c8 = d * DN_HEADS + h
            qh = act_ref[rows, h * DN_DK:(h + 1) * DN_DK]
            kh = act_ref[rows, DN_W + h * DN_DK:DN_W + (h + 1) * DN_DK]
            vh = act_ref[rows, 2 * DN_W + h * DN_DK:2 * DN_W + (h + 1) * DN_DK]
            khb = kh.astype(BF16)
            kk = _dot_nt(khb, khb)
            qk = _dot_nt(qh.astype(BF16), khb)
            g_i = gc[:, c8:c8 + 1]
            g_j = gc_t[c8:c8 + 1, :]
            dec = jnp.exp(jnp.where(incl[d], g_i - g_j, -f32_inf))
            b_i = beta[:, 2 * DN_HEADS + c8:2 * DN_HEADS + c8 + 1]
            nm = -(jnp.where(strict[d], kk * dec, 0.0) * b_i)
            qkm = (qk * dec).astype(BF16)
            eg = jnp.exp(g_i)
            x = jnp.concatenate([vh * b_i, kh * (b_i * eg)], axis=-1)
            pb = nm.astype(BF16)
            qm = nm
            for _ in range(5):
                p = _dot(pb, pb)
                pb = p.astype(BF16)
                qm = qm + p + _dot(qm.astype(BF16), pb)
            x = x + _dot(qm.astype(BF16), x.astype(BF16))
            u = x[:, 0:DN_DK]
            w = x[:, DN_DK:2 * DN_DK]
            qd = qh * eg
            kd = kh * jnp.exp(tot[:, c8:c8 + 1] - g_i)
            s = st_ref[c8]
            vnew = [None] * (bt // CHUNK)
            inter = [None] * (bt // CHUNK)
            order = range(bt // CHUNK) if d == 0 else range(bt // CHUNK - 1, -1, -1)
            for c in order:
                cs = slice(c * CHUNK, (c + 1) * CHUNK)
                lhs = jnp.concatenate([w[cs], qd[cs]], axis=0).astype(BF16)
                ws = _dot(lhs, s.astype(BF16))
                vnew[c] = u[cs] - ws[0:CHUNK]
                inter[c] = ws[CHUNK:2 * CHUNK]
                g_last = jnp.exp(tot[c * CHUNK:c * CHUNK + 1, c8:c8 + 1])
                s = s * g_last + _dot_tn(kd[cs].astype(BF16), vnew[c].astype(BF16))
            st_ref[c8] = s
            o = jnp.concatenate(inter, axis=0) + _dot(qkm, jnp.concatenate(vnew, axis=0).astype(BF16))
            outs.append(o)
        return outs

    if n_blk == 1:
        of = direction_block(0, 0)
        ob = direction_block(1, 0)
        for h in range(DN_HEADS):
            o_ref[:, h * DN_DK:(h + 1) * DN_DK] = of[h] + ob[h]
    else:
        def scan_block(b, carry):
            for d in range(2):
                blk = b if d == 0 else n_blk - 1 - b
                r0 = pl.multiple_of(blk * bt, bt)
                outs = direction_block(d, r0)
                for h in range(DN_HEADS):
                    o_ref[pl.ds(r0, bt), h * DN_DK:(h + 1) * DN_DK] += outs[h]
            return carry

        lax.fori_loop(0, n_blk, scan_block, 0)

    sfin_ref[...] = st_ref[...]


def _deltanet(qkv, ab, s0, conv_w, alog_row, dtb_row, row_len):
    nb, seq_len, _ = qkv.shape
    n_state = 2 * DN_HEADS
    seq = lambda w: pl.BlockSpec((None, seq_len, w), lambda b: (b, 0, 0))
    state = pl.BlockSpec((None, n_state, DN_DK, DN_DK), lambda b: (b, 0, 0, 0))
    return pl.pallas_call(
        functools.partial(_deltanet_kernel, seq_len=seq_len, row_len=row_len),
        grid=(nb,),
        in_specs=[pl.BlockSpec((None, seq_len, QKV_W), lambda b: (b, 0, 0), pipeline_mode=pl.Buffered(1)),
                  seq(LANES), state,
                  _const_spec(conv_w.shape), _const_spec(alog_row.shape), _const_spec(dtb_row.shape)],
        out_specs=[seq(DN_W), state],
        out_shape=[jax.ShapeDtypeStruct((nb, seq_len, DN_W), F32),
                   jax.ShapeDtypeStruct((nb, n_state, DN_DK, DN_DK), F32)],
        scratch_shapes=[pltpu.VMEM((seq_len, QKV_W), F32), pltpu.VMEM((seq_len, LANES), F32),
                        pltpu.VMEM((seq_len, LANES), F32), pltpu.VMEM((seq_len, LANES), F32),
                        pltpu.VMEM((n_state, DN_DK, DN_DK), F32)],
        compiler_params=_params(1),
        name="deltanet",
    )(qkv, ab, s0, conv_w, alog_row, dtb_row)


def _layer(x, mod, s0, row_len, p):
    nb, seq_len, _ = x.shape
    n = nb * seq_len
    tiles_per_mod = (n // mod.shape[0]) // TOKEN_TILE
    x1, qkv, z, glu, ab = _ffn1_proj(x.reshape(n, D_MODEL), mod, p["norm_g"], p["wgu1"], p["wd1"],
                                     p["win"], tiles_per_mod)
    o, s_fin = _deltanet(qkv.reshape(nb, seq_len, QKV_W), ab.reshape(nb, seq_len, LANES), s0,
                         p["dn_conv_w"], p["alog_row"], p["dtb_row"], row_len)
    cv = _conv_module(glu, p["cv_dw_w"], p["cv_dw_b"], p["cv_ln_g"], p["cv_ln_b"], row_len)
    y = _out_ffn2(x1, o.reshape(n, DN_W), z, cv, mod, p["norm_g"], p["dn_norm_g"], p["wout"],
                  p["wgu2"], p["wd2"], tiles_per_mod)
    return y.reshape(nb, seq_len, D_MODEL), s_fin


def kernel(x_prompt, x_sample, state_delta, c, c_ctx, w_mod, b_mod, norm_g, ffn1_w_in, ffn1_w_out, w_in,
           dn_conv_w, dn_a_log, dn_dt_bias, dn_norm_g, cv_dw_w, cv_dw_b, cv_ln_g, cv_ln_b, w_out,
           ffn2_w_in, ffn2_w_out):
    depth = w_mod.shape[0]
    assert depth == 1, "one trunk layer"
    batch, seq_len, _ = x_prompt.shape
    dec_batch, dec_seq, _ = x_sample.shape
    n_state = 2 * DN_HEADS

    cond = jnp.zeros((8, D_MODEL), F32).at[0].set(c_ctx).at[1:1 + dec_batch].set(c)
    mod = _modulation(cond, w_mod[0], b_mod[0]).reshape(8, N_MOD, D_MODEL)
    mod_ctx, mod_lat = mod[0:1], mod[1:1 + dec_batch]

    wi = w_in[0]
    a0 = 4 * DN_W + 4 * DN_HEADS
    win = jnp.concatenate([wi[:, 0:QKV_W], wi[:, QKV_W + 4 * DN_HEADS:a0], wi[:, a0:],
                           wi[:, QKV_W:QKV_W + 4 * DN_HEADS],
                           jnp.zeros((D_MODEL, LANES - 4 * DN_HEADS), F32)], axis=1).astype(BF16)
    pad8 = lambda v: jnp.zeros((1, LANES), F32).at[0, 0:n_state].set(v.reshape(n_state))
    p = dict(norm_g=norm_g[0], wgu1=ffn1_w_in[0].astype(BF16), wd1=ffn1_w_out[0].astype(BF16), win=win,
             dn_conv_w=dn_conv_w[0], alog_row=pad8(dn_a_log[0]), dtb_row=pad8(dn_dt_bias[0]),
             dn_norm_g=dn_norm_g[0].reshape(1, DN_DK), cv_dw_w=cv_dw_w[0],
             cv_dw_b=cv_dw_b[0].reshape(1, CV_W), cv_ln_g=cv_ln_g[0].reshape(1, CV_W),
             cv_ln_b=cv_ln_b[0].reshape(1, CV_W), wout=w_out[0].astype(BF16),
             wgu2=ffn2_w_in[0].astype(BF16), wd2=ffn2_w_out[0].astype(BF16))

    s_zero = jnp.zeros((batch, n_state, DN_DK, DN_DK), F32)
    y_p, s_ctx = _layer(x_prompt, mod_ctx, s_zero, seq_len, p)
    s_lat = state_delta[:, 0].reshape(dec_batch, n_state, DN_DK, DN_DK)
    y_s, _ = _layer(x_sample, mod_lat, s_lat, GRID_W, p)
    new_state = s_ctx.reshape(batch, 1, 2, DN_HEADS, DN_DK, DN_DK).astype(x_prompt.dtype)
    return (y_p, y_s, new_state)
```

```python
import functools

import jax
import jax.numpy as jnp
from jax import lax
from jax.experimental import pallas as pl
from jax.experimental.pallas import tpu as pltpu

F32 = jnp.float32
BF16 = jnp.bfloat16

D_MODEL = 1024
D_FF = 2816
N_MOD = 9
GRID_W = 64
DN_W = 512
CV_W = 512
DN_HEADS = 4
DN_DK = 128
CHUNK = 64
SHORT_CONV = 3
CONV_K = 31
EPS = 1e-6

LANES = 128
BLOCK_T = 256
PAIR_T = 2 * CHUNK
TOKEN_TILE = 256
CONV_HALO = 16
VMEM_LIMIT_BYTES = 56 * 1024 * 1024
FF_CHUNKS = ((0, 768), (768, 768), (1536, 768), (2304, 512))
QKV_W = 3 * DN_W
IN_QKV = (0, QKV_W)
IN_Z = (QKV_W, QKV_W + DN_W)
IN_GLU = (QKV_W + DN_W, QKV_W + DN_W + 2 * CV_W)
IN_AB = (QKV_W + DN_W + 2 * CV_W, QKV_W + DN_W + 2 * CV_W + LANES)
IN_COLS_PADDED = IN_AB[1]


def _dot(a, b):
    return jnp.dot(a, b, preferred_element_type=F32)


def _dot_nt(a, b):
    return lax.dot_general(a, b, (((1,), (1,)), ((), ())), preferred_element_type=F32)


def _dot_tn(a, b):
    return lax.dot_general(a, b, (((0,), (0,)), ((), ())), preferred_element_type=F32)


def _silu(x):
    return x * jax.nn.sigmoid(x)


def _rms(x, g):
    return x * lax.rsqrt(jnp.mean(x * x, axis=-1, keepdims=True) + EPS) * g


def _const_spec(shape):
    nd = len(shape)
    return pl.BlockSpec(shape, lambda *_: (0,) * nd, pipeline_mode=pl.Buffered(1))


def _params(n_grid_dims):
    return pltpu.CompilerParams(dimension_semantics=("arbitrary",) * n_grid_dims,
                                vmem_limit_bytes=VMEM_LIMIT_BYTES)


def _mod_kernel(c_ref, w_ref, b_ref, o_ref):
    s = _silu(c_ref[...]).astype(BF16)
    o_ref[...] = _dot(s, w_ref[...].astype(BF16)) + b_ref[...]


def _modulation(cond, w_mod, b_mod):
    n_out = w_mod.shape[1]
    tn = D_MODEL
    return pl.pallas_call(
        _mod_kernel,
        grid=(n_out // tn,),
        in_specs=[pl.BlockSpec((8, D_MODEL), lambda j: (0, 0)),
                  pl.BlockSpec((D_MODEL, tn), lambda j: (0, j)),
                  pl.BlockSpec((1, tn), lambda j: (0, j))],
        out_specs=pl.BlockSpec((8, tn), lambda j: (0, j)),
        out_shape=jax.ShapeDtypeStruct((8, n_out), F32),
        compiler_params=_params(1),
        name="modulation",
    )(cond, w_mod, b_mod.reshape(1, n_out))


def _swiglu(hb, wgu_ref, wd_ref):
    acc = None
    for s, n in FF_CHUNKS:
        gt = _dot(hb, wgu_ref[:, s:s + n])
        up = _dot(hb, wgu_ref[:, D_FF + s:D_FF + s + n])
        a = (_silu(gt) * up).astype(BF16)
        p = _dot(a, wd_ref[s:s + n, :])
        acc = p if acc is None else acc + p
    return acc


def _ffn1_proj_kernel(x_ref, mod_ref, g_ref, wgu_ref, wd_ref, win_ref,
                      x1_ref, qkv_ref, z_ref, glu_ref, ab_ref):
    x = x_ref[...]
    m = mod_ref[...]
    g = g_ref[...]
    h = _rms(x, g[0:1]) * (1.0 + m[1:2]) + m[0:1]
    f = _swiglu(h.astype(BF16), wgu_ref, wd_ref)
    x1 = x + 0.5 * (m[2:3] * _rms(f, g[1:2]))
    x1_ref[...] = x1
    h1 = (_rms(x1, g[2:3]) * (1.0 + m[4:5]) + m[3:4]).astype(BF16)
    qkv_ref[...] = _dot(h1, win_ref[:, IN_QKV[0]:IN_QKV[1]])
    z_ref[...] = _dot(h1, win_ref[:, IN_Z[0]:IN_Z[1]])
    glu_ref[...] = _dot(h1, win_ref[:, IN_GLU[0]:IN_GLU[1]])
    ab_ref[...] = _dot(h1, win_ref[:, IN_AB[0]:IN_AB[1]])


def _ffn1_proj(x, mod, norm_g, wgu, wd, win, tiles_per_mod):
    n = x.shape[0]
    tm = TOKEN_TILE
    row = lambda w: pl.BlockSpec((tm, w), lambda i: (i, 0))
    return pl.pallas_call(
        _ffn1_proj_kernel,
        grid=(n // tm,),
        in_specs=[row(D_MODEL),
                  pl.BlockSpec((None, N_MOD, D_MODEL), lambda i: (i // tiles_per_mod, 0, 0)),
                  _const_spec(norm_g.shape), _const_spec(wgu.shape), _const_spec(wd.shape),
                  _const_spec(win.shape)],
        out_specs=[row(D_MODEL), row(QKV_W), row(DN_W), row(2 * CV_W), row(LANES)],
        out_shape=[jax.ShapeDtypeStruct((n, D_MODEL), F32), jax.ShapeDtypeStruct((n, QKV_W), F32),
                   jax.ShapeDtypeStruct((n, DN_W), F32), jax.ShapeDtypeStruct((n, 2 * CV_W), F32),
                   jax.ShapeDtypeStruct((n, LANES), F32)],
        compiler_params=_params(1),
        name="ffn1_proj",
    )(x, mod, norm_g, wgu, wd, win)


def _out_ffn2_kernel(x1_ref, o_ref, z_ref, cv_ref, mod_ref, g_ref, dng_ref, wout_ref, wgu_ref, wd_ref,
                     y_ref):
    x1 = x1_ref[...]
    m = mod_ref[...]
    g = g_ref[...]
    o = o_ref[...]
    z = z_ref[...]
    dng = dng_ref[...]
    heads = []
    for h in range(DN_HEADS):
        oh = o[:, h * DN_DK:(h + 1) * DN_DK]
        zh = z[:, h * DN_DK:(h + 1) * DN_DK]
        heads.append((_rms(oh, dng) * _silu(zh)).astype(BF16))
    og = jnp.concatenate(heads, axis=-1)
    y = _dot(og, wout_ref[0:DN_W, :]) + _dot(cv_ref[...].astype(BF16), wout_ref[DN_W:, :])
    x2 = x1 + m[5:6] * _rms(y, g[3:4])
    h2 = _rms(x2, g[4:5]) * (1.0 + m[7:8]) + m[6:7]
    f = _swiglu(h2.astype(BF16), wgu_ref, wd_ref)
    y_ref[...] = x2 + 0.5 * (m[8:9] * _rms(f, g[5:6]))


def _out_ffn2(x1, o, z, cv, mod, norm_g, dn_norm_g, wout, wgu, wd, tiles_per_mod):
    n = x1.shape[0]
    tm = TOKEN_TILE
    row = lambda w: pl.BlockSpec((tm, w), lambda i: (i, 0))
    return pl.pallas_call(
        _out_ffn2_kernel,
        grid=(n // tm,),
        in_specs=[row(D_MODEL), row(DN_W), row(DN_W), row(CV_W),
                  pl.BlockSpec((None, N_MOD, D_MODEL), lambda i: (i // tiles_per_mod, 0, 0)),
                  _const_spec(norm_g.shape), _const_spec(dn_norm_g.shape), _const_spec(wout.shape),
                  _const_spec(wgu.shape), _const_spec(wd.shape)],
        out_specs=row(D_MODEL),
        out_shape=jax.ShapeDtypeStruct((n, D_MODEL), F32),
        compiler_params=_params(1),
        name="out_ffn2",
    )(x1, o, z, cv, mod, norm_g, dn_norm_g, wout, wgu, wd)


def _conv_module_kernel(glu_ref, w_ref, b_ref, lg_ref, lb_ref, o_ref, xp_ref, *, row_len):
    n_rows = TOKEN_TILE // row_len
    pitch = row_len + 2 * CONV_HALO
    x = glu_ref[:, 0:CV_W] * jax.nn.sigmoid(glu_ref[:, CV_W:2 * CV_W])
    zeros = jnp.zeros((CONV_HALO, CV_W), F32)
    for r in range(n_rows):
        base = r * pitch
        xp_ref[base:base + CONV_HALO, :] = zeros
        xp_ref[base + CONV_HALO:base + CONV_HALO + row_len, :] = x[r * row_len:(r + 1) * row_len, :]
        xp_ref[base + CONV_HALO + row_len:base + pitch, :] = zeros
    sub = CHUNK
    for r in range(n_rows):
        for sb in range(row_len // sub):
            first = r * pitch + CONV_HALO - CONV_K // 2 + sb * sub
            cols = []
            for lg in range(CV_W // LANES):
                ls = slice(lg * LANES, (lg + 1) * LANES)
                acc = jnp.zeros((sub, LANES), F32) + b_ref[:, ls]
                for j in range(CONV_K):
                    acc = acc + w_ref[j:j + 1, ls] * xp_ref[first + j:first + j + sub, ls]
                cols.append(acc)
            y = jnp.concatenate(cols, axis=-1)
            mu = jnp.mean(y, axis=-1, keepdims=True)
            var = jnp.mean(jnp.square(y - mu), axis=-1, keepdims=True)
            yn = (y - mu) * lax.rsqrt(var + EPS) * lg_ref[...] + lb_ref[...]
            o_ref[r * row_len + sb * sub:r * row_len + (sb + 1) * sub, :] = _silu(yn)


def _conv_module(glu, w, b, ln_g, ln_b, row_len):
    n = glu.shape[0]
    tm = TOKEN_TILE
    pitch = row_len + 2 * CONV_HALO
    return pl.pallas_call(
        functools.partial(_conv_module_kernel, row_len=row_len),
        grid=(n // tm,),
        in_specs=[pl.BlockSpec((tm, 2 * CV_W), lambda i: (i, 0)),
                  _const_spec(w.shape), _const_spec(b.shape), _const_spec(ln_g.shape),
                  _const_spec(ln_b.shape)],
        out_specs=pl.BlockSpec((tm, CV_W), lambda i: (i, 0)),
        out_shape=jax.ShapeDtypeStruct((n, CV_W), F32),
        scratch_shapes=[pltpu.VMEM(((tm // row_len) * pitch, CV_W), F32)],
        compiler_params=_params(1),
        name="conv_module",
    )(glu, w, b, ln_g, ln_b)


def _deltanet_kernel(qkv_ref, ab_ref, s0_ref, cw_ref, alog_ref, dtb_ref, o_ref, sfin_ref,
                     act_ref, gc_ref, tot_ref, beta_ref, st_ref, *, seq_len, row_len):
    n_blk = seq_len // BLOCK_T
    bt = BLOCK_T
    f32_inf = jnp.float32(jnp.inf)

    rowi = lax.broadcasted_iota(jnp.int32, (bt, LANES), 0)
    lanei = lax.broadcasted_iota(jnp.int32, (bt, LANES), 1)
    pos_row = jnp.bitwise_and(rowi, row_len - 1)
    keep_prev = pos_row != 0
    keep_next = pos_row != row_len - 1
    pos_chunk = jnp.bitwise_and(rowi, CHUNK - 1)

    def prep_block(b, carry):
        r0 = pl.multiple_of(b * bt, bt)
        rows = pl.ds(r0, bt)
        for s in range(QKV_W // LANES):
            ls = slice(s * LANES, (s + 1) * LANES)
            xs = qkv_ref[rows, ls]
            w = cw_ref[:, ls]
            xp = jnp.where(keep_prev, pltpu.roll(xs, 1, 0), 0.0)
            xn = jnp.where(keep_next, pltpu.roll(xs, bt - 1, 0), 0.0)
            y = _silu(w[0:1] * xp + w[1:2] * xs + w[2:3] * xn)
            if s < 2 * DN_HEADS:
                y = y * lax.rsqrt(jnp.sum(y * y, axis=-1, keepdims=True) + EPS)
            if s < DN_HEADS:
                y = y * (DN_DK ** -0.5)
            act_ref[rows, ls] = y
        ab = ab_ref[rows, :]
        t = ab + dtb_ref[...]
        softplus = jnp.maximum(t, 0.0) + jnp.log1p(jnp.exp(-jnp.abs(t)))
        g = -jnp.exp(alog_ref[...]) * softplus
        pre = g
        suf = g
        for sh in (1, 2, 4, 8, 16, 32):
            pre = pre + jnp.where(pos_chunk >= sh, pltpu.roll(pre, sh, 0), 0.0)
            suf = suf + jnp.where(pos_chunk < CHUNK - sh, pltpu.roll(suf, bt - sh, 0), 0.0)
        gc_ref[rows, :] = jnp.where(lanei < DN_HEADS, pre, suf)
        tot_ref[rows, :] = pre + suf - g
        beta_ref[rows, :] = jax.nn.sigmoid(ab)
        return carry

    lax.fori_loop(0, n_blk, prep_block, 0)

    st_ref[...] = s0_ref[...]
    if n_blk > 1:
        o_ref[...] = jnp.zeros((seq_len, DN_W), F32)

    pt = PAIR_T
    n_pair = bt // pt
    n_chunk = bt // CHUNK
    ri = lax.broadcasted_iota(jnp.int32, (pt, pt), 0)
    ci = lax.broadcasted_iota(jnp.int32, (pt, pt), 1)
    same = jnp.right_shift(ri, 6) == jnp.right_shift(ci, 6)
    incl = (same & (ri >= ci), same & (ri <= ci))
    strict = (same & (ri > ci), same & (ri < ci))

    def delta_blocks(jobs):
        gram = {}
        probs = {}
        for d, r0, key in jobs:
            rows = pl.ds(r0, bt)
            gc = gc_ref[rows, :]
            tot = tot_ref[rows, :]
            beta = beta_ref[rows, :]
            gc_t = jnp.transpose(gc)
            beta_t = jnp.transpose(beta)
            for h in range(DN_HEADS):
                c8 = d * DN_HEADS + h
                for p in range(n_pair):
                    ps = slice(p * pt, (p + 1) * pt)
                    prow = pl.ds(r0 + p * pt, pt)
                    qh = act_ref[prow, h * DN_DK:(h + 1) * DN_DK]
                    kh = act_ref[prow, DN_W + h * DN_DK:DN_W + (h + 1) * DN_DK]
                    vh = act_ref[prow, 2 * DN_W + h * DN_DK:2 * DN_W + (h + 1) * DN_DK]
                    if (key, h, p) not in gram:
                        khb = kh.astype(BF16)
                        gram[(key, h, p)] = _dot_nt(jnp.concatenate([khb, qh.astype(BF16)], axis=0), khb)
                    gr = gram[(key, h, p)]
                    kk = gr[0:pt]
                    qk = gr[pt:2 * pt]
                    g_i = gc[ps, c8:c8 + 1]
                    diff = g_i - gc_t[c8:c8 + 1, ps]
                    dec = jnp.exp(jnp.where(incl[d], diff, -f32_inf))
                    dec_t = jnp.exp(jnp.where(incl[1 - d], -diff, -f32_inf))
                    b_i = beta[ps, 2 * DN_HEADS + c8:2 * DN_HEADS + c8 + 1]
                    b_j = beta_t[2 * DN_HEADS + c8:2 * DN_HEADS + c8 + 1, ps]
                    n_t = -(jnp.where(strict[1 - d], kk * dec_t, 0.0) * b_j)
                    eg = jnp.exp(g_i)
                    probs[(d, h, p)] = dict(
                        pt=n_t, qt=n_t, qkm=(qk * dec).astype(BF16),
                        x=jnp.concatenate([vh * b_i, kh * (b_i * eg)], axis=-1),
                        qd=qh * eg, kd=kh * jnp.exp(tot[ps, c8:c8 + 1] - g_i), tot=tot[ps, c8:c8 + 1])
        for pr in probs.values():
            pb = pr["pt"].astype(BF16)
            pr["pt"] = _dot(pb, pb)
        for _ in range(4):
            for pr in probs.values():
                pb = pr["pt"].astype(BF16)
                r = _dot(pb, jnp.concatenate([pb, pr["qt"].astype(BF16)], axis=-1))
                pr["qt"] = pr["qt"] + pr["pt"] + r[:, pt:2 * pt]
                pr["pt"] = r[:, 0:pt]
        for pr in probs.values():
            r = _dot(pr["pt"].astype(BF16), pr["qt"].astype(BF16))
            qm = jnp.transpose(pr["qt"] + pr["pt"] + r)
            pr["x"] = pr["x"] + _dot(qm.astype(BF16), pr["x"].astype(BF16))
        chains = [(d, h) for d, _, _ in jobs for h in range(DN_HEADS)]
        state = {ch: st_ref[ch[0] * DN_HEADS + ch[1]] for ch in chains}
        vnew = {}
        inter = {}
        for step in range(n_chunk):
            ws = {}
            for d, h in chains:
                c = step if d == 0 else n_chunk - 1 - step
                pr = probs[(d, h, c // 2)]
                cs = slice((c % 2) * CHUNK, (c % 2 + 1) * CHUNK)
                lhs = jnp.concatenate([pr["x"][cs, DN_DK:2 * DN_DK], pr["qd"][cs]], axis=0).astype(BF16)
                ws[(d, h)] = _dot(lhs, state[(d, h)].astype(BF16))
            for d, h in chains:
                c = step if d == 0 else n_chunk - 1 - step
                pr = probs[(d, h, c // 2)]
                cs = slice((c % 2) * CHUNK, (c % 2 + 1) * CHUNK)
                vn = pr["x"][cs, 0:DN_DK] - ws[(d, h)][0:CHUNK]
                vnew[(d, h, c)] = vn
                inter[(d, h, c)] = ws[(d, h)][CHUNK:2 * CHUNK]
                g_last = jnp.exp(pr["tot"][(c % 2) * CHUNK:(c % 2) * CHUNK + 1])
                state[(d, h)] = state[(d, h)] * g_last + _dot_tn(pr["kd"][cs].astype(BF16), vn.astype(BF16))
        outs = {}
        for d, h in chains:
            st_ref[d * DN_HEADS + h] = state[(d, h)]
            o_pairs = []
            for p in range(n_pair):
                vn = jnp.concatenate([vnew[(d, h, 2 * p)], vnew[(d, h, 2 * p + 1)]], axis=0)
                it = jnp.concatenate([inter[(d, h, 2 * p)], inter[(d, h, 2 * p + 1)]], axis=0)
                o_pairs.append(it + _dot(probs[(d, h, p)]["qkm"], vn.astype(BF16)))
            outs[(d, h)] = jnp.concatenate(o_pairs, axis=0)
        return outs

    if n_blk == 1:
        outs = delta_blocks([(0, 0, 0), (1, 0, 0)])
        for h in range(DN_HEADS):
            o_ref[:, h * DN_DK:(h + 1) * DN_DK] = outs[(0, h)] + outs[(1, h)]
    else:
        def scan_block(b, carry):
            r0f = pl.multiple_of(b * bt, bt)
            r0b = pl.multiple_of((n_blk - 1 - b) * bt, bt)
            outs = delta_blocks([(0, r0f, 0), (1, r0b, 1)])
            for d, r0 in ((0, r0f), (1, r0b)):
                for h in range(DN_HEADS):
                    o_ref[pl.ds(r0, bt), h * DN_DK:(h + 1) * DN_DK] += outs[(d, h)]
            return carry

        lax.fori_loop(0, n_blk, scan_block, 0)

    sfin_ref[...] = st_ref[...]


def _deltanet(qkv, ab, s0, conv_w, alog_row, dtb_row, row_len):
    nb, seq_len, _ = qkv.shape
    n_state = 2 * DN_HEADS
    seq = lambda w: pl.BlockSpec((None, seq_len, w), lambda b: (b, 0, 0))
    state = pl.BlockSpec((None, n_state, DN_DK, DN_DK), lambda b: (b, 0, 0, 0))
    return pl.pallas_call(
        functools.partial(_deltanet_kernel, seq_len=seq_len, row_len=row_len),
        grid=(nb,),
        in_specs=[pl.BlockSpec((None, seq_len, QKV_W), lambda b: (b, 0, 0), pipeline_mode=pl.Buffered(1)),
                  seq(LANES), state,
                  _const_spec(conv_w.shape), _const_spec(alog_row.shape), _const_spec(dtb_row.shape)],
        out_specs=[seq(DN_W), state],
        out_shape=[jax.ShapeDtypeStruct((nb, seq_len, DN_W), F32),
                   jax.ShapeDtypeStruct((nb, n_state, DN_DK, DN_DK), F32)],
        scratch_shapes=[pltpu.VMEM((seq_len, QKV_W), F32), pltpu.VMEM((seq_len, LANES), F32),
                        pltpu.VMEM((seq_len, LANES), F32), pltpu.VMEM((seq_len, LANES), F32),
                        pltpu.VMEM((n_state, DN_DK, DN_DK), F32)],
        compiler_params=_params(1),
        name="deltanet",
    )(qkv, ab, s0, conv_w, alog_row, dtb_row)


def _layer(x, mod, s0, row_len, p):
    nb, seq_len, _ = x.shape
    n = nb * seq_len
    tiles_per_mod = (n // mod.shape[0]) // TOKEN_TILE
    x1, qkv, z, glu, ab = _ffn1_proj(x.reshape(n, D_MODEL), mod, p["norm_g"], p["wgu1"], p["wd1"],
                                     p["win"], tiles_per_mod)
    o, s_fin = _deltanet(qkv.reshape(nb, seq_len, QKV_W), ab.reshape(nb, seq_len, LANES), s0,
                         p["dn_conv_w"], p["alog_row"], p["dtb_row"], row_len)
    cv = _conv_module(glu, p["cv_dw_w"], p["cv_dw_b"], p["cv_ln_g"], p["cv_ln_b"], row_len)
    y = _out_ffn2(x1, o.reshape(n, DN_W), z, cv, mod, p["norm_g"], p["dn_norm_g"], p["wout"],
                  p["wgu2"], p["wd2"], tiles_per_mod)
    return y.reshape(nb, seq_len, D_MODEL), s_fin


def kernel(x_prompt, x_sample, state_delta, c, c_ctx, w_mod, b_mod, norm_g, ffn1_w_in, ffn1_w_out, w_in,
           dn_conv_w, dn_a_log, dn_dt_bias, dn_norm_g, cv_dw_w, cv_dw_b, cv_ln_g, cv_ln_b, w_out,
           ffn2_w_in, ffn2_w_out):
    depth = w_mod.shape[0]
    assert depth == 1, "one trunk layer"
    batch, seq_len, _ = x_prompt.shape
    dec_batch, dec_seq, _ = x_sample.shape
    n_state = 2 * DN_HEADS

    cond = jnp.zeros((8, D_MODEL), F32).at[0].set(c_ctx).at[1:1 + dec_batch].set(c)
    mod = _modulation(cond, w_mod[0], b_mod[0]).reshape(8, N_MOD, D_MODEL)
    mod_ctx, mod_lat = mod[0:1], mod[1:1 + dec_batch]

    wi = w_in[0]
    a0 = 4 * DN_W + 4 * DN_HEADS
    win = jnp.concatenate([wi[:, 0:QKV_W], wi[:, QKV_W + 4 * DN_HEADS:a0], wi[:, a0:],
                           wi[:, QKV_W:QKV_W + 4 * DN_HEADS],
                           jnp.zeros((D_MODEL, LANES - 4 * DN_HEADS), F32)], axis=1).astype(BF16)
    pad8 = lambda v: jnp.zeros((1, LANES), F32).at[0, 0:n_state].set(v.reshape(n_state))
    p = dict(norm_g=norm_g[0], wgu1=ffn1_w_in[0].astype(BF16), wd1=ffn1_w_out[0].astype(BF16), win=win,
             dn_conv_w=dn_conv_w[0], alog_row=pad8(dn_a_log[0]), dtb_row=pad8(dn_dt_bias[0]),
             dn_norm_g=dn_norm_g[0].reshape(1, DN_DK), cv_dw_w=cv_dw_w[0],
             cv_dw_b=cv_dw_b[0].reshape(1, CV_W), cv_ln_g=cv_ln_g[0].reshape(1, CV_W),
             cv_ln_b=cv_ln_b[0].reshape(1, CV_W), wout=w_out[0].astype(BF16),
             wgu2=ffn2_w_in[0].astype(BF16), wd2=ffn2_w_out[0].astype(BF16))

    s_zero = jnp.zeros((batch, n_state, DN_DK, DN_DK), F32)
    y_p, s_ctx = _layer(x_prompt, mod_ctx, s_zero, seq_len, p)
    s_lat = state_delta[:, 0].reshape(dec_batch, n_state, DN_DK, DN_DK)
    y_s, _ = _layer(x_sample, mod_lat, s_lat, GRID_W, p)
    new_state = s_ctx.reshape(batch, 1, 2, DN_HEADS, DN_DK, DN_DK).astype(x_prompt.dtype)
    return (y_p, y_s, new_state)
```

```python
import functools

import jax
import jax.numpy as jnp
from jax import lax
from jax.experimental import pallas as pl
from jax.experimental.pallas import tpu as pltpu

F32 = jnp.float32
BF16 = jnp.bfloat16

D_MODEL = 1024
D_FF = 2816
N_MOD = 9
GRID_W = 64
DN_W = 512
CV_W = 512
DN_HEADS = 4
DN_DK = 128
CHUNK = 64
SHORT_CONV = 3
CONV_K = 31
EPS = 1e-6

LANES = 128
SUBLANES = 8
BLOCK_T = 256
PAIR_T = 2 * CHUNK
CTX_SEQS_PER_STEP = 1
TOKEN_TILE = 256
FFN_TILE = 512
CONV_HALO = 16
CONV_ROWS = 32
VMEM_LIMIT_BYTES = 56 * 1024 * 1024
FF_CHUNKS = ((0, 768), (768, 768), (1536, 768), (2304, 512))
QKV_W = 3 * DN_W
IN_QKV = (0, QKV_W)
IN_Z = (QKV_W, QKV_W + DN_W)
IN_GLU = (QKV_W + DN_W, QKV_W + DN_W + 2 * CV_W)
IN_AB = (QKV_W + DN_W + 2 * CV_W, QKV_W + DN_W + 2 * CV_W + LANES)
IN_COLS_PADDED = IN_AB[1]


def _dot(a, b):
    return jnp.dot(a, b, preferred_element_type=F32)


def _dot_nt(a, b):
    return lax.dot_general(a, b, (((1,), (1,)), ((), ())), preferred_element_type=F32)


def _dot_tn(a, b):
    return lax.dot_general(a, b, (((0,), (0,)), ((), ())), preferred_element_type=F32)


def _silu(x):
    return x * jax.nn.sigmoid(x)


def _rms(x, g):
    return x * lax.rsqrt(jnp.mean(x * x, axis=-1, keepdims=True) + EPS) * g


def _const_spec(shape):
    nd = len(shape)
    return pl.BlockSpec(shape, lambda *_: (0,) * nd, pipeline_mode=pl.Buffered(1))


def _params(n_grid_dims):
    return pltpu.CompilerParams(dimension_semantics=("arbitrary",) * n_grid_dims,
                                vmem_limit_bytes=VMEM_LIMIT_BYTES)


def _mod_kernel(c_ref, w_ref, b_ref, o_ref):
    s = _silu(c_ref[...]).astype(BF16)
    o_ref[...] = _dot(s, w_ref[...].astype(BF16)) + b_ref[...]


def _modulation(cond, w_mod, b_mod):
    n_out = w_mod.shape[1]
    tn = D_MODEL
    return pl.pallas_call(
        _mod_kernel,
        grid=(n_out // tn,),
        in_specs=[pl.BlockSpec((8, D_MODEL), lambda j: (0, 0)),
                  pl.BlockSpec((D_MODEL, tn), lambda j: (0, j)),
                  pl.BlockSpec((1, tn), lambda j: (0, j))],
        out_specs=pl.BlockSpec((8, tn), lambda j: (0, j)),
        out_shape=jax.ShapeDtypeStruct((8, n_out), F32),
        compiler_params=_params(1),
        name="modulation",
    )(cond, w_mod, b_mod.reshape(1, n_out))


def _swiglu(hb, wgu_ref, wd_ref):
    acc = None
    for s, n in FF_CHUNKS:
        gt = _dot(hb, wgu_ref[:, s:s + n])
        up = _dot(hb, wgu_ref[:, D_FF + s:D_FF + s + n])
        a = (_silu(gt) * up).astype(BF16)
        p = _dot(a, wd_ref[s:s + n, :])
        acc = p if acc is None else acc + p
    return acc


def _ffn1_proj_kernel(x_ref, mod_ref, g_ref, wgu_ref, wd_ref, win_ref,
                      x1_ref, qkv_ref, z_ref, glu_ref, ab_ref):
    x = x_ref[...]
    m = mod_ref[...]
    g = g_ref[...]
    h = _rms(x, g[0:1]) * (1.0 + m[1:2]) + m[0:1]
    f = _swiglu(h.astype(BF16), wgu_ref, wd_ref)
    x1 = x + 0.5 * (m[2:3] * _rms(f, g[1:2]))
    x1_ref[...] = x1
    h1 = (_rms(x1, g[2:3]) * (1.0 + m[4:5]) + m[3:4]).astype(BF16)
    qkv_ref[...] = _dot(h1, win_ref[:, IN_QKV[0]:IN_QKV[1]])
    z_ref[...] = _dot(h1, win_ref[:, IN_Z[0]:IN_Z[1]])
    glu_ref[...] = _dot(h1, win_ref[:, IN_GLU[0]:IN_GLU[1]])
    ab_ref[...] = _dot(h1, win_ref[:, IN_AB[0]:IN_AB[1]])


def _ffn1_proj(x, mod, norm_g, wgu, wd, win, tiles_per_mod):
    n = x.shape[0]
    tm = FFN_TILE
    row = lambda w: pl.BlockSpec((tm, w), lambda i: (i, 0))
    return pl.pallas_call(
        _ffn1_proj_kernel,
        grid=(n // tm,),
        in_specs=[row(D_MODEL),
                  pl.BlockSpec((None, N_MOD, D_MODEL), lambda i: (i // tiles_per_mod, 0, 0)),
                  _const_spec(norm_g.shape), _const_spec(wgu.shape), _const_spec(wd.shape),
                  _const_spec(win.shape)],
        out_specs=[row(D_MODEL), row(QKV_W), row(DN_W), row(2 * CV_W), row(LANES)],
        out_shape=[jax.ShapeDtypeStruct((n, D_MODEL), F32), jax.ShapeDtypeStruct((n, QKV_W), F32),
                   jax.ShapeDtypeStruct((n, DN_W), F32), jax.ShapeDtypeStruct((n, 2 * CV_W), F32),
                   jax.ShapeDtypeStruct((n, LANES), F32)],
        compiler_params=_params(1),
        name="ffn1_proj",
    )(x, mod, norm_g, wgu, wd, win)


def _out_ffn2_kernel(x1_ref, o_ref, z_ref, cv_ref, mod_ref, g_ref, dng_ref, wout_ref, wgu_ref, wd_ref,
                     y_ref):
    x1 = x1_ref[...]
    m = mod_ref[...]
    g = g_ref[...]
    o = o_ref[...]
    z = z_ref[...]
    dng = dng_ref[...]
    heads = []
    for h in range(DN_HEADS):
        oh = o[:, h * DN_DK:(h + 1) * DN_DK]
        zh = z[:, h * DN_DK:(h + 1) * DN_DK]
        heads.append((_rms(oh, dng) * _silu(zh)).astype(BF16))
    og = jnp.concatenate(heads, axis=-1)
    y = _dot(og, wout_ref[0:DN_W, :]) + _dot(cv_ref[...].astype(BF16), wout_ref[DN_W:, :])
    x2 = x1 + m[5:6] * _rms(y, g[3:4])
    h2 = _rms(x2, g[4:5]) * (1.0 + m[7:8]) + m[6:7]
    f = _swiglu(h2.astype(BF16), wgu_ref, wd_ref)
    y_ref[...] = x2 + 0.5 * (m[8:9] * _rms(f, g[5:6]))


def _out_ffn2(x1, o, z, cv, mod, norm_g, dn_norm_g, wout, wgu, wd, tiles_per_mod):
    n = x1.shape[0]
    tm = FFN_TILE
    row = lambda w: pl.BlockSpec((tm, w), lambda i: (i, 0))
    return pl.pallas_call(
        _out_ffn2_kernel,
        grid=(n // tm,),
        in_specs=[row(D_MODEL), row(DN_W), row(DN_W), row(CV_W),
                  pl.BlockSpec((None, N_MOD, D_MODEL), lambda i: (i // tiles_per_mod, 0, 0)),
                  _const_spec(norm_g.shape), _const_spec(dn_norm_g.shape), _const_spec(wout.shape),
                  _const_spec(wgu.shape), _const_spec(wd.shape)],
        out_specs=row(D_MODEL),
        out_shape=jax.ShapeDtypeStruct((n, D_MODEL), F32),
        compiler_params=_params(1),
        name="out_ffn2",
    )(x1, o, z, cv, mod, norm_g, dn_norm_g, wout, wgu, wd)


def _conv_module_kernel(glu_ref, w_ref, b_ref, lg_ref, lb_ref, o_ref, xp_ref, *, row_len):
    n_rows = TOKEN_TILE // row_len
    pitch = row_len + 2 * CONV_HALO
    total = n_rows * pitch
    zeros = jnp.zeros((CONV_HALO, LANES), F32)
    for lg in range(CV_W // LANES):
        x = glu_ref[:, lg * LANES:(lg + 1) * LANES] * jax.nn.sigmoid(
            glu_ref[:, CV_W + lg * LANES:CV_W + (lg + 1) * LANES])
        for r in range(n_rows):
            base = r * pitch
            xp_ref[0, lg, base:base + CONV_HALO, :] = zeros
            xp_ref[0, lg, base + CONV_HALO:base + CONV_HALO + row_len, :] = x[r * row_len:(r + 1) * row_len, :]
            xp_ref[0, lg, base + CONV_HALO + row_len:base + pitch, :] = zeros
        for b in range(1, SUBLANES):
            xp_ref[b, lg, 0:total - SUBLANES, :] = xp_ref[0, lg, b:b + total - SUBLANES, :]
    sub = CONV_ROWS
    subs_per_row = row_len // sub
    lead = CONV_HALO - CONV_K // 2

    def row_group(i, carry):
        out0 = pl.multiple_of(i * sub, sub)
        base = out0 + (i // subs_per_row) * (2 * CONV_HALO)
        for lg in range(CV_W // LANES):
            ls = slice(lg * LANES, (lg + 1) * LANES)
            acc = jnp.zeros((sub, LANES), F32) + b_ref[:, ls]
            for j in range(CONV_K):
                shift = (lead + j) % SUBLANES
                start = pl.multiple_of(base + (lead + j - shift), SUBLANES)
                acc = acc + w_ref[j:j + 1, ls] * xp_ref[shift, lg, pl.ds(start, sub), :]
            o_ref[pl.ds(out0, sub), ls] = acc
        return carry

    lax.fori_loop(0, TOKEN_TILE // sub, row_group, 0)
    y = o_ref[...]
    mu = jnp.mean(y, axis=-1, keepdims=True)
    var = jnp.mean(jnp.square(y - mu), axis=-1, keepdims=True)
    yn = (y - mu) * lax.rsqrt(var + EPS) * lg_ref[...] + lb_ref[...]
    o_ref[...] = _silu(yn)


def _conv_module(glu, w, b, ln_g, ln_b, row_len):
    n = glu.shape[0]
    tm = TOKEN_TILE
    pitch = row_len + 2 * CONV_HALO
    return pl.pallas_call(
        functools.partial(_conv_module_kernel, row_len=row_len),
        grid=(n // tm,),
        in_specs=[pl.BlockSpec((tm, 2 * CV_W), lambda i: (i, 0)),
                  _const_spec(w.shape), _const_spec(b.shape), _const_spec(ln_g.shape),
                  _const_spec(ln_b.shape)],
        out_specs=pl.BlockSpec((tm, CV_W), lambda i: (i, 0)),
        out_shape=jax.ShapeDtypeStruct((n, CV_W), F32),
        scratch_shapes=[pltpu.VMEM((SUBLANES, CV_W // LANES, (tm // row_len) * pitch, LANES), F32)],
        compiler_params=_params(1),
        name="conv_module",
    )(glu, w, b, ln_g, ln_b)


def _deltanet_kernel(*refs, seq_len, row_len, n_seq, zero_init):
    if zero_init:
        qkv_ref, ab_ref, cw_ref, alog_ref, dtb_ref, o_ref, sfin_ref = refs[:7]
        s0_ref = None
    else:
        qkv_ref, ab_ref, s0_ref, cw_ref, alog_ref, dtb_ref, o_ref, sfin_ref = refs[:8]
    act_ref, gc_ref, tot_ref, beta_ref, st_ref = refs[-5:]
    n_blk = seq_len // BLOCK_T
    bt = BLOCK_T
    f32_inf = jnp.float32(jnp.inf)

    rowi = lax.broadcasted_iota(jnp.int32, (bt, LANES), 0)
    lanei = lax.broadcasted_iota(jnp.int32, (bt, LANES), 1)
    pos_row = jnp.bitwise_and(rowi, row_len - 1)
    keep_prev = pos_row != 0
    keep_next = pos_row != row_len - 1
    pos_chunk = jnp.bitwise_and(rowi, CHUNK - 1)

    def prep_block(b, carry):
        r0 = pl.multiple_of(b * bt, bt)
        rows = pl.ds(r0, bt)
        for g in range(n_seq):
            for s in range(QKV_W // LANES):
                ls = slice(s * LANES, (s + 1) * LANES)
                xs = qkv_ref[g, rows, ls]
                w = cw_ref[:, ls]
                xp = jnp.where(keep_prev, pltpu.roll(xs, 1, 0), 0.0)
                xn = jnp.where(keep_next, pltpu.roll(xs, bt - 1, 0), 0.0)
                y = _silu(w[0:1] * xp + w[1:2] * xs + w[2:3] * xn)
                if s < 2 * DN_HEADS:
                    y = y * lax.rsqrt(jnp.sum(y * y, axis=-1, keepdims=True) + EPS)
                if s < DN_HEADS:
                    y = y * (DN_DK ** -0.5)
                act_ref[g, rows, ls] = y
            ab = ab_ref[g, rows, :]
            t = ab + dtb_ref[...]
            softplus = jnp.maximum(t, 0.0) + jnp.log1p(jnp.exp(-jnp.abs(t)))
            gate = -jnp.exp(alog_ref[...]) * softplus
            pre = gate
            suf = gate
            for sh in (1, 2, 4, 8, 16, 32):
                pre = pre + jnp.where(pos_chunk >= sh, pltpu.roll(pre, sh, 0), 0.0)
                suf = suf + jnp.where(pos_chunk < CHUNK - sh, pltpu.roll(suf, bt - sh, 0), 0.0)
            gc_ref[g, rows, :] = jnp.where(lanei < DN_HEADS, pre, suf)
            tot_ref[g, rows, :] = pre + suf - gate
            beta_ref[g, rows, :] = jax.nn.sigmoid(ab)
        return carry

    lax.fori_loop(0, n_blk, prep_block, 0)

    n_state = 2 * DN_HEADS
    if zero_init:
        st_ref[...] = jnp.zeros((n_seq, n_state, DN_DK, DN_DK), F32)
    else:
        st_ref[...] = s0_ref[...]
    if n_blk > 1:
        o_ref[...] = jnp.zeros((n_seq, seq_len, DN_W), F32)

    pt = PAIR_T
    n_pair = bt // pt
    n_chunk = bt // CHUNK
    ri = lax.broadcasted_iota(jnp.int32, (pt, pt), 0)
    ci = lax.broadcasted_iota(jnp.int32, (pt, pt), 1)
    same = jnp.right_shift(ri, 6) == jnp.right_shift(ci, 6)
    incl = (same & (ri >= ci), same & (ri <= ci))
    strict = (same & (ri > ci), same & (ri < ci))

    def delta_blocks(jobs):
        gram = {}
        probs = {}
        for d, g, r0, key in jobs:
            rows = pl.ds(r0, bt)
            gc = gc_ref[g, rows, :]
            tot = tot_ref[g, rows, :]
            beta = beta_ref[g, rows, :]
            gc_t = jnp.transpose(gc)
            beta_t = jnp.transpose(beta)
            for h in range(DN_HEADS):
                c8 = d * DN_HEADS + h
                for p in range(n_pair):
                    ps = slice(p * pt, (p + 1) * pt)
                    prow = pl.ds(r0 + p * pt, pt)
                    qh = act_ref[g, prow, h * DN_DK:(h + 1) * DN_DK]
                    kh = act_ref[g, prow, DN_W + h * DN_DK:DN_W + (h + 1) * DN_DK]
                    vh = act_ref[g, prow, 2 * DN_W + h * DN_DK:2 * DN_W + (h + 1) * DN_DK]
                    if (key, h, p) not in gram:
                        khb = kh.astype(BF16)
                        gram[(key, h, p)] = _dot_nt(jnp.concatenate([khb, qh.astype(BF16)], axis=0), khb)
                    gr = gram[(key, h, p)]
                    kk = gr[0:pt]
                    qk = gr[pt:2 * pt]
                    g_i = gc[ps, c8:c8 + 1]
                    diff = g_i - gc_t[c8:c8 + 1, ps]
                    dec = jnp.exp(jnp.where(incl[d], diff, -f32_inf))
                    dec_t = jnp.exp(jnp.where(incl[1 - d], -diff, -f32_inf))
                    b_i = beta[ps, n_state + c8:n_state + c8 + 1]
                    b_j = beta_t[n_state + c8:n_state + c8 + 1, ps]
                    n_t = -(jnp.where(strict[1 - d], kk * dec_t, 0.0) * b_j)
                    eg = jnp.exp(g_i)
                    probs[(d, g, h, p)] = dict(
                        pt=n_t, qt=n_t, qkm=(qk * dec).astype(BF16),
                        x=jnp.concatenate([vh * b_i, kh * (b_i * eg)], axis=-1),
                        qd=qh * eg, kd=kh * jnp.exp(tot[ps, c8:c8 + 1] - g_i), tot=tot[ps, c8:c8 + 1])
        for pr in probs.values():
            pb = pr["pt"].astype(BF16)
            pr["pt"] = _dot(pb, pb)
        for _ in range(4):
            for pr in probs.values():
                pb = pr["pt"].astype(BF16)
                r = _dot(pb, jnp.concatenate([pb, pr["qt"].astype(BF16)], axis=-1))
                pr["qt"] = pr["qt"] + pr["pt"] + r[:, pt:2 * pt]
                pr["pt"] = r[:, 0:pt]
        for pr in probs.values():
            r = _dot(pr["pt"].astype(BF16), pr["qt"].astype(BF16))
            qm = jnp.transpose(pr["qt"] + pr["pt"] + r)
            pr["x"] = pr["x"] + _dot(qm.astype(BF16), pr["x"].astype(BF16))
        chains = [(d, g, h) for d, g, _, _ in jobs for h in range(DN_HEADS)]
        state = {(d, g, h): st_ref[g, d * DN_HEADS + h] for d, g, h in chains}
        vnew = {}
        inter = {}
        for step in range(n_chunk):
            ws = {}
            for ch in chains:
                c = step if ch[0] == 0 else n_chunk - 1 - step
                pr = probs[ch + (c // 2,)]
                cs = slice((c % 2) * CHUNK, (c % 2 + 1) * CHUNK)
                lhs = jnp.concatenate([pr["x"][cs, DN_DK:2 * DN_DK], pr["qd"][cs]], axis=0).astype(BF16)
                ws[ch] = _dot(lhs, state[ch].astype(BF16))
            for ch in chains:
                c = step if ch[0] == 0 else n_chunk - 1 - step
                pr = probs[ch + (c // 2,)]
                cs = slice((c % 2) * CHUNK, (c % 2 + 1) * CHUNK)
                vn = pr["x"][cs, 0:DN_DK] - ws[ch][0:CHUNK]
                vnew[ch + (c,)] = vn
                inter[ch + (c,)] = ws[ch][CHUNK:2 * CHUNK]
                g_last = jnp.exp(pr["tot"][(c % 2) * CHUNK:(c % 2) * CHUNK + 1])
                state[ch] = state[ch] * g_last + _dot_tn(pr["kd"][cs].astype(BF16), vn.astype(BF16))
        outs = {}
        for ch in chains:
            d, g, h = ch
            st_ref[g, d * DN_HEADS + h] = state[ch]
            o_pairs = []
            for p in range(n_pair):
                vn = jnp.concatenate([vnew[ch + (2 * p,)], vnew[ch + (2 * p + 1,)]], axis=0)
                it = jnp.concatenate([inter[ch + (2 * p,)], inter[ch + (2 * p + 1,)]], axis=0)
                o_pairs.append(it + _dot(probs[ch + (p,)]["qkm"], vn.astype(BF16)))
            outs[ch] = jnp.concatenate(o_pairs, axis=0)
        return outs

    if n_blk == 1:
        outs = delta_blocks([(d, g, 0, g) for g in range(n_seq) for d in range(2)])
        for g in range(n_seq):
            for h in range(DN_HEADS):
                o_ref[g, :, h * DN_DK:(h + 1) * DN_DK] = outs[(0, g, h)] + outs[(1, g, h)]
    else:
        def scan_block(b, carry):
            r0 = (pl.multiple_of(b * bt, bt), pl.multiple_of((n_blk - 1 - b) * bt, bt))
            outs = delta_blocks([(d, g, r0[d], (g, d)) for g in range(n_seq) for d in range(2)])
            for (d, g, h), o in outs.items():
                o_ref[g, pl.ds(r0[d], bt), h * DN_DK:(h + 1) * DN_DK] += o
            return carry

        lax.fori_loop(0, n_blk, scan_block, 0)

    sfin_ref[...] = st_ref[...]


def _deltanet(qkv, ab, s0, conv_w, alog_row, dtb_row, row_len, n_seq):
    nb, seq_len, _ = qkv.shape
    n_state = 2 * DN_HEADS
    zero_init = s0 is None
    seq = lambda w, **kw: pl.BlockSpec((n_seq, seq_len, w), lambda b: (b, 0, 0), **kw)
    state = pl.BlockSpec((n_seq, n_state, DN_DK, DN_DK), lambda b: (b, 0, 0, 0))
    qkv_mode = dict(pipeline_mode=pl.Buffered(1)) if seq_len * n_seq > 4 * BLOCK_T else {}
    consts = [_const_spec(conv_w.shape), _const_spec(alog_row.shape), _const_spec(dtb_row.shape)]
    operands = (qkv, ab) + (() if zero_init else (s0,)) + (conv_w, alog_row, dtb_row)
    return pl.pallas_call(
        functools.partial(_deltanet_kernel, seq_len=seq_len, row_len=row_len, n_seq=n_seq,
                          zero_init=zero_init),
        grid=(nb // n_seq,),
        in_specs=[seq(QKV_W, **qkv_mode), seq(LANES)] + ([] if zero_init else [state]) + consts,
        out_specs=[seq(DN_W), state],
        out_shape=[jax.ShapeDtypeStruct((nb, seq_len, DN_W), F32),
                   jax.ShapeDtypeStruct((nb, n_state, DN_DK, DN_DK), F32)],
        scratch_shapes=[pltpu.VMEM((n_seq, seq_len, QKV_W), F32), pltpu.VMEM((n_seq, seq_len, LANES), F32),
                        pltpu.VMEM((n_seq, seq_len, LANES), F32), pltpu.VMEM((n_seq, seq_len, LANES), F32),
                        pltpu.VMEM((n_seq, n_state, DN_DK, DN_DK), F32)],
        compiler_params=_params(1),
        name="deltanet",
    )(*operands)


def _layer(x, mod, s0, row_len, n_seq, p):
    nb, seq_len, _ = x.shape
    n = nb * seq_len
    tiles_per_mod = (n // mod.shape[0]) // FFN_TILE
    x1, qkv, z, glu, ab = _ffn1_proj(x.reshape(n, D_MODEL), mod, p["norm_g"], p["wgu1"], p["wd1"],
                                     p["win"], tiles_per_mod)
    o, s_fin = _deltanet(qkv.reshape(nb, seq_len, QKV_W), ab.reshape(nb, seq_len, LANES), s0,
                         p["dn_conv_w"], p["alog_row"], p["dtb_row"], row_len, n_seq)
    cv = _conv_module(glu, p["cv_dw_w"], p["cv_dw_b"], p["cv_ln_g"], p["cv_ln_b"], row_len)
    y = _out_ffn2(x1, o.reshape(n, DN_W), z, cv, mod, p["norm_g"], p["dn_norm_g"], p["wout"],
                  p["wgu2"], p["wd2"], tiles_per_mod)
    return y.reshape(nb, seq_len, D_MODEL), s_fin


def kernel(x_prompt, x_sample, state_delta, c, c_ctx, w_mod, b_mod, norm_g, ffn1_w_in, ffn1_w_out, w_in,
           dn_conv_w, dn_a_log, dn_dt_bias, dn_norm_g, cv_dw_w, cv_dw_b, cv_ln_g, cv_ln_b, w_out,
           ffn2_w_in, ffn2_w_out):
    depth = w_mod.shape[0]
    assert depth == 1, "one trunk layer"
    batch, seq_len, _ = x_prompt.shape
    dec_batch, dec_seq, _ = x_sample.shape
    n_state = 2 * DN_HEADS

    cond = jnp.zeros((8, D_MODEL), F32).at[0].set(c_ctx).at[1:1 + dec_batch].set(c)
    mod = _modulation(cond, w_mod[0], b_mod[0]).reshape(8, N_MOD, D_MODEL)
    mod_ctx, mod_lat = mod[0:1], mod[1:1 + dec_batch]

    wi = w_in[0]
    a0 = 4 * DN_W + 4 * DN_HEADS
    win = jnp.concatenate([wi[:, 0:QKV_W], wi[:, QKV_W + 4 * DN_HEADS:a0], wi[:, a0:],
                           wi[:, QKV_W:QKV_W + 4 * DN_HEADS],
                           jnp.zeros((D_MODEL, LANES - 4 * DN_HEADS), F32)], axis=1).astype(BF16)
    pad8 = lambda v: jnp.zeros((1, LANES), F32).at[0, 0:n_state].set(v.reshape(n_state))
    p = dict(norm_g=norm_g[0], wgu1=ffn1_w_in[0].astype(BF16), wd1=ffn1_w_out[0].astype(BF16), win=win,
             dn_conv_w=dn_conv_w[0], alog_row=pad8(dn_a_log[0]), dtb_row=pad8(dn_dt_bias[0]),
             dn_norm_g=dn_norm_g[0].reshape(1, DN_DK), cv_dw_w=cv_dw_w[0],
             cv_dw_b=cv_dw_b[0].reshape(1, CV_W), cv_ln_g=cv_ln_g[0].reshape(1, CV_W),
             cv_ln_b=cv_ln_b[0].reshape(1, CV_W), wout=w_out[0].astype(BF16),
             wgu2=ffn2_w_in[0].astype(BF16), wd2=ffn2_w_out[0].astype(BF16))

    y_p, s_ctx = _layer(x_prompt, mod_ctx, None, seq_len, CTX_SEQS_PER_STEP, p)
    s_lat = state_delta[:, 0].reshape(dec_batch, n_state, DN_DK, DN_DK)
    y_s, _ = _layer(x_sample, mod_lat, s_lat, GRID_W, 1, p)
    new_state = s_ctx.reshape(batch, 1, 2, DN_HEADS, DN_DK, DN_DK).astype(x_prompt.dtype)
    return (y_p, y_s, new_state)
```

```python
import functools

import jax
import jax.numpy as jnp
from jax import lax
from jax.experimental import pallas as pl
from jax.experimental.pallas import tpu as pltpu

F32 = jnp.float32
BF16 = jnp.bfloat16

D_MODEL = 1024
D_FF = 2816
N_MOD = 9
GRID_W = 64
DN_W = 512
CV_W = 512
DN_HEADS = 4
DN_DK = 128
CHUNK = 64
SHORT_CONV = 3
CONV_K = 31
EPS = 1e-6

LANES = 128
SUBLANES = 8
BLOCK_T = 256
PAIR_T = 2 * CHUNK
CTX_SEQS_PER_STEP = 1
TOKEN_TILE = 256
FFN_TILE = 512
CONV_HALO = 16
CONV_ROWS = 32
VMEM_LIMIT_BYTES = 56 * 1024 * 1024
FF_CHUNKS = ((0, 768), (768, 768), (1536, 768), (2304, 512))
QKV_W = 3 * DN_W
IN_QKV = (0, QKV_W)
IN_Z = (QKV_W, QKV_W + DN_W)
IN_GLU = (QKV_W + DN_W, QKV_W + DN_W + 2 * CV_W)
IN_AB = (QKV_W + DN_W + 2 * CV_W, QKV_W + DN_W + 2 * CV_W + LANES)
IN_COLS_PADDED = IN_AB[1]


def _dot(a, b):
    return jnp.dot(a, b, preferred_element_type=F32)


def _dot_nt(a, b):
    return lax.dot_general(a, b, (((1,), (1,)), ((), ())), preferred_element_type=F32)


def _dot_tn(a, b):
    return lax.dot_general(a, b, (((0,), (0,)), ((), ())), preferred_element_type=F32)


def _silu(x):
    return x * jax.nn.sigmoid(x)


def _rms(x, g):
    return x * lax.rsqrt(jnp.mean(x * x, axis=-1, keepdims=True) + EPS) * g


def _const_spec(shape):
    nd = len(shape)
    return pl.BlockSpec(shape, lambda *_: (0,) * nd, pipeline_mode=pl.Buffered(1))


def _params(n_grid_dims):
    return pltpu.CompilerParams(dimension_semantics=("arbitrary",) * n_grid_dims,
                                vmem_limit_bytes=VMEM_LIMIT_BYTES)


def _mod_kernel(c_ref, w_ref, b_ref, o_ref):
    s = _silu(c_ref[...]).astype(BF16)
    o_ref[...] = _dot(s, w_ref[...].astype(BF16)) + b_ref[...]


def _modulation(cond, w_mod, b_mod):
    n_out = w_mod.shape[1]
    tn = D_MODEL
    return pl.pallas_call(
        _mod_kernel,
        grid=(n_out // tn,),
        in_specs=[pl.BlockSpec((8, D_MODEL), lambda j: (0, 0)),
                  pl.BlockSpec((D_MODEL, tn), lambda j: (0, j)),
                  pl.BlockSpec((1, tn), lambda j: (0, j))],
        out_specs=pl.BlockSpec((8, tn), lambda j: (0, j)),
        out_shape=jax.ShapeDtypeStruct((8, n_out), F32),
        compiler_params=_params(1),
        name="modulation",
    )(cond, w_mod, b_mod.reshape(1, n_out))


def _swiglu(hb, wgu_ref, wd_ref):
    acc = None
    for s, n in FF_CHUNKS:
        gt = _dot(hb, wgu_ref[:, s:s + n])
        up = _dot(hb, wgu_ref[:, D_FF + s:D_FF + s + n])
        a = (_silu(gt) * up).astype(BF16)
        p = _dot(a, wd_ref[s:s + n, :])
        acc = p if acc is None else acc + p
    return acc


def _ffn1_proj_kernel(x_ref, mod_ref, g_ref, wgu_ref, wd_ref, win_ref,
                      x1_ref, qkv_ref, z_ref, glu_ref, ab_ref):
    x = x_ref[...]
    m = mod_ref[...]
    g = g_ref[...]
    h = _rms(x, g[0:1]) * (1.0 + m[1:2]) + m[0:1]
    f = _swiglu(h.astype(BF16), wgu_ref, wd_ref)
    x1 = x + 0.5 * (m[2:3] * _rms(f, g[1:2]))
    x1_ref[...] = x1
    h1 = (_rms(x1, g[2:3]) * (1.0 + m[4:5]) + m[3:4]).astype(BF16)
    qkv_ref[...] = _dot(h1, win_ref[:, IN_QKV[0]:IN_QKV[1]])
    z_ref[...] = _dot(h1, win_ref[:, IN_Z[0]:IN_Z[1]])
    glu_ref[...] = _dot(h1, win_ref[:, IN_GLU[0]:IN_GLU[1]])
    ab_ref[...] = _dot(h1, win_ref[:, IN_AB[0]:IN_AB[1]])


def _ffn1_proj(x, mod, norm_g, wgu, wd, win, tiles_per_mod):
    n = x.shape[0]
    tm = FFN_TILE
    row = lambda w: pl.BlockSpec((tm, w), lambda i: (i, 0))
    return pl.pallas_call(
        _ffn1_proj_kernel,
        grid=(n // tm,),
        in_specs=[row(D_MODEL),
                  pl.BlockSpec((None, N_MOD, D_MODEL), lambda i: (i // tiles_per_mod, 0, 0)),
                  _const_spec(norm_g.shape), _const_spec(wgu.shape), _const_spec(wd.shape),
                  _const_spec(win.shape)],
        out_specs=[row(D_MODEL), row(QKV_W), row(DN_W), row(2 * CV_W), row(LANES)],
        out_shape=[jax.ShapeDtypeStruct((n, D_MODEL), F32), jax.ShapeDtypeStruct((n, QKV_W), F32),
                   jax.ShapeDtypeStruct((n, DN_W), F32), jax.ShapeDtypeStruct((n, 2 * CV_W), F32),
                   jax.ShapeDtypeStruct((n, LANES), F32)],
        compiler_params=_params(1),
        name="ffn1_proj",
    )(x, mod, norm_g, wgu, wd, win)


def _out_ffn2_kernel(x1_ref, o_ref, z_ref, cv_ref, mod_ref, g_ref, dng_ref, wout_ref, wgu_ref, wd_ref,
                     y_ref):
    x1 = x1_ref[...]
    m = mod_ref[...]
    g = g_ref[...]
    o = o_ref[...]
    z = z_ref[...]
    dng = dng_ref[...]
    heads = []
    for h in range(DN_HEADS):
        oh = o[:, h * DN_DK:(h + 1) * DN_DK]
        zh = z[:, h * DN_DK:(h + 1) * DN_DK]
        heads.append((_rms(oh, dng) * _silu(zh)).astype(BF16))
    og = jnp.concatenate(heads, axis=-1)
    y = _dot(og, wout_ref[0:DN_W, :]) + _dot(cv_ref[...].astype(BF16), wout_ref[DN_W:, :])
    x2 = x1 + m[5:6] * _rms(y, g[3:4])
    h2 = _rms(x2, g[4:5]) * (1.0 + m[7:8]) + m[6:7]
    f = _swiglu(h2.astype(BF16), wgu_ref, wd_ref)
    y_ref[...] = x2 + 0.5 * (m[8:9] * _rms(f, g[5:6]))


def _out_ffn2(x1, o, z, cv, mod, norm_g, dn_norm_g, wout, wgu, wd, tiles_per_mod):
    n = x1.shape[0]
    tm = FFN_TILE
    row = lambda w: pl.BlockSpec((tm, w), lambda i: (i, 0))
    return pl.pallas_call(
        _out_ffn2_kernel,
        grid=(n // tm,),
        in_specs=[row(D_MODEL), row(DN_W), row(DN_W), row(CV_W),
                  pl.BlockSpec((None, N_MOD, D_MODEL), lambda i: (i // tiles_per_mod, 0, 0)),
                  _const_spec(norm_g.shape), _const_spec(dn_norm_g.shape), _const_spec(wout.shape),
                  _const_spec(wgu.shape), _const_spec(wd.shape)],
        out_specs=row(D_MODEL),
        out_shape=jax.ShapeDtypeStruct((n, D_MODEL), F32),
        compiler_params=_params(1),
        name="out_ffn2",
    )(x1, o, z, cv, mod, norm_g, dn_norm_g, wout, wgu, wd)


def _conv_module_kernel(glu_ref, w_ref, b_ref, lg_ref, lb_ref, o_ref, xp_ref, *, row_len):
    n_rows = TOKEN_TILE // row_len
    pitch = row_len + 2 * CONV_HALO
    total = n_rows * pitch
    zeros = jnp.zeros((CONV_HALO, LANES), F32)
    for lg in range(CV_W // LANES):
        x = glu_ref[:, lg * LANES:(lg + 1) * LANES] * jax.nn.sigmoid(
            glu_ref[:, CV_W + lg * LANES:CV_W + (lg + 1) * LANES])
        for r in range(n_rows):
            base = r * pitch
            xp_ref[0, lg, base:base + CONV_HALO, :] = zeros
            xp_ref[0, lg, base + CONV_HALO:base + CONV_HALO + row_len, :] = x[r * row_len:(r + 1) * row_len, :]
            xp_ref[0, lg, base + CONV_HALO + row_len:base + pitch, :] = zeros
        for b in range(1, SUBLANES):
            xp_ref[b, lg, 0:total - SUBLANES, :] = xp_ref[0, lg, b:b + total - SUBLANES, :]
    sub = CONV_ROWS
    subs_per_row = row_len // sub
    lead = CONV_HALO - CONV_K // 2

    def row_group(i, carry):
        out0 = pl.multiple_of(i * sub, sub)
        base = out0 + (i // subs_per_row) * (2 * CONV_HALO)
        for lg in range(CV_W // LANES):
            ls = slice(lg * LANES, (lg + 1) * LANES)
            acc = jnp.zeros((sub, LANES), F32) + b_ref[:, ls]
            for j in range(CONV_K):
                shift = (lead + j) % SUBLANES
                start = pl.multiple_of(base + (lead + j - shift), SUBLANES)
                acc = acc + w_ref[j:j + 1, ls] * xp_ref[shift, lg, pl.ds(start, sub), :]
            o_ref[pl.ds(out0, sub), ls] = acc
        return carry

    lax.fori_loop(0, TOKEN_TILE // sub, row_group, 0)
    y = o_ref[...]
    mu = jnp.mean(y, axis=-1, keepdims=True)
    var = jnp.mean(jnp.square(y - mu), axis=-1, keepdims=True)
    yn = (y - mu) * lax.rsqrt(var + EPS) * lg_ref[...] + lb_ref[...]
    o_ref[...] = _silu(yn)


def _conv_module(glu, w, b, ln_g, ln_b, row_len):
    n = glu.shape[0]
    tm = TOKEN_TILE
    pitch = row_len + 2 * CONV_HALO
    return pl.pallas_call(
        functools.partial(_conv_module_kernel, row_len=row_len),
        grid=(n // tm,),
        in_specs=[pl.BlockSpec((tm, 2 * CV_W), lambda i: (i, 0)),
                  _const_spec(w.shape), _const_spec(b.shape), _const_spec(ln_g.shape),
                  _const_spec(ln_b.shape)],
        out_specs=pl.BlockSpec((tm, CV_W), lambda i: (i, 0)),
        out_shape=jax.ShapeDtypeStruct((n, CV_W), F32),
        scratch_shapes=[pltpu.VMEM((SUBLANES, CV_W // LANES, (tm // row_len) * pitch, LANES), F32)],
        compiler_params=_params(1),
        name="conv_module",
    )(glu, w, b, ln_g, ln_b)


def _deltanet_kernel(*refs, seq_len, row_len, n_seq, zero_init):
    if zero_init:
        qkv_ref, ab_ref, cw_ref, alog_ref, dtb_ref, o_ref, sfin_ref = refs[:7]
        s0_ref = None
    else:
        qkv_ref, ab_ref, s0_ref, cw_ref, alog_ref, dtb_ref, o_ref, sfin_ref = refs[:8]
    act_ref, gc_ref, tot_ref, beta_ref, st_ref = refs[-5:]
    n_blk = seq_len // BLOCK_T
    bt = BLOCK_T
    f32_inf = jnp.float32(jnp.inf)

    rowi = lax.broadcasted_iota(jnp.int32, (bt, LANES), 0)
    lanei = lax.broadcasted_iota(jnp.int32, (bt, LANES), 1)
    pos_row = jnp.bitwise_and(rowi, row_len - 1)
    keep_prev = pos_row != 0
    keep_next = pos_row != row_len - 1
    pos_chunk = jnp.bitwise_and(rowi, CHUNK - 1)

    def prep_block(b, carry):
        r0 = pl.multiple_of(b * bt, bt)
        rows = pl.ds(r0, bt)
        for g in range(n_seq):
            for s in range(QKV_W // LANES):
                ls = slice(s * LANES, (s + 1) * LANES)
                xs = qkv_ref[g, rows, ls]
                w = cw_ref[:, ls]
                xp = jnp.where(keep_prev, pltpu.roll(xs, 1, 0), 0.0)
                xn = jnp.where(keep_next, pltpu.roll(xs, bt - 1, 0), 0.0)
                y = _silu(w[0:1] * xp + w[1:2] * xs + w[2:3] * xn)
                if s < 2 * DN_HEADS:
                    y = y * lax.rsqrt(jnp.sum(y * y, axis=-1, keepdims=True) + EPS)
                if s < DN_HEADS:
                    y = y * (DN_DK ** -0.5)
                act_ref[g, rows, ls] = y
            ab = ab_ref[g, rows, :]
            t = ab + dtb_ref[...]
            softplus = jnp.maximum(t, 0.0) + jnp.log1p(jnp.exp(-jnp.abs(t)))
            gate = -jnp.exp(alog_ref[...]) * softplus
            pre = gate
            suf = gate
            for sh in (1, 2, 4, 8, 16, 32):
                pre = pre + jnp.where(pos_chunk >= sh, pltpu.roll(pre, sh, 0), 0.0)
                suf = suf + jnp.where(pos_chunk < CHUNK - sh, pltpu.roll(suf, bt - sh, 0), 0.0)
            gc_ref[g, rows, :] = jnp.where(lanei < DN_HEADS, pre, suf)
            tot_ref[g, rows, :] = pre + suf - gate
            beta_ref[g, rows, :] = jax.nn.sigmoid(ab)
        return carry

    lax.fori_loop(0, n_blk, prep_block, 0)

    n_state = 2 * DN_HEADS
    if zero_init:
        st_ref[...] = jnp.zeros((n_seq, n_state, DN_DK, DN_DK), F32)
    else:
        st_ref[...] = s0_ref[...]
    if n_blk > 1:
        o_ref[...] = jnp.zeros((n_seq, seq_len, DN_W), F32)

    state_is_zero = zero_init and n_blk == 1
    pt = PAIR_T
    n_pair = bt // pt
    n_chunk = bt // CHUNK
    ri = lax.broadcasted_iota(jnp.int32, (pt, pt), 0)
    ci = lax.broadcasted_iota(jnp.int32, (pt, pt), 1)
    same = jnp.right_shift(ri, 6) == jnp.right_shift(ci, 6)
    incl = (same & (ri >= ci), same & (ri <= ci))
    strict = (same & (ri > ci), same & (ri < ci))
    first_chunk_lanes = ci < CHUNK

    def delta_blocks(jobs):
        gram = {}
        probs = {}
        for d, g, r0, key in jobs:
            rows = pl.ds(r0, bt)
            gc = gc_ref[g, rows, :]
            tot = tot_ref[g, rows, :]
            beta = beta_ref[g, rows, :]
            gc_t = jnp.transpose(gc)
            tot_t = jnp.transpose(tot)
            for h in range(DN_HEADS):
                c8 = d * DN_HEADS + h
                for p in range(n_pair):
                    ps = slice(p * pt, (p + 1) * pt)
                    prow = pl.ds(r0 + p * pt, pt)
                    qh = act_ref[g, prow, h * DN_DK:(h + 1) * DN_DK]
                    kh = act_ref[g, prow, DN_W + h * DN_DK:DN_W + (h + 1) * DN_DK]
                    vh = act_ref[g, prow, 2 * DN_W + h * DN_DK:2 * DN_W + (h + 1) * DN_DK]
                    if (key, h, p) not in gram:
                        khb = kh.astype(BF16)
                        gram[(key, h, p)] = (_dot_nt(jnp.concatenate([khb, qh.astype(BF16)], axis=0), khb),
                                             jnp.transpose(kh))
                    gr, kh_t = gram[(key, h, p)]
                    kk = gr[0:pt]
                    qk = gr[pt:2 * pt]
                    g_i = gc[ps, c8:c8 + 1]
                    g_j = gc_t[c8:c8 + 1, ps]
                    dec = jnp.exp(jnp.where(incl[d], g_i - g_j, -f32_inf))
                    b_i = beta[ps, n_state + c8:n_state + c8 + 1]
                    eg = jnp.exp(g_i)
                    kd_t = kh_t * jnp.exp(tot_t[c8:c8 + 1, ps] - g_j)
                    probs[(d, g, h, p)] = dict(
                        p=-(jnp.where(strict[d], kk * dec, 0.0) * b_i), qkm=(qk * dec).astype(BF16),
                        x=jnp.concatenate([vh * b_i, kh * (b_i * eg)], axis=-1), qd=qh * eg,
                        kd_t=[jnp.where(first_chunk_lanes, kd_t, 0.0).astype(BF16),
                              jnp.where(first_chunk_lanes, 0.0, kd_t).astype(BF16)],
                        tot=tot[ps, c8:c8 + 1])
        for pr in probs.values():
            pb = pr["p"].astype(BF16)
            pr["q"] = pr["p"]
            pr["p"] = _dot(pb, pb)
        for _ in range(4):
            for pr in probs.values():
                pb = pr["p"].astype(BF16)
                r = _dot(pb, jnp.concatenate([pb, pr["q"].astype(BF16)], axis=-1))
                pr["q"] = pr["q"] + pr["p"] + r[:, pt:2 * pt]
                pr["p"] = r[:, 0:pt]
        for pr in probs.values():
            r = _dot(pr["p"].astype(BF16), pr["q"].astype(BF16))
            qm = pr["q"] + pr["p"] + r
            pr["x"] = pr["x"] + _dot(qm.astype(BF16), pr["x"].astype(BF16))
        for pr in probs.values():
            xb = pr["x"].astype(BF16)
            qx = _dot(pr["qkm"], xb)
            pr["o0"] = qx[:, 0:DN_DK]
            pr["e"] = pr["qd"] - qx[:, DN_DK:2 * DN_DK]
            pr["kx"] = [_dot(kd_t, xb) for kd_t in pr["kd_t"]]
        chains = [(d, g, h) for d, g, _, _ in jobs for h in range(DN_HEADS)]
        state = {(d, g, h): (None if state_is_zero else st_ref[g, d * DN_HEADS + h]) for d, g, h in chains}
        out_c = {}
        for step in range(n_chunk):
            res = {}
            for ch in chains:
                if state[ch] is None:
                    continue
                c = step if ch[0] == 0 else n_chunk - 1 - step
                pr = probs[ch + (c // 2,)]
                cs = slice((c % 2) * CHUNK, (c % 2 + 1) * CHUNK)
                lhs = jnp.concatenate([-pr["kx"][c % 2][:, DN_DK:2 * DN_DK], pr["e"][cs]], axis=0)
                res[ch] = _dot(lhs.astype(BF16), state[ch].astype(BF16))
            for ch in chains:
                c = step if ch[0] == 0 else n_chunk - 1 - step
                pr = probs[ch + (c // 2,)]
                cs = slice((c % 2) * CHUNK, (c % 2 + 1) * CHUNK)
                b_c = pr["kx"][c % 2][:, 0:DN_DK]
                if state[ch] is None:
                    out_c[ch + (c,)] = pr["o0"][cs]
                    state[ch] = b_c
                else:
                    g_last = jnp.exp(pr["tot"][(c % 2) * CHUNK:(c % 2) * CHUNK + 1])
                    out_c[ch + (c,)] = res[ch][DN_DK:DN_DK + CHUNK] + pr["o0"][cs]
                    state[ch] = state[ch] * g_last + res[ch][0:DN_DK] + b_c
        outs = {}
        for ch in chains:
            d, g, h = ch
            st_ref[g, d * DN_HEADS + h] = state[ch]
            outs[ch] = jnp.concatenate([out_c[ch + (c,)] for c in range(n_chunk)], axis=0)
        return outs

    if n_blk == 1:
        outs = delta_blocks([(d, g, 0, g) for g in range(n_seq) for d in range(2)])
        for g in range(n_seq):
            for h in range(DN_HEADS):
                o_ref[g, :, h * DN_DK:(h + 1) * DN_DK] = outs[(0, g, h)] + outs[(1, g, h)]
    else:
        def scan_block(b, carry):
            r0 = (pl.multiple_of(b * bt, bt), pl.multiple_of((n_blk - 1 - b) * bt, bt))
            outs = delta_blocks([(d, g, r0[d], (g, d)) for g in range(n_seq) for d in range(2)])
            for (d, g, h), o in outs.items():
                o_ref[g, pl.ds(r0[d], bt), h * DN_DK:(h + 1) * DN_DK] += o
            return carry

        lax.fori_loop(0, n_blk, scan_block, 0)

    sfin_ref[...] = st_ref[...]


def _deltanet(qkv, ab, s0, conv_w, alog_row, dtb_row, row_len, n_seq):
    nb, seq_len, _ = qkv.shape
    n_state = 2 * DN_HEADS
    zero_init = s0 is None
    seq = lambda w, **kw: pl.BlockSpec((n_seq, seq_len, w), lambda b: (b, 0, 0), **kw)
    state = pl.BlockSpec((n_seq, n_state, DN_DK, DN_DK), lambda b: (b, 0, 0, 0))
    qkv_mode = dict(pipeline_mode=pl.Buffered(1)) if seq_len * n_seq > 4 * BLOCK_T else {}
    consts = [_const_spec(conv_w.shape), _const_spec(alog_row.shape), _const_spec(dtb_row.shape)]
    operands = (qkv, ab) + (() if zero_init else (s0,)) + (conv_w, alog_row, dtb_row)
    return pl.pallas_call(
        functools.partial(_deltanet_kernel, seq_len=seq_len, row_len=row_len, n_seq=n_seq,
                          zero_init=zero_init),
        grid=(nb // n_seq,),
        in_specs=[seq(QKV_W, **qkv_mode), seq(LANES)] + ([] if zero_init else [state]) + consts,
        out_specs=[seq(DN_W), state],
        out_shape=[jax.ShapeDtypeStruct((nb, seq_len, DN_W), F32),
                   jax.ShapeDtypeStruct((nb, n_state, DN_DK, DN_DK), F32)],
        scratch_shapes=[pltpu.VMEM((n_seq, seq_len, QKV_W), F32), pltpu.VMEM((n_seq, seq_len, LANES), F32),
                        pltpu.VMEM((n_seq, seq_len, LANES), F32), pltpu.VMEM((n_seq, seq_len, LANES), F32),
                        pltpu.VMEM((n_seq, n_state, DN_DK, DN_DK), F32)],
        compiler_params=_params(1),
        name="deltanet",
    )(*operands)


def _layer(x, mod, s0, row_len, n_seq, p):
    nb, seq_len, _ = x.shape
    n = nb * seq_len
    tiles_per_mod = (n // mod.shape[0]) // FFN_TILE
    x1, qkv, z, glu, ab = _ffn1_proj(x.reshape(n, D_MODEL), mod, p["norm_g"], p["wgu1"], p["wd1"],
                                     p["win"], tiles_per_mod)
    o, s_fin = _deltanet(qkv.reshape(nb, seq_len, QKV_W), ab.reshape(nb, seq_len, LANES), s0,
                         p["dn_conv_w"], p["alog_row"], p["dtb_row"], row_len, n_seq)
    cv = _conv_module(glu, p["cv_dw_w"], p["cv_dw_b"], p["cv_ln_g"], p["cv_ln_b"], row_len)
    y = _out_ffn2(x1, o.reshape(n, DN_W), z, cv, mod, p["norm_g"], p["dn_norm_g"], p["wout"],
                  p["wgu2"], p["wd2"], tiles_per_mod)
    return y.reshape(nb, seq_len, D_MODEL), s_fin


def kernel(x_prompt, x_sample, state_delta, c, c_ctx, w_mod, b_mod, norm_g, ffn1_w_in, ffn1_w_out, w_in,
           dn_conv_w, dn_a_log, dn_dt_bias, dn_norm_g, cv_dw_w, cv_dw_b, cv_ln_g, cv_ln_b, w_out,
           ffn2_w_in, ffn2_w_out):
    depth = w_mod.shape[0]
    assert depth == 1, "one trunk layer"
    batch, seq_len, _ = x_prompt.shape
    dec_batch, dec_seq, _ = x_sample.shape
    n_state = 2 * DN_HEADS

    cond = jnp.zeros((8, D_MODEL), F32).at[0].set(c_ctx).at[1:1 + dec_batch].set(c)
    mod = _modulation(cond, w_mod[0], b_mod[0]).reshape(8, N_MOD, D_MODEL)
    mod_ctx, mod_lat = mod[0:1], mod[1:1 + dec_batch]

    wi = w_in[0]
    a0 = 4 * DN_W + 4 * DN_HEADS
    win = jnp.concatenate([wi[:, 0:QKV_W], wi[:, QKV_W + 4 * DN_HEADS:a0], wi[:, a0:],
                           wi[:, QKV_W:QKV_W + 4 * DN_HEADS],
                           jnp.zeros((D_MODEL, LANES - 4 * DN_HEADS), F32)], axis=1).astype(BF16)
    pad8 = lambda v: jnp.zeros((1, LANES), F32).at[0, 0:n_state].set(v.reshape(n_state))
    p = dict(norm_g=norm_g[0], wgu1=ffn1_w_in[0].astype(BF16), wd1=ffn1_w_out[0].astype(BF16), win=win,
             dn_conv_w=dn_conv_w[0], alog_row=pad8(dn_a_log[0]), dtb_row=pad8(dn_dt_bias[0]),
             dn_norm_g=dn_norm_g[0].reshape(1, DN_DK), cv_dw_w=cv_dw_w[0],
             cv_dw_b=cv_dw_b[0].reshape(1, CV_W), cv_ln_g=cv_ln_g[0].reshape(1, CV_W),
             cv_ln_b=cv_ln_b[0].reshape(1, CV_W), wout=w_out[0].astype(BF16),
             wgu2=ffn2_w_in[0].astype(BF16), wd2=ffn2_w_out[0].astype(BF16))

    y_p, s_ctx = _layer(x_prompt, mod_ctx, None, seq_len, CTX_SEQS_PER_STEP, p)
    s_lat = state_delta[:, 0].reshape(dec_batch, n_state, DN_DK, DN_DK)
    y_s, _ = _layer(x_sample, mod_lat, s_lat, GRID_W, 1, p)
    new_state = s_ctx.reshape(batch, 1, 2, DN_HEADS, DN_DK, DN_DK).astype(x_prompt.dtype)
    return (y_p, y_s, new_state)
```

```python
import functools

import jax
import jax.numpy as jnp
from jax import lax
from jax.experimental import pallas as pl
from jax.experimental.pallas import tpu as pltpu

F32 = jnp.float32
BF16 = jnp.bfloat16

D_MODEL = 1024
D_FF = 2816
N_MOD = 9
GRID_W = 64
DN_W = 512
CV_W = 512
DN_HEADS = 4
DN_DK = 128
CHUNK = 64
SHORT_CONV = 3
CONV_K = 31
EPS = 1e-6

LANES = 128
SUBLANES = 8
BLOCK_T = 256
PAIR_T = 2 * CHUNK
GATE_BETA = 2 * DN_HEADS
GATE_TOT = 4 * DN_HEADS
CTX_SEQS_PER_STEP = 1
TOKEN_TILE = 256
FFN_TILE = 512
CONV_HALO = 16
CONV_ROWS = 32
VMEM_LIMIT_BYTES = 56 * 1024 * 1024
FF_CHUNKS = ((0, 768), (768, 768), (1536, 768), (2304, 512))
QKV_W = 3 * DN_W
IN_QKV = (0, QKV_W)
IN_Z = (QKV_W, QKV_W + DN_W)
IN_GLU = (QKV_W + DN_W, QKV_W + DN_W + 2 * CV_W)
IN_AB = (QKV_W + DN_W + 2 * CV_W, QKV_W + DN_W + 2 * CV_W + LANES)
IN_COLS_PADDED = IN_AB[1]


def _dot(a, b):
    return jnp.dot(a, b, preferred_element_type=F32)


def _dot_nt(a, b):
    return lax.dot_general(a, b, (((1,), (1,)), ((), ())), preferred_element_type=F32)


def _dot_tn(a, b):
    return lax.dot_general(a, b, (((0,), (0,)), ((), ())), preferred_element_type=F32)


def _silu(x):
    return x * jax.nn.sigmoid(x)


def _rms(x, g):
    return x * lax.rsqrt(jnp.mean(x * x, axis=-1, keepdims=True) + EPS) * g


def _const_spec(shape):
    nd = len(shape)
    return pl.BlockSpec(shape, lambda *_: (0,) * nd, pipeline_mode=pl.Buffered(1))


def _params(n_grid_dims):
    return pltpu.CompilerParams(dimension_semantics=("arbitrary",) * n_grid_dims,
                                vmem_limit_bytes=VMEM_LIMIT_BYTES)


def _mod_kernel(c_ref, w_ref, b_ref, o_ref):
    s = _silu(c_ref[...]).astype(BF16)
    o_ref[...] = _dot(s, w_ref[...].astype(BF16)) + b_ref[...]


def _modulation(cond, w_mod, b_mod):
    n_out = w_mod.shape[1]
    tn = D_MODEL
    return pl.pallas_call(
        _mod_kernel,
        grid=(n_out // tn,),
        in_specs=[pl.BlockSpec((8, D_MODEL), lambda j: (0, 0)),
                  pl.BlockSpec((D_MODEL, tn), lambda j: (0, j)),
                  pl.BlockSpec((1, tn), lambda j: (0, j))],
        out_specs=pl.BlockSpec((8, tn), lambda j: (0, j)),
        out_shape=jax.ShapeDtypeStruct((8, n_out), F32),
        compiler_params=_params(1),
        name="modulation",
    )(cond, w_mod, b_mod.reshape(1, n_out))


def _swiglu(hb, wgu_ref, wd_ref):
    acc = None
    for s, n in FF_CHUNKS:
        gt = _dot(hb, wgu_ref[:, s:s + n])
        up = _dot(hb, wgu_ref[:, D_FF + s:D_FF + s + n])
        a = (_silu(gt) * up).astype(BF16)
        p = _dot(a, wd_ref[s:s + n, :])
        acc = p if acc is None else acc + p
    return acc


def _ffn1_proj_kernel(x_ref, mod_ref, g_ref, wgu_ref, wd_ref, win_ref, cw_ref, alog_ref, dtb_ref,
                      x1_ref, act_ref, z_ref, glu_ref, gates_ref, *, row_len):
    x = x_ref[...]
    m = mod_ref[...]
    g = g_ref[...]
    h = _rms(x, g[0:1]) * (1.0 + m[1:2]) + m[0:1]
    f = _swiglu(h.astype(BF16), wgu_ref, wd_ref)
    x1 = x + 0.5 * (m[2:3] * _rms(f, g[1:2]))
    x1_ref[...] = x1
    h1 = (_rms(x1, g[2:3]) * (1.0 + m[4:5]) + m[3:4]).astype(BF16)
    qkv = _dot(h1, win_ref[:, IN_QKV[0]:IN_QKV[1]])
    z_ref[...] = _dot(h1, win_ref[:, IN_Z[0]:IN_Z[1]])
    glu_ref[...] = _dot(h1, win_ref[:, IN_GLU[0]:IN_GLU[1]])
    ab = _dot(h1, win_ref[:, IN_AB[0]:IN_AB[1]])

    tm = x.shape[0]
    rowi = lax.broadcasted_iota(jnp.int32, (tm, LANES), 0)
    lanei = lax.broadcasted_iota(jnp.int32, (tm, LANES), 1)
    pos_row = jnp.bitwise_and(rowi, row_len - 1)
    keep_prev = pos_row != 0
    keep_next = pos_row != row_len - 1
    for s in range(QKV_W // LANES):
        ls = slice(s * LANES, (s + 1) * LANES)
        xs = qkv[:, ls]
        w = cw_ref[:, ls]
        xp = jnp.where(keep_prev, pltpu.roll(xs, 1, 0), 0.0)
        xn = jnp.where(keep_next, pltpu.roll(xs, tm - 1, 0), 0.0)
        y = _silu(w[0:1] * xp + w[1:2] * xs + w[2:3] * xn)
        if s < 2 * DN_HEADS:
            y = y * lax.rsqrt(jnp.sum(y * y, axis=-1, keepdims=True) + EPS)
        if s < DN_HEADS:
            y = y * (DN_DK ** -0.5)
        act_ref[:, ls] = y
    t = ab + dtb_ref[...]
    softplus = jnp.maximum(t, 0.0) + jnp.log1p(jnp.exp(-jnp.abs(t)))
    gate = -jnp.exp(alog_ref[...]) * softplus
    pos_chunk = jnp.bitwise_and(rowi, CHUNK - 1)
    pre = gate
    suf = gate
    for sh in (1, 2, 4, 8, 16, 32):
        pre = pre + jnp.where(pos_chunk >= sh, pltpu.roll(pre, sh, 0), 0.0)
        suf = suf + jnp.where(pos_chunk < CHUNK - sh, pltpu.roll(suf, tm - sh, 0), 0.0)
    gc = jnp.where(lanei < DN_HEADS, pre, suf)
    tot = pltpu.roll(pre + suf - gate, GATE_TOT, 1)
    gates_ref[...] = jnp.where(lanei < GATE_BETA, gc, jnp.where(lanei < GATE_TOT, jax.nn.sigmoid(ab), tot))


def _ffn1_proj(x, mod, norm_g, wgu, wd, win, conv_w, alog_row, dtb_row, tiles_per_mod, row_len):
    n = x.shape[0]
    tm = FFN_TILE
    row = lambda w: pl.BlockSpec((tm, w), lambda i: (i, 0))
    return pl.pallas_call(
        functools.partial(_ffn1_proj_kernel, row_len=row_len),
        grid=(n // tm,),
        in_specs=[row(D_MODEL),
                  pl.BlockSpec((None, N_MOD, D_MODEL), lambda i: (i // tiles_per_mod, 0, 0)),
                  _const_spec(norm_g.shape), _const_spec(wgu.shape), _const_spec(wd.shape),
                  _const_spec(win.shape), _const_spec(conv_w.shape), _const_spec(alog_row.shape),
                  _const_spec(dtb_row.shape)],
        out_specs=[row(D_MODEL), row(QKV_W), row(DN_W), row(2 * CV_W), row(LANES)],
        out_shape=[jax.ShapeDtypeStruct((n, D_MODEL), F32), jax.ShapeDtypeStruct((n, QKV_W), F32),
                   jax.ShapeDtypeStruct((n, DN_W), F32), jax.ShapeDtypeStruct((n, 2 * CV_W), F32),
                   jax.ShapeDtypeStruct((n, LANES), F32)],
        compiler_params=_params(1),
        name="ffn1_proj",
    )(x, mod, norm_g, wgu, wd, win, conv_w, alog_row, dtb_row)


def _out_ffn2_kernel(x1_ref, o_ref, z_ref, cv_ref, mod_ref, g_ref, dng_ref, wout_ref, wgu_ref, wd_ref,
                     y_ref):
    x1 = x1_ref[...]
    m = mod_ref[...]
    g = g_ref[...]
    o = o_ref[...]
    z = z_ref[...]
    dng = dng_ref[...]
    heads = []
    for h in range(DN_HEADS):
        oh = o[:, h * DN_DK:(h + 1) * DN_DK]
        zh = z[:, h * DN_DK:(h + 1) * DN_DK]
        heads.append((_rms(oh, dng) * _silu(zh)).astype(BF16))
    og = jnp.concatenate(heads, axis=-1)
    y = _dot(og, wout_ref[0:DN_W, :]) + _dot(cv_ref[...].astype(BF16), wout_ref[DN_W:, :])
    x2 = x1 + m[5:6] * _rms(y, g[3:4])
    h2 = _rms(x2, g[4:5]) * (1.0 + m[7:8]) + m[6:7]
    f = _swiglu(h2.astype(BF16), wgu_ref, wd_ref)
    y_ref[...] = x2 + 0.5 * (m[8:9] * _rms(f, g[5:6]))


def _out_ffn2(x1, o, z, cv, mod, norm_g, dn_norm_g, wout, wgu, wd, tiles_per_mod):
    n = x1.shape[0]
    tm = FFN_TILE
    row = lambda w: pl.BlockSpec((tm, w), lambda i: (i, 0))
    return pl.pallas_call(
        _out_ffn2_kernel,
        grid=(n // tm,),
        in_specs=[row(D_MODEL), row(DN_W), row(DN_W), row(CV_W),
                  pl.BlockSpec((None, N_MOD, D_MODEL), lambda i: (i // tiles_per_mod, 0, 0)),
                  _const_spec(norm_g.shape), _const_spec(dn_norm_g.shape), _const_spec(wout.shape),
                  _const_spec(wgu.shape), _const_spec(wd.shape)],
        out_specs=row(D_MODEL),
        out_shape=jax.ShapeDtypeStruct((n, D_MODEL), F32),
        compiler_params=_params(1),
        name="out_ffn2",
    )(x1, o, z, cv, mod, norm_g, dn_norm_g, wout, wgu, wd)


def _conv_module_kernel(glu_ref, w_ref, b_ref, lg_ref, lb_ref, o_ref, xp_ref, *, row_len):
    n_rows = TOKEN_TILE // row_len
    pitch = row_len + 2 * CONV_HALO
    total = n_rows * pitch
    zeros = jnp.zeros((CONV_HALO, LANES), F32)
    for lg in range(CV_W // LANES):
        x = glu_ref[:, lg * LANES:(lg + 1) * LANES] * jax.nn.sigmoid(
            glu_ref[:, CV_W + lg * LANES:CV_W + (lg + 1) * LANES])
        for r in range(n_rows):
            base = r * pitch
            xp_ref[0, lg, base:base + CONV_HALO, :] = zeros
            xp_ref[0, lg, base + CONV_HALO:base + CONV_HALO + row_len, :] = x[r * row_len:(r + 1) * row_len, :]
            xp_ref[0, lg, base + CONV_HALO + row_len:base + pitch, :] = zeros
        for b in range(1, SUBLANES):
            xp_ref[b, lg, 0:total - SUBLANES, :] = xp_ref[0, lg, b:b + total - SUBLANES, :]
    sub = CONV_ROWS
    subs_per_row = row_len // sub
    lead = CONV_HALO - CONV_K // 2

    def row_group(i, carry):
        out0 = pl.multiple_of(i * sub, sub)
        base = out0 + (i // subs_per_row) * (2 * CONV_HALO)
        for lg in range(CV_W // LANES):
            ls = slice(lg * LANES, (lg + 1) * LANES)
            acc = jnp.zeros((sub, LANES), F32) + b_ref[:, ls]
            for j in range(CONV_K):
                shift = (lead + j) % SUBLANES
                start = pl.multiple_of(base + (lead + j - shift), SUBLANES)
                acc = acc + w_ref[j:j + 1, ls] * xp_ref[shift, lg, pl.ds(start, sub), :]
            o_ref[pl.ds(out0, sub), ls] = acc
        return carry

    lax.fori_loop(0, TOKEN_TILE // sub, row_group, 0)
    y = o_ref[...]
    mu = jnp.mean(y, axis=-1, keepdims=True)
    var = jnp.mean(jnp.square(y - mu), axis=-1, keepdims=True)
    yn = (y - mu) * lax.rsqrt(var + EPS) * lg_ref[...] + lb_ref[...]
    o_ref[...] = _silu(yn)


def _conv_module(glu, w, b, ln_g, ln_b, row_len):
    n = glu.shape[0]
    tm = TOKEN_TILE
    pitch = row_len + 2 * CONV_HALO
    return pl.pallas_call(
        functools.partial(_conv_module_kernel, row_len=row_len),
        grid=(n // tm,),
        in_specs=[pl.BlockSpec((tm, 2 * CV_W), lambda i: (i, 0)),
                  _const_spec(w.shape), _const_spec(b.shape), _const_spec(ln_g.shape),
                  _const_spec(ln_b.shape)],
        out_specs=pl.BlockSpec((tm, CV_W), lambda i: (i, 0)),
        out_shape=jax.ShapeDtypeStruct((n, CV_W), F32),
        scratch_shapes=[pltpu.VMEM((SUBLANES, CV_W // LANES, (tm // row_len) * pitch, LANES), F32)],
        compiler_params=_params(1),
        name="conv_module",
    )(glu, w, b, ln_g, ln_b)


def _deltanet_kernel(*refs, seq_len, n_seq, zero_init):
    if zero_init:
        act_ref, gates_ref, o_ref, sfin_ref, st_ref = refs
        s0_ref = None
    else:
        act_ref, gates_ref, s0_ref, o_ref, sfin_ref, st_ref = refs
    n_blk = seq_len // BLOCK_T
    bt = BLOCK_T
    f32_inf = jnp.float32(jnp.inf)

    n_state = 2 * DN_HEADS
    if zero_init:
        st_ref[...] = jnp.zeros((n_seq, n_state, DN_DK, DN_DK), F32)
    else:
        st_ref[...] = s0_ref[...]
    if n_blk > 1:
        o_ref[...] = jnp.zeros((n_seq, seq_len, DN_W), F32)

    state_is_zero = zero_init and n_blk == 1
    pt = PAIR_T
    n_pair = bt // pt
    n_chunk = bt // CHUNK
    ri = lax.broadcasted_iota(jnp.int32, (pt, pt), 0)
    ci = lax.broadcasted_iota(jnp.int32, (pt, pt), 1)
    same = jnp.right_shift(ri, 6) == jnp.right_shift(ci, 6)
    incl = (same & (ri >= ci), same & (ri <= ci))
    strict = (same & (ri > ci), same & (ri < ci))
    first_chunk_lanes = ci < CHUNK

    def delta_blocks(jobs):
        gram = {}
        probs = {}
        for d, g, r0, key in jobs:
            rows = pl.ds(r0, bt)
            gates = gates_ref[g, rows, :]
            gates_t = jnp.transpose(gates)
            for h in range(DN_HEADS):
                c8 = d * DN_HEADS + h
                for p in range(n_pair):
                    ps = slice(p * pt, (p + 1) * pt)
                    prow = pl.ds(r0 + p * pt, pt)
                    qh = act_ref[g, prow, h * DN_DK:(h + 1) * DN_DK]
                    kh = act_ref[g, prow, DN_W + h * DN_DK:DN_W + (h + 1) * DN_DK]
                    vh = act_ref[g, prow, 2 * DN_W + h * DN_DK:2 * DN_W + (h + 1) * DN_DK]
                    if (key, h, p) not in gram:
                        khb = kh.astype(BF16)
                        gram[(key, h, p)] = (_dot_nt(jnp.concatenate([khb, qh.astype(BF16)], axis=0), khb),
                                             jnp.transpose(kh))
                    gr, kh_t = gram[(key, h, p)]
                    kk = gr[0:pt]
                    qk = gr[pt:2 * pt]
                    g_i = gates[ps, c8:c8 + 1]
                    g_j = gates_t[c8:c8 + 1, ps]
                    dec = jnp.exp(jnp.where(incl[d], g_i - g_j, -f32_inf))
                    b_i = gates[ps, GATE_BETA + c8:GATE_BETA + c8 + 1]
                    eg = jnp.exp(g_i)
                    kd_t = kh_t * jnp.exp(gates_t[GATE_TOT + c8:GATE_TOT + c8 + 1, ps] - g_j)
                    probs[(d, g, h, p)] = dict(
                        p=-(jnp.where(strict[d], kk * dec, 0.0) * b_i), qkm=(qk * dec).astype(BF16),
                        x=jnp.concatenate([vh * b_i, kh * (b_i * eg)], axis=-1), qd=qh * eg,
                        kd_t=[jnp.where(first_chunk_lanes, kd_t, 0.0).astype(BF16),
                              jnp.where(first_chunk_lanes, 0.0, kd_t).astype(BF16)],
                        tot=gates[ps, GATE_TOT + c8:GATE_TOT + c8 + 1])
        for pr in probs.values():
            pb = pr["p"].astype(BF16)
            pr["q"] = pr["p"]
            pr["p"] = _dot(pb, pb)
        for _ in range(4):
            for pr in probs.values():
                pb = pr["p"].astype(BF16)
                r = _dot(pb, jnp.concatenate([pb, pr["q"].astype(BF16)], axis=-1))
                pr["q"] = pr["q"] + pr["p"] + r[:, pt:2 * pt]
                pr["p"] = r[:, 0:pt]
        for pr in probs.values():
            r = _dot(pr["p"].astype(BF16), pr["q"].astype(BF16))
            qm = pr["q"] + pr["p"] + r
            pr["x"] = pr["x"] + _dot(qm.astype(BF16), pr["x"].astype(BF16))
        for pr in probs.values():
            xb = pr["x"].astype(BF16)
            qx = _dot(pr["qkm"], xb)
            pr["o0"] = qx[:, 0:DN_DK]
            pr["e"] = pr["qd"] - qx[:, DN_DK:2 * DN_DK]
            pr["kx"] = [_dot(kd_t, xb) for kd_t in pr["kd_t"]]
        chains = [(d, g, h) for d, g, _, _ in jobs for h in range(DN_HEADS)]
        state = {(d, g, h): (None if state_is_zero else st_ref[g, d * DN_HEADS + h]) for d, g, h in chains}
        out_c = {}
        for step in range(n_chunk):
            res = {}
            for ch in chains:
                if state[ch] is None:
                    continue
                c = step if ch[0] == 0 else n_chunk - 1 - step
                pr = probs[ch + (c // 2,)]
                cs = slice((c % 2) * CHUNK, (c % 2 + 1) * CHUNK)
                lhs = jnp.concatenate([-pr["kx"][c % 2][:, DN_DK:2 * DN_DK], pr["e"][cs]], axis=0)
                res[ch] = _dot(lhs.astype(BF16), state[ch].astype(BF16))
            for ch in chains:
                c = step if ch[0] == 0 else n_chunk - 1 - step
                pr = probs[ch + (c // 2,)]
                cs = slice((c % 2) * CHUNK, (c % 2 + 1) * CHUNK)
                b_c = pr["kx"][c % 2][:, 0:DN_DK]
                if state[ch] is None:
                    out_c[ch + (c,)] = pr["o0"][cs]
                    state[ch] = b_c
                else:
                    g_last = jnp.exp(pr["tot"][(c % 2) * CHUNK:(c % 2) * CHUNK + 1])
                    out_c[ch + (c,)] = res[ch][DN_DK:DN_DK + CHUNK] + pr["o0"][cs]
                    state[ch] = state[ch] * g_last + res[ch][0:DN_DK] + b_c
        outs = {}
        for ch in chains:
            d, g, h = ch
            st_ref[g, d * DN_HEADS + h] = state[ch]
            outs[ch] = jnp.concatenate([out_c[ch + (c,)] for c in range(n_chunk)], axis=0)
        return outs

    if n_blk == 1:
        outs = delta_blocks([(d, g, 0, g) for g in range(n_seq) for d in range(2)])
        for g in range(n_seq):
            for h in range(DN_HEADS):
                o_ref[g, :, h * DN_DK:(h + 1) * DN_DK] = outs[(0, g, h)] + outs[(1, g, h)]
    else:
        def scan_block(b, carry):
            r0 = (pl.multiple_of(b * bt, bt), pl.multiple_of((n_blk - 1 - b) * bt, bt))
            outs = delta_blocks([(d, g, r0[d], (g, d)) for g in range(n_seq) for d in range(2)])
            for (d, g, h), o in outs.items():
                o_ref[g, pl.ds(r0[d], bt), h * DN_DK:(h + 1) * DN_DK] += o
            return carry

        lax.fori_loop(0, n_blk, scan_block, 0)

    sfin_ref[...] = st_ref[...]


def _deltanet(act, gates, s0, n_seq):
    nb, seq_len, _ = act.shape
    n_state = 2 * DN_HEADS
    zero_init = s0 is None
    seq = lambda w, **kw: pl.BlockSpec((n_seq, seq_len, w), lambda b: (b, 0, 0), **kw)
    state = pl.BlockSpec((n_seq, n_state, DN_DK, DN_DK), lambda b: (b, 0, 0, 0))
    act_mode = dict(pipeline_mode=pl.Buffered(1)) if seq_len * n_seq > 4 * BLOCK_T else {}
    operands = (act, gates) + (() if zero_init else (s0,))
    return pl.pallas_call(
        functools.partial(_deltanet_kernel, seq_len=seq_len, n_seq=n_seq, zero_init=zero_init),
        grid=(nb // n_seq,),
        in_specs=[seq(QKV_W, **act_mode), seq(LANES)] + ([] if zero_init else [state]),
        out_specs=[seq(DN_W), state],
        out_shape=[jax.ShapeDtypeStruct((nb, seq_len, DN_W), F32),
                   jax.ShapeDtypeStruct((nb, n_state, DN_DK, DN_DK), F32)],
        scratch_shapes=[pltpu.VMEM((n_seq, n_state, DN_DK, DN_DK), F32)],
        compiler_params=_params(1),
        name="deltanet",
    )(*operands)


def _layer(x, mod, s0, row_len, n_seq, p):
    nb, seq_len, _ = x.shape
    n = nb * seq_len
    tiles_per_mod = (n // mod.shape[0]) // FFN_TILE
    x1, act, z, glu, gates = _ffn1_proj(x.reshape(n, D_MODEL), mod, p["norm_g"], p["wgu1"], p["wd1"],
                                        p["win"], p["dn_conv_w"], p["alog_row"], p["dtb_row"],
                                        tiles_per_mod, row_len)
    o, s_fin = _deltanet(act.reshape(nb, seq_len, QKV_W), gates.reshape(nb, seq_len, LANES), s0, n_seq)
    cv = _conv_module(glu, p["cv_dw_w"], p["cv_dw_b"], p["cv_ln_g"], p["cv_ln_b"], row_len)
    y = _out_ffn2(x1, o.reshape(n, DN_W), z, cv, mod, p["norm_g"], p["dn_norm_g"], p["wout"],
                  p["wgu2"], p["wd2"], tiles_per_mod)
    return y.reshape(nb, seq_len, D_MODEL), s_fin


def kernel(x_prompt, x_sample, state_delta, c, c_ctx, w_mod, b_mod, norm_g, ffn1_w_in, ffn1_w_out, w_in,
           dn_conv_w, dn_a_log, dn_dt_bias, dn_norm_g, cv_dw_w, cv_dw_b, cv_ln_g, cv_ln_b, w_out,
           ffn2_w_in, ffn2_w_out):
    depth = w_mod.shape[0]
    assert depth == 1, "one trunk layer"
    batch, seq_len, _ = x_prompt.shape
    dec_batch, dec_seq, _ = x_sample.shape
    n_state = 2 * DN_HEADS

    cond = jnp.zeros((8, D_MODEL), F32).at[0].set(c_ctx).at[1:1 + dec_batch].set(c)
    mod = _modulation(cond, w_mod[0], b_mod[0]).reshape(8, N_MOD, D_MODEL)
    mod_ctx, mod_lat = mod[0:1], mod[1:1 + dec_batch]

    wi = w_in[0].astype(BF16)
    a0 = 4 * DN_W + 4 * DN_HEADS
    win = jnp.concatenate([wi[:, 0:QKV_W], wi[:, QKV_W + 4 * DN_HEADS:a0], wi[:, a0:],
                           wi[:, QKV_W:QKV_W + 4 * DN_HEADS],
                           jnp.zeros((D_MODEL, LANES - 4 * DN_HEADS), BF16)], axis=1)
    pad8 = lambda v: jnp.zeros((1, LANES), F32).at[0, 0:n_state].set(v.reshape(n_state))
    p = dict(norm_g=norm_g[0], wgu1=ffn1_w_in[0].astype(BF16), wd1=ffn1_w_out[0].astype(BF16), win=win,
             dn_conv_w=dn_conv_w[0], alog_row=pad8(dn_a_log[0]), dtb_row=pad8(dn_dt_bias[0]),
             dn_norm_g=dn_norm_g[0].reshape(1, DN_DK), cv_dw_w=cv_dw_w[0],
             cv_dw_b=cv_dw_b[0].reshape(1, CV_W), cv_ln_g=cv_ln_g[0].reshape(1, CV_W),
             cv_ln_b=cv_ln_b[0].reshape(1, CV_W), wout=w_out[0].astype(BF16),
             wgu2=ffn2_w_in[0].astype(BF16), wd2=ffn2_w_out[0].astype(BF16))

    y_p, s_ctx = _layer(x_prompt, mod_ctx, None, seq_len, CTX_SEQS_PER_STEP, p)
    s_lat = state_delta[:, 0].reshape(dec_batch, n_state, DN_DK, DN_DK)
    y_s, _ = _layer(x_sample, mod_lat, s_lat, GRID_W, 1, p)
    new_state = s_ctx.reshape(batch, 1, 2, DN_HEADS, DN_DK, DN_DK).astype(x_prompt.dtype)
    return (y_p, y_s, new_state)
```

```python
import functools

import jax
import jax.numpy as jnp
from jax import lax
from jax.experimental import pallas as pl
from jax.experimental.pallas import tpu as pltpu

F32 = jnp.float32
BF16 = jnp.bfloat16

D_MODEL = 1024
D_FF = 2816
N_MOD = 9
GRID_W = 64
DN_W = 512
CV_W = 512
DN_HEADS = 4
DN_DK = 128
CHUNK = 64
SHORT_CONV = 3
CONV_K = 31
EPS = 1e-6

LANES = 128
SUBLANES = 8
BLOCK_T = 256
PAIR_T = 2 * CHUNK
GATE_BETA = 2 * DN_HEADS
GATE_TOT = 4 * DN_HEADS
CTX_SEQS_PER_STEP = 1
FFN_TILE = 512
SUB_TILE = 256
CONV_HALO = 16
CONV_ROWS = 32
VMEM_LIMIT_BYTES = 56 * 1024 * 1024
FF_CHUNKS = ((0, 768), (768, 768), (1536, 768), (2304, 512))
QKV_W = 3 * DN_W
IN_QKV = (0, QKV_W)
IN_Z = (QKV_W, QKV_W + DN_W)
IN_GLU = (QKV_W + DN_W, QKV_W + DN_W + 2 * CV_W)
IN_AB = (QKV_W + DN_W + 2 * CV_W, QKV_W + DN_W + 2 * CV_W + LANES)
IN_COLS_PADDED = IN_AB[1]


def _dot(a, b):
    return jnp.dot(a, b, preferred_element_type=F32)


def _dot_nt(a, b):
    return lax.dot_general(a, b, (((1,), (1,)), ((), ())), preferred_element_type=F32)


def _dot_tn(a, b):
    return lax.dot_general(a, b, (((0,), (0,)), ((), ())), preferred_element_type=F32)


def _silu(x):
    return x * jax.nn.sigmoid(x)


def _rms(x, g):
    return x * lax.rsqrt(jnp.mean(x * x, axis=-1, keepdims=True) + EPS) * g


def _const_spec(shape):
    nd = len(shape)
    return pl.BlockSpec(shape, lambda *_: (0,) * nd, pipeline_mode=pl.Buffered(1))


def _params(n_grid_dims):
    return pltpu.CompilerParams(dimension_semantics=("arbitrary",) * n_grid_dims,
                                vmem_limit_bytes=VMEM_LIMIT_BYTES)


def _mod_kernel(c_ref, w_ref, b_ref, o_ref):
    s = _silu(c_ref[...]).astype(BF16)
    o_ref[...] = _dot(s, w_ref[...].astype(BF16)) + b_ref[...]


def _modulation(cond, w_mod, b_mod):
    n_out = w_mod.shape[1]
    tn = D_MODEL
    return pl.pallas_call(
        _mod_kernel,
        grid=(n_out // tn,),
        in_specs=[pl.BlockSpec((8, D_MODEL), lambda j: (0, 0)),
                  pl.BlockSpec((D_MODEL, tn), lambda j: (0, j)),
                  pl.BlockSpec((1, tn), lambda j: (0, j))],
        out_specs=pl.BlockSpec((8, tn), lambda j: (0, j)),
        out_shape=jax.ShapeDtypeStruct((8, n_out), F32),
        compiler_params=_params(1),
        name="modulation",
    )(cond, w_mod, b_mod.reshape(1, n_out))


def _ffn1_proj_kernel(x_ref, mod_ref, g_ref, wgu_ref, wd_ref, win_ref, cw_ref, alog_ref, dtb_ref,
                      x1_ref, act_ref, z_ref, glu_ref, gates_ref, *, row_len):
    m = mod_ref[...]
    g = g_ref[...]
    ts = SUB_TILE
    rowi = lax.broadcasted_iota(jnp.int32, (ts, LANES), 0)
    lanei = lax.broadcasted_iota(jnp.int32, (ts, LANES), 1)
    pos_row = jnp.bitwise_and(rowi, row_len - 1)
    keep_prev = pos_row != 0
    keep_next = pos_row != row_len - 1
    pos_chunk = jnp.bitwise_and(rowi, CHUNK - 1)

    def matmul_part(rows, out):
        x = x_ref[rows, :]
        hb = (_rms(x, g[0:1]) * (1.0 + m[1:2]) + m[0:1]).astype(BF16)
        acc = None
        for s, n in FF_CHUNKS:
            gt = _dot(hb, wgu_ref[:, s:s + n])
            up = _dot(hb, wgu_ref[:, D_FF + s:D_FF + s + n])
            a = (_silu(gt) * up).astype(BF16)
            p = _dot(a, wd_ref[s:s + n, :])
            acc = p if acc is None else acc + p
            yield
        x1 = x + 0.5 * (m[2:3] * _rms(acc, g[1:2]))
        x1_ref[rows, :] = x1
        h1 = (_rms(x1, g[2:3]) * (1.0 + m[4:5]) + m[3:4]).astype(BF16)
        out["qkv"] = _dot(h1, win_ref[:, IN_QKV[0]:IN_QKV[1]])
        out["ab"] = _dot(h1, win_ref[:, IN_AB[0]:IN_AB[1]])
        z_ref[rows, :] = _dot(h1, win_ref[:, IN_Z[0]:IN_Z[1]])
        glu_ref[rows, :] = _dot(h1, win_ref[:, IN_GLU[0]:IN_GLU[1]])

    def vector_part(rows, inp):
        qkv, ab = inp["qkv"], inp["ab"]
        for s in range(QKV_W // LANES):
            ls = slice(s * LANES, (s + 1) * LANES)
            xs = qkv[:, ls]
            w = cw_ref[:, ls]
            xp = jnp.where(keep_prev, pltpu.roll(xs, 1, 0), 0.0)
            xn = jnp.where(keep_next, pltpu.roll(xs, ts - 1, 0), 0.0)
            y = _silu(w[0:1] * xp + w[1:2] * xs + w[2:3] * xn)
            if s < 2 * DN_HEADS:
                y = y * lax.rsqrt(jnp.sum(y * y, axis=-1, keepdims=True) + EPS)
            if s < DN_HEADS:
                y = y * (DN_DK ** -0.5)
            act_ref[rows, ls] = y
            if s % 4 == 3:
                yield
        t = ab + dtb_ref[...]
        softplus = jnp.maximum(t, 0.0) + jnp.log1p(jnp.exp(-jnp.abs(t)))
        gate = -jnp.exp(alog_ref[...]) * softplus
        pre = gate
        suf = gate
        for sh in (1, 2, 4, 8, 16, 32):
            pre = pre + jnp.where(pos_chunk >= sh, pltpu.roll(pre, sh, 0), 0.0)
            suf = suf + jnp.where(pos_chunk < CHUNK - sh, pltpu.roll(suf, ts - sh, 0), 0.0)
        gc = jnp.where(lanei < DN_HEADS, pre, suf)
        tot = pltpu.roll(pre + suf - gate, GATE_TOT, 1)
        gates_ref[rows, :] = jnp.where(lanei < GATE_BETA, gc,
                                       jnp.where(lanei < GATE_TOT, jax.nn.sigmoid(ab), tot))

    n_sub = x_ref.shape[0] // ts
    carried = [dict() for _ in range(n_sub)]
    pending = None
    for s in range(n_sub + 1):
        mm = matmul_part(pl.ds(s * ts, ts), carried[s]) if s < n_sub else iter(())
        while True:
            more_mm = next(mm, StopIteration) is not StopIteration
            more_vec = pending is not None and next(pending, StopIteration) is not StopIteration
            if not more_mm and not more_vec:
                break
        pending = vector_part(pl.ds(s * ts, ts), carried[s]) if s < n_sub else None


def _ffn1_proj(x, mod, norm_g, wgu, wd, win, conv_w, alog_row, dtb_row, tiles_per_mod, row_len):
    n = x.shape[0]
    tm = FFN_TILE
    row = lambda w: pl.BlockSpec((tm, w), lambda i: (i, 0))
    return pl.pallas_call(
        functools.partial(_ffn1_proj_kernel, row_len=row_len),
        grid=(n // tm,),
        in_specs=[row(D_MODEL),
                  pl.BlockSpec((None, N_MOD, D_MODEL), lambda i: (i // tiles_per_mod, 0, 0)),
                  _const_spec(norm_g.shape), _const_spec(wgu.shape), _const_spec(wd.shape),
                  _const_spec(win.shape), _const_spec(conv_w.shape), _const_spec(alog_row.shape),
                  _const_spec(dtb_row.shape)],
        out_specs=[row(D_MODEL), row(QKV_W), row(DN_W), row(2 * CV_W), row(LANES)],
        out_shape=[jax.ShapeDtypeStruct((n, D_MODEL), F32), jax.ShapeDtypeStruct((n, QKV_W), F32),
                   jax.ShapeDtypeStruct((n, DN_W), F32), jax.ShapeDtypeStruct((n, 2 * CV_W), F32),
                   jax.ShapeDtypeStruct((n, LANES), F32)],
        compiler_params=_params(1),
        name="ffn1_proj",
    )(x, mod, norm_g, wgu, wd, win, conv_w, alog_row, dtb_row)


def _interleave(primary, secondary):
    more_a = more_b = True
    while more_a or more_b:
        if more_a:
            more_a = next(primary, StopIteration) is not StopIteration
        if more_b:
            more_b = secondary is not None and next(secondary, StopIteration) is not StopIteration


def _out_ffn2_kernel(x1_ref, o_ref, z_ref, glu_ref, mod_ref, g_ref, dng_ref, cw_ref, cb_ref, lg_ref, lb_ref,
                     wout_ref, wgu_ref, wd_ref, y_ref, xp_ref, cv_ref, *, row_len):
    m = mod_ref[...]
    g = g_ref[...]
    dng = dng_ref[...]
    ts = SUB_TILE
    n_sub = x1_ref.shape[0] // ts
    n_rows = ts // row_len
    pitch = row_len + 2 * CONV_HALO
    total = n_rows * pitch
    grp = CONV_ROWS
    lead = CONV_HALO - CONV_K // 2

    def vector_part(s, out):
        rows = pl.ds(s * ts, ts)
        heads = []
        for h in range(DN_HEADS):
            hs = slice(h * DN_DK, (h + 1) * DN_DK)
            heads.append((_rms(o_ref[rows, hs], dng) * _silu(z_ref[rows, hs])).astype(BF16))
        out["og"] = jnp.concatenate(heads, axis=-1)
        yield
        zeros = jnp.zeros((CONV_HALO, LANES), F32)
        for lg in range(CV_W // LANES):
            x = glu_ref[rows, lg * LANES:(lg + 1) * LANES] * jax.nn.sigmoid(
                glu_ref[rows, CV_W + lg * LANES:CV_W + (lg + 1) * LANES])
            for r in range(n_rows):
                base = r * pitch
                xp_ref[s, 0, lg, base:base + CONV_HALO, :] = zeros
                xp_ref[s, 0, lg, base + CONV_HALO:base + CONV_HALO + row_len, :] = (
                    x[r * row_len:(r + 1) * row_len, :])
                xp_ref[s, 0, lg, base + CONV_HALO + row_len:base + pitch, :] = zeros
            for b in range(1, SUBLANES):
                xp_ref[s, b, lg, 0:total - SUBLANES, :] = xp_ref[s, 0, lg, b:b + total - SUBLANES, :]
            yield
        for i in range(ts // grp):
            base = i * grp + (i * grp // row_len) * (2 * CONV_HALO)
            for lg in range(CV_W // LANES):
                ls = slice(lg * LANES, (lg + 1) * LANES)
                acc = jnp.zeros((grp, LANES), F32) + cb_ref[:, ls]
                for j in range(CONV_K):
                    shift = (lead + j) % SUBLANES
                    start = base + lead + j - shift
                    acc = acc + cw_ref[j:j + 1, ls] * xp_ref[s, shift, lg, start:start + grp, :]
                cv_ref[s, i * grp:(i + 1) * grp, ls] = acc
            yield
        c = cv_ref[s]
        mu = jnp.mean(c, axis=-1, keepdims=True)
        var = jnp.mean(jnp.square(c - mu), axis=-1, keepdims=True)
        cn = (c - mu) * lax.rsqrt(var + EPS) * lg_ref[...] + lb_ref[...]
        out["cv"] = _silu(cn).astype(BF16)

    def matmul_part(s, inp):
        rows = pl.ds(s * ts, ts)
        y = _dot(inp["og"], wout_ref[0:DN_W, :]) + _dot(inp["cv"], wout_ref[DN_W:, :])
        x2 = x1_ref[rows, :] + m[5:6] * _rms(y, g[3:4])
        hb = (_rms(x2, g[4:5]) * (1.0 + m[7:8]) + m[6:7]).astype(BF16)
        yield
        acc = None
        for c0, n in FF_CHUNKS:
            gt = _dot(hb, wgu_ref[:, c0:c0 + n])
            yield
            up = _dot(hb, wgu_ref[:, D_FF + c0:D_FF + c0 + n])
            a = (_silu(gt) * up).astype(BF16)
            yield
            p = _dot(a, wd_ref[c0:c0 + n, :])
            acc = p if acc is None else acc + p
            yield
        y_ref[rows, :] = x2 + 0.5 * (m[8:9] * _rms(acc, g[5:6]))

    carried = [dict() for _ in range(n_sub)]
    _interleave(vector_part(0, carried[0]), None)
    for s in range(n_sub):
        nxt = vector_part(s + 1, carried[s + 1]) if s + 1 < n_sub else None
        _interleave(matmul_part(s, carried[s]), nxt)


def _out_ffn2(x1, o, z, glu, mod, norm_g, dn_norm_g, conv_w, conv_b, ln_g, ln_b, wout, wgu, wd,
              tiles_per_mod, row_len):
    n = x1.shape[0]
    tm = FFN_TILE
    n_sub = tm // SUB_TILE
    total = (SUB_TILE // row_len) * (row_len + 2 * CONV_HALO)
    row = lambda w: pl.BlockSpec((tm, w), lambda i: (i, 0))
    consts = (norm_g, dn_norm_g, conv_w, conv_b, ln_g, ln_b, wout, wgu, wd)
    return pl.pallas_call(
        functools.partial(_out_ffn2_kernel, row_len=row_len),
        grid=(n // tm,),
        in_specs=[row(D_MODEL), row(DN_W), row(DN_W), row(2 * CV_W),
                  pl.BlockSpec((None, N_MOD, D_MODEL), lambda i: (i // tiles_per_mod, 0, 0))]
                 + [_const_spec(c.shape) for c in consts],
        out_specs=row(D_MODEL),
        out_shape=jax.ShapeDtypeStruct((n, D_MODEL), F32),
        scratch_shapes=[pltpu.VMEM((n_sub, SUBLANES, CV_W // LANES, total, LANES), F32),
                        pltpu.VMEM((n_sub, SUB_TILE, CV_W), F32)],
        compiler_params=_params(1),
        name="out_ffn2",
    )(x1, o, z, glu, mod, *consts)


def _deltanet_kernel(*refs, seq_len, n_seq, zero_init):
    if zero_init:
        act_ref, gates_ref, o_ref, sfin_ref, st_ref = refs
        s0_ref = None
    else:
        act_ref, gates_ref, s0_ref, o_ref, sfin_ref, st_ref = refs
    n_blk = seq_len // BLOCK_T
    bt = BLOCK_T
    f32_inf = jnp.float32(jnp.inf)

    n_state = 2 * DN_HEADS
    if zero_init:
        st_ref[...] = jnp.zeros((n_seq, n_state, DN_DK, DN_DK), F32)
    else:
        st_ref[...] = s0_ref[...]
    if n_blk > 1:
        o_ref[...] = jnp.zeros((n_seq, seq_len, DN_W), F32)

    state_is_zero = zero_init and n_blk == 1
    pt = PAIR_T
    n_pair = bt // pt
    n_chunk = bt // CHUNK
    ri = lax.broadcasted_iota(jnp.int32, (pt, pt), 0)
    ci = lax.broadcasted_iota(jnp.int32, (pt, pt), 1)
    same = jnp.right_shift(ri, 6) == jnp.right_shift(ci, 6)
    incl = (same & (ri >= ci), same & (ri <= ci))
    strict = (same & (ri > ci), same & (ri < ci))
    first_chunk_lanes = ci < CHUNK

    def delta_blocks(jobs):
        gram = {}
        probs = {}
        for d, g, r0, key in jobs:
            rows = pl.ds(r0, bt)
            gates = gates_ref[g, rows, :]
            gates_t = jnp.transpose(gates)
            for h in range(DN_HEADS):
                c8 = d * DN_HEADS + h
                for p in range(n_pair):
                    ps = slice(p * pt, (p + 1) * pt)
                    prow = pl.ds(r0 + p * pt, pt)
                    qh = act_ref[g, prow, h * DN_DK:(h + 1) * DN_DK]
                    kh = act_ref[g, prow, DN_W + h * DN_DK:DN_W + (h + 1) * DN_DK]
                    vh = act_ref[g, prow, 2 * DN_W + h * DN_DK:2 * DN_W + (h + 1) * DN_DK]
                    if (key, h, p) not in gram:
                        khb = kh.astype(BF16)
                        gram[(key, h, p)] = (_dot_nt(jnp.concatenate([khb, qh.astype(BF16)], axis=0), khb),
                                             jnp.transpose(kh))
                    gr, kh_t = gram[(key, h, p)]
                    kk = gr[0:pt]
                    qk = gr[pt:2 * pt]
                    g_i = gates[ps, c8:c8 + 1]
                    g_j = gates_t[c8:c8 + 1, ps]
                    dec = jnp.exp(jnp.where(incl[d], g_i - g_j, -f32_inf))
                    b_i = gates[ps, GATE_BETA + c8:GATE_BETA + c8 + 1]
                    eg = jnp.exp(g_i)
                    kd_t = kh_t * jnp.exp(gates_t[GATE_TOT + c8:GATE_TOT + c8 + 1, ps] - g_j)
                    probs[(d, g, h, p)] = dict(
                        p=-(jnp.where(strict[d], kk * dec, 0.0) * b_i), qkm=(qk * dec).astype(BF16),
                        x=jnp.concatenate([vh * b_i, kh * (b_i * eg)], axis=-1), qd=qh * eg,
                        kd_t=[jnp.where(first_chunk_lanes, kd_t, 0.0).astype(BF16),
                              jnp.where(first_chunk_lanes, 0.0, kd_t).astype(BF16)],
                        tot=gates[ps, GATE_TOT + c8:GATE_TOT + c8 + 1])
        for pr in probs.values():
            pb = pr["p"].astype(BF16)
            pr["q"] = pr["p"]
            pr["p"] = _dot(pb, pb)
        for _ in range(4):
            for pr in probs.values():
                pb = pr["p"].astype(BF16)
                r = _dot(pb, jnp.concatenate([pb, pr["q"].astype(BF16)], axis=-1))
                pr["q"] = pr["q"] + pr["p"] + r[:, pt:2 * pt]
                pr["p"] = r[:, 0:pt]
        for pr in probs.values():
            r = _dot(pr["p"].astype(BF16), pr["q"].astype(BF16))
            qm = pr["q"] + pr["p"] + r
            pr["x"] = pr["x"] + _dot(qm.astype(BF16), pr["x"].astype(BF16))
        for pr in probs.values():
            xb = pr["x"].astype(BF16)
            qx = _dot(pr["qkm"], xb)
            pr["o0"] = qx[:, 0:DN_DK]
            pr["e"] = pr["qd"] - qx[:, DN_DK:2 * DN_DK]
            pr["kx"] = [_dot(kd_t, xb) for kd_t in pr["kd_t"]]
        chains = [(d, g, h) for d, g, _, _ in jobs for h in range(DN_HEADS)]
        state = {(d, g, h): (None if state_is_zero else st_ref[g, d * DN_HEADS + h]) for d, g, h in chains}
        out_c = {}
        for step in range(n_chunk):
            res = {}
            for ch in chains:
                if state[ch] is None:
                    continue
                c = step if ch[0] == 0 else n_chunk - 1 - step
                pr = probs[ch + (c // 2,)]
                cs = slice((c % 2) * CHUNK, (c % 2 + 1) * CHUNK)
                lhs = jnp.concatenate([-pr["kx"][c % 2][:, DN_DK:2 * DN_DK], pr["e"][cs]], axis=0)
                res[ch] = _dot(lhs.astype(BF16), state[ch].astype(BF16))
            for ch in chains:
                c = step if ch[0] == 0 else n_chunk - 1 - step
                pr = probs[ch + (c // 2,)]
                cs = slice((c % 2) * CHUNK, (c % 2 + 1) * CHUNK)
                b_c = pr["kx"][c % 2][:, 0:DN_DK]
                if state[ch] is None:
                    out_c[ch + (c,)] = pr["o0"][cs]
                    state[ch] = b_c
                else:
                    g_last = jnp.exp(pr["tot"][(c % 2) * CHUNK:(c % 2) * CHUNK + 1])
                    out_c[ch + (c,)] = res[ch][DN_DK:DN_DK + CHUNK] + pr["o0"][cs]
                    state[ch] = state[ch] * g_last + res[ch][0:DN_DK] + b_c
        outs = {}
        for ch in chains:
            d, g, h = ch
            st_ref[g, d * DN_HEADS + h] = state[ch]
            outs[ch] = jnp.concatenate([out_c[ch + (c,)] for c in range(n_chunk)], axis=0)
        return outs

    if n_blk == 1:
        outs = delta_blocks([(d, g, 0, g) for g in range(n_seq) for d in range(2)])
        for g in range(n_seq):
            for h in range(DN_HEADS):
                o_ref[g, :, h * DN_DK:(h + 1) * DN_DK] = outs[(0, g, h)] + outs[(1, g, h)]
    else:
        def scan_block(b, carry):
            r0 = (pl.multiple_of(b * bt, bt), pl.multiple_of((n_blk - 1 - b) * bt, bt))
            outs = delta_blocks([(d, g, r0[d], (g, d)) for g in range(n_seq) for d in range(2)])
            for (d, g, h), o in outs.items():
                o_ref[g, pl.ds(r0[d], bt), h * DN_DK:(h + 1) * DN_DK] += o
            return carry

        lax.fori_loop(0, n_blk, scan_block, 0)

    sfin_ref[...] = st_ref[...]


def _deltanet(act, gates, s0, n_seq):
    nb, seq_len, _ = act.shape
    n_state = 2 * DN_HEADS
    zero_init = s0 is None
    seq = lambda w, **kw: pl.BlockSpec((n_seq, seq_len, w), lambda b: (b, 0, 0), **kw)
    state = pl.BlockSpec((n_seq, n_state, DN_DK, DN_DK), lambda b: (b, 0, 0, 0))
    act_mode = dict(pipeline_mode=pl.Buffered(1)) if seq_len * n_seq > 4 * BLOCK_T else {}
    operands = (act, gates) + (() if zero_init else (s0,))
    return pl.pallas_call(
        functools.partial(_deltanet_kernel, seq_len=seq_len, n_seq=n_seq, zero_init=zero_init),
        grid=(nb // n_seq,),
        in_specs=[seq(QKV_W, **act_mode), seq(LANES)] + ([] if zero_init else [state]),
        out_specs=[seq(DN_W), state],
        out_shape=[jax.ShapeDtypeStruct((nb, seq_len, DN_W), F32),
                   jax.ShapeDtypeStruct((nb, n_state, DN_DK, DN_DK), F32)],
        scratch_shapes=[pltpu.VMEM((n_seq, n_state, DN_DK, DN_DK), F32)],
        compiler_params=_params(1),
        name="deltanet",
    )(*operands)


def _layer(x, mod, s0, row_len, n_seq, p):
    nb, seq_len, _ = x.shape
    n = nb * seq_len
    tiles_per_mod = (n // mod.shape[0]) // FFN_TILE
    x1, act, z, glu, gates = _ffn1_proj(x.reshape(n, D_MODEL), mod, p["norm_g"], p["wgu1"], p["wd1"],
                                        p["win"], p["dn_conv_w"], p["alog_row"], p["dtb_row"],
                                        tiles_per_mod, row_len)
    o, s_fin = _deltanet(act.reshape(nb, seq_len, QKV_W), gates.reshape(nb, seq_len, LANES), s0, n_seq)
    y = _out_ffn2(x1, o.reshape(n, DN_W), z, glu, mod, p["norm_g"], p["dn_norm_g"], p["cv_dw_w"],
                  p["cv_dw_b"], p["cv_ln_g"], p["cv_ln_b"], p["wout"], p["wgu2"], p["wd2"],
                  tiles_per_mod, row_len)
    return y.reshape(nb, seq_len, D_MODEL), s_fin


def kernel(x_prompt, x_sample, state_delta, c, c_ctx, w_mod, b_mod, norm_g, ffn1_w_in, ffn1_w_out, w_in,
           dn_conv_w, dn_a_log, dn_dt_bias, dn_norm_g, cv_dw_w, cv_dw_b, cv_ln_g, cv_ln_b, w_out,
           ffn2_w_in, ffn2_w_out):
    depth = w_mod.shape[0]
    assert depth == 1, "one trunk layer"
    batch, seq_len, _ = x_prompt.shape
    dec_batch, dec_seq, _ = x_sample.shape
    n_state = 2 * DN_HEADS

    cond = jnp.zeros((8, D_MODEL), F32).at[0].set(c_ctx).at[1:1 + dec_batch].set(c)
    mod = _modulation(cond, w_mod[0], b_mod[0]).reshape(8, N_MOD, D_MODEL)
    mod_ctx, mod_lat = mod[0:1], mod[1:1 + dec_batch]

    wi = w_in[0].astype(BF16)
    a0 = 4 * DN_W + 4 * DN_HEADS
    win = jnp.concatenate([wi[:, 0:QKV_W], wi[:, QKV_W + 4 * DN_HEADS:a0], wi[:, a0:],
                           wi[:, QKV_W:QKV_W + 4 * DN_HEADS],
                           jnp.zeros((D_MODEL, LANES - 4 * DN_HEADS), BF16)], axis=1)
    pad8 = lambda v: jnp.zeros((1, LANES), F32).at[0, 0:n_state].set(v.reshape(n_state))
    p = dict(norm_g=norm_g[0], wgu1=ffn1_w_in[0].astype(BF16), wd1=ffn1_w_out[0].astype(BF16), win=win,
             dn_conv_w=dn_conv_w[0], alog_row=pad8(dn_a_log[0]), dtb_row=pad8(dn_dt_bias[0]),
             dn_norm_g=dn_norm_g[0].reshape(1, DN_DK), cv_dw_w=cv_dw_w[0],
             cv_dw_b=cv_dw_b[0].reshape(1, CV_W), cv_ln_g=cv_ln_g[0].reshape(1, CV_W),
             cv_ln_b=cv_ln_b[0].reshape(1, CV_W), wout=w_out[0].astype(BF16),
             wgu2=ffn2_w_in[0].astype(BF16), wd2=ffn2_w_out[0].astype(BF16))

    y_p, s_ctx = _layer(x_prompt, mod_ctx, None, seq_len, CTX_SEQS_PER_STEP, p)
    s_lat = state_delta[:, 0].reshape(dec_batch, n_state, DN_DK, DN_DK)
    y_s, _ = _layer(x_sample, mod_lat, s_lat, GRID_W, 1, p)
    new_state = s_ctx.reshape(batch, 1, 2, DN_HEADS, DN_DK, DN_DK).astype(x_prompt.dtype)
    return (y_p, y_s, new_state)
```

```python
import functools

import jax
import jax.numpy as jnp
from jax import lax
from jax.experimental import pallas as pl
from jax.experimental.pallas import tpu as pltpu

F32 = jnp.float32
BF16 = jnp.bfloat16

D_MODEL = 1024
D_FF = 2816
N_MOD = 9
GRID_W = 64
DN_W = 512
CV_W = 512
DN_HEADS = 4
DN_DK = 128
CHUNK = 64
SHORT_CONV = 3
CONV_K = 31
EPS = 1e-6

LANES = 128
SUBLANES = 8
BLOCK_T = 256
PAIR_T = 2 * CHUNK
GATE_BETA = 2 * DN_HEADS
GATE_TOT = 4 * DN_HEADS
CTX_SEQS_PER_STEP = 1
FFN_TILE = 512
SUB_TILE = 256
CONV_HALO = 16
CONV_ROWS = 32
VMEM_LIMIT_BYTES = 56 * 1024 * 1024
FF_CHUNKS = ((0, 768), (768, 768), (1536, 768), (2304, 512))
QKV_W = 3 * DN_W
IN_QKV = (0, QKV_W)
IN_Z = (QKV_W, QKV_W + DN_W)
IN_GLU = (QKV_W + DN_W, QKV_W + DN_W + 2 * CV_W)
IN_AB = (QKV_W + DN_W + 2 * CV_W, QKV_W + DN_W + 2 * CV_W + LANES)
IN_COLS_PADDED = IN_AB[1]


def _dot(a, b):
    return jnp.dot(a, b, preferred_element_type=F32)


def _dot_nt(a, b):
    return lax.dot_general(a, b, (((1,), (1,)), ((), ())), preferred_element_type=F32)


def _dot_tn(a, b):
    return lax.dot_general(a, b, (((0,), (0,)), ((), ())), preferred_element_type=F32)


def _silu(x):
    return x * jax.nn.sigmoid(x)


def _rms(x, g):
    return x * lax.rsqrt(jnp.mean(x * x, axis=-1, keepdims=True) + EPS) * g


def _const_spec(shape):
    nd = len(shape)
    return pl.BlockSpec(shape, lambda *_: (0,) * nd, pipeline_mode=pl.Buffered(1))


def _params(n_grid_dims):
    return pltpu.CompilerParams(dimension_semantics=("arbitrary",) * n_grid_dims,
                                vmem_limit_bytes=VMEM_LIMIT_BYTES)


def _mod_kernel(c_ref, w_ref, b_ref, o_ref):
    s = _silu(c_ref[...]).astype(BF16)
    o_ref[...] = _dot(s, w_ref[...].astype(BF16)) + b_ref[...]


def _modulation(cond, w_mod, b_mod):
    n_out = w_mod.shape[1]
    tn = n_out // 4
    return pl.pallas_call(
        _mod_kernel,
        grid=(n_out // tn,),
        in_specs=[pl.BlockSpec((8, D_MODEL), lambda j: (0, 0)),
                  pl.BlockSpec((D_MODEL, tn), lambda j: (0, j)),
                  pl.BlockSpec((1, tn), lambda j: (0, j))],
        out_specs=pl.BlockSpec((8, tn), lambda j: (0, j)),
        out_shape=jax.ShapeDtypeStruct((8, n_out), F32),
        compiler_params=_params(1),
        name="modulation",
    )(cond, w_mod, b_mod.reshape(1, n_out))


def _reorder_w_in_kernel(w_ref, o_ref):
    ab0 = QKV_W
    z0 = ab0 + 4 * DN_HEADS
    glu0 = z0 + DN_W
    rows = w_ref.shape[0]
    o_ref[:, IN_QKV[0]:IN_QKV[1]] = w_ref[:, 0:QKV_W].astype(BF16)
    o_ref[:, IN_Z[0]:IN_Z[1]] = w_ref[:, z0:z0 + DN_W].astype(BF16)
    o_ref[:, IN_GLU[0]:IN_GLU[1]] = w_ref[:, glu0:glu0 + 2 * CV_W].astype(BF16)
    ab = jnp.concatenate([w_ref[:, ab0:z0], jnp.zeros((rows, LANES - 4 * DN_HEADS), F32)], axis=1)
    o_ref[:, IN_AB[0]:IN_AB[1]] = ab.astype(BF16)


def _reorder_w_in(w_in):
    rows, cols = w_in.shape
    tr = 256
    return pl.pallas_call(
        _reorder_w_in_kernel,
        grid=(rows // tr,),
        in_specs=[pl.BlockSpec((tr, cols), lambda i: (i, 0))],
        out_specs=pl.BlockSpec((tr, IN_COLS_PADDED), lambda i: (i, 0)),
        out_shape=jax.ShapeDtypeStruct((rows, IN_COLS_PADDED), BF16),
        compiler_params=_params(1),
        name="reorder_w_in",
    )(w_in)


def _ffn1_proj_kernel(x_ref, mod_ref, g_ref, wgu_ref, wd_ref, win_ref, cw_ref, alog_ref, dtb_ref,
                      x1_ref, act_ref, z_ref, glu_ref, gates_ref, *, row_len):
    m = mod_ref[...]
    g = g_ref[...]
    ts = SUB_TILE
    rowi = lax.broadcasted_iota(jnp.int32, (ts, LANES), 0)
    lanei = lax.broadcasted_iota(jnp.int32, (ts, LANES), 1)
    pos_row = jnp.bitwise_and(rowi, row_len - 1)
    keep_prev = pos_row != 0
    keep_next = pos_row != row_len - 1
    pos_chunk = jnp.bitwise_and(rowi, CHUNK - 1)

    def matmul_part(rows, out):
        x = x_ref[rows, :]
        hb = (_rms(x, g[0:1]) * (1.0 + m[1:2]) + m[0:1]).astype(BF16)
        acc = None
        for s, n in FF_CHUNKS:
            gt = _dot(hb, wgu_ref[:, s:s + n])
            up = _dot(hb, wgu_ref[:, D_FF + s:D_FF + s + n])
            a = (_silu(gt) * up).astype(BF16)
            p = _dot(a, wd_ref[s:s + n, :])
            acc = p if acc is None else acc + p
            yield
        x1 = x + 0.5 * (m[2:3] * _rms(acc, g[1:2]))
        x1_ref[rows, :] = x1
        h1 = (_rms(x1, g[2:3]) * (1.0 + m[4:5]) + m[3:4]).astype(BF16)
        out["qkv"] = _dot(h1, win_ref[:, IN_QKV[0]:IN_QKV[1]])
        out["ab"] = _dot(h1, win_ref[:, IN_AB[0]:IN_AB[1]])
        z_ref[rows, :] = _dot(h1, win_ref[:, IN_Z[0]:IN_Z[1]])
        glu_ref[rows, :] = _dot(h1, win_ref[:, IN_GLU[0]:IN_GLU[1]])

    def vector_part(rows, inp):
        qkv, ab = inp["qkv"], inp["ab"]
        for s in range(QKV_W // LANES):
            ls = slice(s * LANES, (s + 1) * LANES)
            xs = qkv[:, ls]
            w = cw_ref[:, ls]
            xp = jnp.where(keep_prev, pltpu.roll(xs, 1, 0), 0.0)
            xn = jnp.where(keep_next, pltpu.roll(xs, ts - 1, 0), 0.0)
            y = _silu(w[0:1] * xp + w[1:2] * xs + w[2:3] * xn)
            if s < 2 * DN_HEADS:
                y = y * lax.rsqrt(jnp.sum(y * y, axis=-1, keepdims=True) + EPS)
            if s < DN_HEADS:
                y = y * (DN_DK ** -0.5)
            act_ref[rows, ls] = y
            if s % 4 == 3:
                yield
        t = ab + dtb_ref[...]
        softplus = jnp.maximum(t, 0.0) + jnp.log1p(jnp.exp(-jnp.abs(t)))
        gate = -jnp.exp(alog_ref[...]) * softplus
        pre = gate
        suf = gate
        for sh in (1, 2, 4, 8, 16, 32):
            pre = pre + jnp.where(pos_chunk >= sh, pltpu.roll(pre, sh, 0), 0.0)
            suf = suf + jnp.where(pos_chunk < CHUNK - sh, pltpu.roll(suf, ts - sh, 0), 0.0)
        gc = jnp.where(lanei < DN_HEADS, pre, suf)
        tot = pltpu.roll(pre + suf - gate, GATE_TOT, 1)
        gates_ref[rows, :] = jnp.where(lanei < GATE_BETA, gc,
                                       jnp.where(lanei < GATE_TOT, jax.nn.sigmoid(ab), tot))

    n_sub = x_ref.shape[0] // ts
    carried = [dict() for _ in range(n_sub)]
    pending = None
    for s in range(n_sub + 1):
        mm = matmul_part(pl.ds(s * ts, ts), carried[s]) if s < n_sub else iter(())
        while True:
            more_mm = next(mm, StopIteration) is not StopIteration
            more_vec = pending is not None and next(pending, StopIteration) is not StopIteration
            if not more_mm and not more_vec:
                break
        pending = vector_part(pl.ds(s * ts, ts), carried[s]) if s < n_sub else None


def _ffn1_proj(x, mod, norm_g, wgu, wd, win, conv_w, alog_row, dtb_row, tiles_per_mod, row_len):
    n = x.shape[0]
    tm = FFN_TILE
    row = lambda w: pl.BlockSpec((tm, w), lambda i: (i, 0))
    return pl.pallas_call(
        functools.partial(_ffn1_proj_kernel, row_len=row_len),
        grid=(n // tm,),
        in_specs=[row(D_MODEL),
                  pl.BlockSpec((None, N_MOD, D_MODEL), lambda i: (i // tiles_per_mod, 0, 0)),
                  _const_spec(norm_g.shape), _const_spec(wgu.shape), _const_spec(wd.shape),
                  _const_spec(win.shape), _const_spec(conv_w.shape), _const_spec(alog_row.shape),
                  _const_spec(dtb_row.shape)],
        out_specs=[row(D_MODEL), row(QKV_W), row(DN_W), row(2 * CV_W), row(LANES)],
        out_shape=[jax.ShapeDtypeStruct((n, D_MODEL), F32), jax.ShapeDtypeStruct((n, QKV_W), F32),
                   jax.ShapeDtypeStruct((n, DN_W), F32), jax.ShapeDtypeStruct((n, 2 * CV_W), F32),
                   jax.ShapeDtypeStruct((n, LANES), F32)],
        compiler_params=_params(1),
        name="ffn1_proj",
    )(x, mod, norm_g, wgu, wd, win, conv_w, alog_row, dtb_row)


def _interleave(primary, secondary):
    more_a = more_b = True
    while more_a or more_b:
        if more_a:
            more_a = next(primary, StopIteration) is not StopIteration
        if more_b:
            more_b = secondary is not None and next(secondary, StopIteration) is not StopIteration


def _out_ffn2_kernel(x1_ref, o_ref, z_ref, glu_ref, mod_ref, g_ref, dng_ref, cw_ref, cb_ref, lg_ref, lb_ref,
                     wout_ref, wgu_ref, wd_ref, y_ref, xp_ref, cv_ref, *, row_len):
    m = mod_ref[...]
    g = g_ref[...]
    dng = dng_ref[...]
    ts = SUB_TILE
    n_sub = x1_ref.shape[0] // ts
    n_rows = ts // row_len
    pitch = row_len + 2 * CONV_HALO
    total = n_rows * pitch
    grp = CONV_ROWS
    lead = CONV_HALO - CONV_K // 2

    def vector_part(s, out):
        rows = pl.ds(s * ts, ts)
        heads = []
        for h in range(DN_HEADS):
            hs = slice(h * DN_DK, (h + 1) * DN_DK)
            heads.append((_rms(o_ref[rows, hs], dng) * _silu(z_ref[rows, hs])).astype(BF16))
        out["og"] = jnp.concatenate(heads, axis=-1)
        yield
        zeros = jnp.zeros((CONV_HALO, LANES), F32)
        for lg in range(CV_W // LANES):
            x = glu_ref[rows, lg * LANES:(lg + 1) * LANES] * jax.nn.sigmoid(
                glu_ref[rows, CV_W + lg * LANES:CV_W + (lg + 1) * LANES])
            for r in range(n_rows):
                base = r * pitch
                xp_ref[s, 0, lg, base:base + CONV_HALO, :] = zeros
                xp_ref[s, 0, lg, base + CONV_HALO:base + CONV_HALO + row_len, :] = (
                    x[r * row_len:(r + 1) * row_len, :])
                xp_ref[s, 0, lg, base + CONV_HALO + row_len:base + pitch, :] = zeros
            for b in range(1, SUBLANES):
                xp_ref[s, b, lg, 0:total - SUBLANES, :] = xp_ref[s, 0, lg, b:b + total - SUBLANES, :]
            yield
        for i in range(ts // grp):
            base = i * grp + (i * grp // row_len) * (2 * CONV_HALO)
            for lg in range(CV_W // LANES):
                ls = slice(lg * LANES, (lg + 1) * LANES)
                acc = jnp.zeros((grp, LANES), F32) + cb_ref[:, ls]
                for j in range(CONV_K):
                    shift = (lead + j) % SUBLANES
                    start = base + lead + j - shift
                    acc = acc + cw_ref[j:j + 1, ls] * xp_ref[s, shift, lg, start:start + grp, :]
                cv_ref[s, i * grp:(i + 1) * grp, ls] = acc
            yield
        c = cv_ref[s]
        mu = jnp.mean(c, axis=-1, keepdims=True)
        var = jnp.mean(jnp.square(c - mu), axis=-1, keepdims=True)
        cn = (c - mu) * lax.rsqrt(var + EPS) * lg_ref[...] + lb_ref[...]
        out["cv"] = _silu(cn).astype(BF16)

    def matmul_part(s, inp):
        rows = pl.ds(s * ts, ts)
        y = _dot(inp["og"], wout_ref[0:DN_W, :]) + _dot(inp["cv"], wout_ref[DN_W:, :])
        x2 = x1_ref[rows, :] + m[5:6] * _rms(y, g[3:4])
        hb = (_rms(x2, g[4:5]) * (1.0 + m[7:8]) + m[6:7]).astype(BF16)
        yield
        acc = None
        for c0, n in FF_CHUNKS:
            gt = _dot(hb, wgu_ref[:, c0:c0 + n])
            yield
            up = _dot(hb, wgu_ref[:, D_FF + c0:D_FF + c0 + n])
            a = (_silu(gt) * up).astype(BF16)
            yield
            p = _dot(a, wd_ref[c0:c0 + n, :])
            acc = p if acc is None else acc + p
            yield
        y_ref[rows, :] = x2 + 0.5 * (m[8:9] * _rms(acc, g[5:6]))

    carried = [dict() for _ in range(n_sub)]
    _interleave(vector_part(0, carried[0]), None)
    for s in range(n_sub):
        nxt = vector_part(s + 1, carried[s + 1]) if s + 1 < n_sub else None
        _interleave(matmul_part(s, carried[s]), nxt)


def _out_ffn2(x1, o, z, glu, mod, norm_g, dn_norm_g, conv_w, conv_b, ln_g, ln_b, wout, wgu, wd,
              tiles_per_mod, row_len):
    n = x1.shape[0]
    tm = FFN_TILE
    n_sub = tm // SUB_TILE
    total = (SUB_TILE // row_len) * (row_len + 2 * CONV_HALO)
    row = lambda w: pl.BlockSpec((tm, w), lambda i: (i, 0))
    consts = (norm_g, dn_norm_g, conv_w, conv_b, ln_g, ln_b, wout, wgu, wd)
    return pl.pallas_call(
        functools.partial(_out_ffn2_kernel, row_len=row_len),
        grid=(n // tm,),
        in_specs=[row(D_MODEL), row(DN_W), row(DN_W), row(2 * CV_W),
                  pl.BlockSpec((None, N_MOD, D_MODEL), lambda i: (i // tiles_per_mod, 0, 0))]
                 + [_const_spec(c.shape) for c in consts],
        out_specs=row(D_MODEL),
        out_shape=jax.ShapeDtypeStruct((n, D_MODEL), F32),
        scratch_shapes=[pltpu.VMEM((n_sub, SUBLANES, CV_W // LANES, total, LANES), F32),
                        pltpu.VMEM((n_sub, SUB_TILE, CV_W), F32)],
        compiler_params=_params(1),
        name="out_ffn2",
    )(x1, o, z, glu, mod, *consts)


def _deltanet_kernel(*refs, seq_len, n_seq, zero_init):
    if zero_init:
        act_ref, gates_ref, o_ref, sfin_ref, st_ref = refs
        s0_ref = None
    else:
        act_ref, gates_ref, s0_ref, o_ref, sfin_ref, st_ref = refs
    n_blk = seq_len // BLOCK_T
    bt = BLOCK_T
    f32_inf = jnp.float32(jnp.inf)

    n_state = 2 * DN_HEADS
    if zero_init:
        st_ref[...] = jnp.zeros((n_seq, n_state, DN_DK, DN_DK), F32)
    else:
        st_ref[...] = s0_ref[...]
    if n_blk > 1:
        o_ref[...] = jnp.zeros((n_seq, seq_len, DN_W), F32)

    state_is_zero = zero_init and n_blk == 1
    pt = PAIR_T
    n_pair = bt // pt
    n_chunk = bt // CHUNK
    ri = lax.broadcasted_iota(jnp.int32, (pt, pt), 0)
    ci = lax.broadcasted_iota(jnp.int32, (pt, pt), 1)
    same = jnp.right_shift(ri, 6) == jnp.right_shift(ci, 6)
    incl = (same & (ri >= ci), same & (ri <= ci))
    strict = (same & (ri > ci), same & (ri < ci))
    first_chunk_lanes = ci < CHUNK

    def delta_blocks(jobs):
        gram = {}
        probs = {}
        for d, g, r0, key in jobs:
            rows = pl.ds(r0, bt)
            gates = gates_ref[g, rows, :]
            gates_t = jnp.transpose(gates)
            for h in range(DN_HEADS):
                c8 = d * DN_HEADS + h
                for p in range(n_pair):
                    ps = slice(p * pt, (p + 1) * pt)
                    prow = pl.ds(r0 + p * pt, pt)
                    qh = act_ref[g, prow, h * DN_DK:(h + 1) * DN_DK]
                    kh = act_ref[g, prow, DN_W + h * DN_DK:DN_W + (h + 1) * DN_DK]
                    vh = act_ref[g, prow, 2 * DN_W + h * DN_DK:2 * DN_W + (h + 1) * DN_DK]
                    if (key, h, p) not in gram:
                        khb = kh.astype(BF16)
                        gram[(key, h, p)] = (_dot_nt(jnp.concatenate([khb, qh.astype(BF16)], axis=0), khb),
                                             jnp.transpose(kh))
                    gr, kh_t = gram[(key, h, p)]
                    kk = gr[0:pt]
                    qk = gr[pt:2 * pt]
                    g_i = gates[ps, c8:c8 + 1]
                    g_j = gates_t[c8:c8 + 1, ps]
                    dec = jnp.exp(jnp.where(incl[d], g_i - g_j, -f32_inf))
                    b_i = gates[ps, GATE_BETA + c8:GATE_BETA + c8 + 1]
                    eg = jnp.exp(g_i)
                    kd_t = kh_t * jnp.exp(gates_t[GATE_TOT + c8:GATE_TOT + c8 + 1, ps] - g_j)
                    probs[(d, g, h, p)] = dict(
                        p=-(jnp.where(strict[d], kk * dec, 0.0) * b_i), qkm=(qk * dec).astype(BF16),
                        x=jnp.concatenate([vh * b_i, kh * (b_i * eg)], axis=-1), qd=qh * eg,
                        kd_t=[jnp.where(first_chunk_lanes, kd_t, 0.0).astype(BF16),
                              jnp.where(first_chunk_lanes, 0.0, kd_t).astype(BF16)],
                        tot=gates[ps, GATE_TOT + c8:GATE_TOT + c8 + 1])
        for pr in probs.values():
            pb = pr["p"].astype(BF16)
            pr["q"] = pr["p"]
            pr["p"] = _dot(pb, pb)
        for _ in range(4):
            for pr in probs.values():
                pb = pr["p"].astype(BF16)
                r = _dot(pb, jnp.concatenate([pb, pr["q"].astype(BF16)], axis=-1))
                pr["q"] = pr["q"] + pr["p"] + r[:, pt:2 * pt]
                pr["p"] = r[:, 0:pt]
        for pr in probs.values():
            r = _dot(pr["p"].astype(BF16), pr["q"].astype(BF16))
            qm = pr["q"] + pr["p"] + r
            pr["x"] = pr["x"] + _dot(qm.astype(BF16), pr["x"].astype(BF16))
        for pr in probs.values():
            xb = pr["x"].astype(BF16)
            qx = _dot(pr["qkm"], xb)
            pr["o0"] = qx[:, 0:DN_DK]
            pr["e"] = pr["qd"] - qx[:, DN_DK:2 * DN_DK]
            pr["kx"] = [_dot(kd_t, xb) for kd_t in pr["kd_t"]]
        chains = [(d, g, h) for d, g, _, _ in jobs for h in range(DN_HEADS)]
        state = {(d, g, h): (None if state_is_zero else st_ref[g, d * DN_HEADS + h]) for d, g, h in chains}
        out_c = {}
        for step in range(n_chunk):
            res = {}
            for ch in chains:
                if state[ch] is None:
                    continue
                c = step if ch[0] == 0 else n_chunk - 1 - step
                pr = probs[ch + (c // 2,)]
                cs = slice((c % 2) * CHUNK, (c % 2 + 1) * CHUNK)
                lhs = jnp.concatenate([-pr["kx"][c % 2][:, DN_DK:2 * DN_DK], pr["e"][cs]], axis=0)
                res[ch] = _dot(lhs.astype(BF16), state[ch].astype(BF16))
            for ch in chains:
                c = step if ch[0] == 0 else n_chunk - 1 - step
                pr = probs[ch + (c // 2,)]
                cs = slice((c % 2) * CHUNK, (c % 2 + 1) * CHUNK)
                b_c = pr["kx"][c % 2][:, 0:DN_DK]
                if state[ch] is None:
                    out_c[ch + (c,)] = pr["o0"][cs]
                    state[ch] = b_c
                else:
                    g_last = jnp.exp(pr["tot"][(c % 2) * CHUNK:(c % 2) * CHUNK + 1])
                    out_c[ch + (c,)] = res[ch][DN_DK:DN_DK + CHUNK] + pr["o0"][cs]
                    state[ch] = state[ch] * g_last + res[ch][0:DN_DK] + b_c
        outs = {}
        for ch in chains:
            d, g, h = ch
            st_ref[g, d * DN_HEADS + h] = state[ch]
            outs[ch] = jnp.concatenate([out_c[ch + (c,)] for c in range(n_chunk)], axis=0)
        return outs

    if n_blk == 1:
        outs = delta_blocks([(d, g, 0, g) for g in range(n_seq) for d in range(2)])
        for g in range(n_seq):
            for h in range(DN_HEADS):
                o_ref[g, :, h * DN_DK:(h + 1) * DN_DK] = outs[(0, g, h)] + outs[(1, g, h)]
    else:
        def scan_block(b, carry):
            r0 = (pl.multiple_of(b * bt, bt), pl.multiple_of((n_blk - 1 - b) * bt, bt))
            outs = delta_blocks([(d, g, r0[d], (g, d)) for g in range(n_seq) for d in range(2)])
            for (d, g, h), o in outs.items():
                o_ref[g, pl.ds(r0[d], bt), h * DN_DK:(h + 1) * DN_DK] += o
            return carry

        lax.fori_loop(0, n_blk, scan_block, 0)

    sfin_ref[...] = st_ref[...]


def _deltanet(act, gates, s0, n_seq):
    nb, seq_len, _ = act.shape
    n_state = 2 * DN_HEADS
    zero_init = s0 is None
    seq = lambda w, **kw: pl.BlockSpec((n_seq, seq_len, w), lambda b: (b, 0, 0), **kw)
    state = pl.BlockSpec((n_seq, n_state, DN_DK, DN_DK), lambda b: (b, 0, 0, 0))
    act_mode = dict(pipeline_mode=pl.Buffered(1)) if seq_len * n_seq > 4 * BLOCK_T else {}
    operands = (act, gates) + (() if zero_init else (s0,))
    return pl.pallas_call(
        functools.partial(_deltanet_kernel, seq_len=seq_len, n_seq=n_seq, zero_init=zero_init),
        grid=(nb // n_seq,),
        in_specs=[seq(QKV_W, **act_mode), seq(LANES)] + ([] if zero_init else [state]),
        out_specs=[seq(DN_W), state],
        out_shape=[jax.ShapeDtypeStruct((nb, seq_len, DN_W), F32),
                   jax.ShapeDtypeStruct((nb, n_state, DN_DK, DN_DK), F32)],
        scratch_shapes=[pltpu.VMEM((n_seq, n_state, DN_DK, DN_DK), F32)],
        compiler_params=_params(1),
        name="deltanet",
    )(*operands)


def _layer(x, mod, s0, row_len, n_seq, p):
    nb, seq_len, _ = x.shape
    n = nb * seq_len
    tiles_per_mod = (n // mod.shape[0]) // FFN_TILE
    x1, act, z, glu, gates = _ffn1_proj(x.reshape(n, D_MODEL), mod, p["norm_g"], p["wgu1"], p["wd1"],
                                        p["win"], p["dn_conv_w"], p["alog_row"], p["dtb_row"],
                                        tiles_per_mod, row_len)
    o, s_fin = _deltanet(act.reshape(nb, seq_len, QKV_W), gates.reshape(nb, seq_len, LANES), s0, n_seq)
    y = _out_ffn2(x1, o.reshape(n, DN_W), z, glu, mod, p["norm_g"], p["dn_norm_g"], p["cv_dw_w"],
                  p["cv_dw_b"], p["cv_ln_g"], p["cv_ln_b"], p["wout"], p["wgu2"], p["wd2"],
                  tiles_per_mod, row_len)
    return y.reshape(nb, seq_len, D_MODEL), s_fin


def kernel(x_prompt, x_sample, state_delta, c, c_ctx, w_mod, b_mod, norm_g, ffn1_w_in, ffn1_w_out, w_in,
           dn_conv_w, dn_a_log, dn_dt_bias, dn_norm_g, cv_dw_w, cv_dw_b, cv_ln_g, cv_ln_b, w_out,
           ffn2_w_in, ffn2_w_out):
    depth = w_mod.shape[0]
    assert depth == 1, "one trunk layer"
    batch, seq_len, _ = x_prompt.shape
    dec_batch, dec_seq, _ = x_sample.shape
    n_state = 2 * DN_HEADS

    cond = jnp.zeros((8, D_MODEL), F32).at[0].set(c_ctx).at[1:1 + dec_batch].set(c)
    mod = _modulation(cond, w_mod[0], b_mod[0]).reshape(8, N_MOD, D_MODEL)
    mod_ctx, mod_lat = mod[0:1], mod[1:1 + dec_batch]

    win = _reorder_w_in(w_in[0])
    pad8 = lambda v: jnp.zeros((1, LANES), F32).at[0, 0:n_state].set(v.reshape(n_state))
    p = dict(norm_g=norm_g[0], wgu1=ffn1_w_in[0].astype(BF16), wd1=ffn1_w_out[0].astype(BF16), win=win,
             dn_conv_w=dn_conv_w[0], alog_row=pad8(dn_a_log[0]), dtb_row=pad8(dn_dt_bias[0]),
             dn_norm_g=dn_norm_g[0].reshape(1, DN_DK), cv_dw_w=cv_dw_w[0],
             cv_dw_b=cv_dw_b[0].reshape(1, CV_W), cv_ln_g=cv_ln_g[0].reshape(1, CV_W),
             cv_ln_b=cv_ln_b[0].reshape(1, CV_W), wout=w_out[0].astype(BF16),
             wgu2=ffn2_w_in[0].astype(BF16), wd2=ffn2_w_out[0].astype(BF16))

    y_p, s_ctx = _layer(x_prompt, mod_ctx, None, seq_len, CTX_SEQS_PER_STEP, p)
    s_lat = state_delta[:, 0].reshape(dec_batch, n_state, DN_DK, DN_DK)
    y_s, _ = _layer(x_sample, mod_lat, s_lat, GRID_W, 1, p)
    new_state = s_ctx.reshape(batch, 1, 2, DN_HEADS, DN_DK, DN_DK).astype(x_prompt.dtype)
    return (y_p, y_s, new_state)
```

```python
import functools

import jax
import jax.numpy as jnp
from jax import lax
from jax.experimental import pallas as pl
from jax.experimental.pallas import tpu as pltpu

F32 = jnp.float32
BF16 = jnp.bfloat16

D_MODEL = 1024
D_FF = 2816
N_MOD = 9
GRID_W = 64
DN_W = 512
CV_W = 512
DN_HEADS = 4
DN_DK = 128
CHUNK = 64
SHORT_CONV = 3
CONV_K = 31
EPS = 1e-6

LANES = 128
SUBLANES = 8
BLOCK_T = 256
PAIR_T = 2 * CHUNK
GATE_BETA = 2 * DN_HEADS
GATE_TOT = 4 * DN_HEADS
CTX_SEQS_PER_STEP = 1
FFN_TILE = 512
SUB_TILE = 256
CONV_HALO = 16
CONV_ROWS = 32
VMEM_LIMIT_BYTES = 56 * 1024 * 1024
FF_CHUNKS = ((0, 768), (768, 768), (1536, 768), (2304, 512))
QKV_W = 3 * DN_W
IN_QKV = (0, QKV_W)
IN_Z = (QKV_W, QKV_W + DN_W)
IN_GLU = (QKV_W + DN_W, QKV_W + DN_W + 2 * CV_W)
IN_AB = (QKV_W + DN_W + 2 * CV_W, QKV_W + DN_W + 2 * CV_W + LANES)
IN_COLS_PADDED = IN_AB[1]


def _dot(a, b):
    return jnp.dot(a, b, preferred_element_type=F32)


def _dot_nt(a, b):
    return lax.dot_general(a, b, (((1,), (1,)), ((), ())), preferred_element_type=F32)


def _dot_tn(a, b):
    return lax.dot_general(a, b, (((0,), (0,)), ((), ())), preferred_element_type=F32)


def _silu(x):
    return x * jax.nn.sigmoid(x)


def _rms(x, g):
    return x * lax.rsqrt(jnp.mean(x * x, axis=-1, keepdims=True) + EPS) * g


def _const_spec(shape):
    nd = len(shape)
    return pl.BlockSpec(shape, lambda *_: (0,) * nd, pipeline_mode=pl.Buffered(1))


def _params(n_grid_dims):
    return pltpu.CompilerParams(dimension_semantics=("arbitrary",) * n_grid_dims,
                                vmem_limit_bytes=VMEM_LIMIT_BYTES)


def _mod_kernel(c_ref, w_ref, b_ref, o_ref):
    s = _silu(c_ref[...]).astype(BF16)
    o_ref[...] = _dot(s, w_ref[...].astype(BF16)) + b_ref[...]


def _modulation(cond, w_mod, b_mod):
    n_out = w_mod.shape[1]
    tn = n_out // 4
    return pl.pallas_call(
        _mod_kernel,
        grid=(n_out // tn,),
        in_specs=[pl.BlockSpec((8, D_MODEL), lambda j: (0, 0)),
                  pl.BlockSpec((D_MODEL, tn), lambda j: (0, j)),
                  pl.BlockSpec((1, tn), lambda j: (0, j))],
        out_specs=pl.BlockSpec((8, tn), lambda j: (0, j)),
        out_shape=jax.ShapeDtypeStruct((8, n_out), F32),
        compiler_params=_params(1),
        name="modulation",
    )(cond, w_mod, b_mod.reshape(1, n_out))


def _reorder_w_in_kernel(w_ref, o_ref):
    ab0 = QKV_W
    z0 = ab0 + 4 * DN_HEADS
    glu0 = z0 + DN_W
    rows = w_ref.shape[0]
    o_ref[:, IN_QKV[0]:IN_QKV[1]] = w_ref[:, 0:QKV_W].astype(BF16)
    o_ref[:, IN_Z[0]:IN_Z[1]] = w_ref[:, z0:z0 + DN_W].astype(BF16)
    o_ref[:, IN_GLU[0]:IN_GLU[1]] = w_ref[:, glu0:glu0 + 2 * CV_W].astype(BF16)
    ab = jnp.concatenate([w_ref[:, ab0:z0], jnp.zeros((rows, LANES - 4 * DN_HEADS), F32)], axis=1)
    o_ref[:, IN_AB[0]:IN_AB[1]] = ab.astype(BF16)


def _reorder_w_in(w_in):
    rows, cols = w_in.shape
    tr = 256
    return pl.pallas_call(
        _reorder_w_in_kernel,
        grid=(rows // tr,),
        in_specs=[pl.BlockSpec((tr, cols), lambda i: (i, 0))],
        out_specs=pl.BlockSpec((tr, IN_COLS_PADDED), lambda i: (i, 0)),
        out_shape=jax.ShapeDtypeStruct((rows, IN_COLS_PADDED), BF16),
        compiler_params=_params(1),
        name="reorder_w_in",
    )(w_in)


def _ffn1_proj_kernel(x_ref, mod_ref, g_ref, wgu_ref, wd_ref, win_ref, cw_ref, alog_ref, dtb_ref,
                      ccw_ref, ccb_ref, clg_ref, clb_ref,
                      x1_ref, act_ref, z_ref, cv_ref, gates_ref, xp_ref, cacc_ref, *, row_len):
    m = mod_ref[...]
    g = g_ref[...]
    ts = SUB_TILE
    n_rows = ts // row_len
    pitch = row_len + 2 * CONV_HALO
    total = n_rows * pitch
    grp = CONV_ROWS
    lead = CONV_HALO - CONV_K // 2
    rowi = lax.broadcasted_iota(jnp.int32, (ts, LANES), 0)
    lanei = lax.broadcasted_iota(jnp.int32, (ts, LANES), 1)
    pos_row = jnp.bitwise_and(rowi, row_len - 1)
    keep_prev = pos_row != 0
    keep_next = pos_row != row_len - 1
    pos_chunk = jnp.bitwise_and(rowi, CHUNK - 1)

    def matmul_part(rows, out):
        x = x_ref[rows, :]
        hb = (_rms(x, g[0:1]) * (1.0 + m[1:2]) + m[0:1]).astype(BF16)
        acc = None
        for s, n in FF_CHUNKS:
            gt = _dot(hb, wgu_ref[:, s:s + n])
            up = _dot(hb, wgu_ref[:, D_FF + s:D_FF + s + n])
            a = (_silu(gt) * up).astype(BF16)
            p = _dot(a, wd_ref[s:s + n, :])
            acc = p if acc is None else acc + p
            yield
        x1 = x + 0.5 * (m[2:3] * _rms(acc, g[1:2]))
        x1_ref[rows, :] = x1
        h1 = (_rms(x1, g[2:3]) * (1.0 + m[4:5]) + m[3:4]).astype(BF16)
        out["qkv"] = _dot(h1, win_ref[:, IN_QKV[0]:IN_QKV[1]])
        out["ab"] = _dot(h1, win_ref[:, IN_AB[0]:IN_AB[1]])
        z_ref[rows, :] = _dot(h1, win_ref[:, IN_Z[0]:IN_Z[1]])
        out["glu"] = _dot(h1, win_ref[:, IN_GLU[0]:IN_GLU[1]])

    def vector_part(rows, inp):
        qkv, ab, glu = inp["qkv"], inp["ab"], inp["glu"]
        for s in range(QKV_W // LANES):
            ls = slice(s * LANES, (s + 1) * LANES)
            xs = qkv[:, ls]
            w = cw_ref[:, ls]
            xp = jnp.where(keep_prev, pltpu.roll(xs, 1, 0), 0.0)
            xn = jnp.where(keep_next, pltpu.roll(xs, ts - 1, 0), 0.0)
            y = _silu(w[0:1] * xp + w[1:2] * xs + w[2:3] * xn)
            if s < 2 * DN_HEADS:
                y = y * lax.rsqrt(jnp.sum(y * y, axis=-1, keepdims=True) + EPS)
            if s < DN_HEADS:
                y = y * (DN_DK ** -0.5)
            act_ref[rows, ls] = y
            if s % 4 == 3:
                yield
        t = ab + dtb_ref[...]
        softplus = jnp.maximum(t, 0.0) + jnp.log1p(jnp.exp(-jnp.abs(t)))
        gate = -jnp.exp(alog_ref[...]) * softplus
        pre = gate
        suf = gate
        for sh in (1, 2, 4, 8, 16, 32):
            pre = pre + jnp.where(pos_chunk >= sh, pltpu.roll(pre, sh, 0), 0.0)
            suf = suf + jnp.where(pos_chunk < CHUNK - sh, pltpu.roll(suf, ts - sh, 0), 0.0)
        gc = jnp.where(lanei < DN_HEADS, pre, suf)
        tot = pltpu.roll(pre + suf - gate, GATE_TOT, 1)
        gates_ref[rows, :] = jnp.where(lanei < GATE_BETA, gc,
                                       jnp.where(lanei < GATE_TOT, jax.nn.sigmoid(ab), tot))
        yield
        zeros = jnp.zeros((CONV_HALO, LANES), F32)
        for lg in range(CV_W // LANES):
            x = glu[:, lg * LANES:(lg + 1) * LANES] * jax.nn.sigmoid(
                glu[:, CV_W + lg * LANES:CV_W + (lg + 1) * LANES])
            for r in range(n_rows):
                base = r * pitch
                xp_ref[0, lg, base:base + CONV_HALO, :] = zeros
                xp_ref[0, lg, base + CONV_HALO:base + CONV_HALO + row_len, :] = (
                    x[r * row_len:(r + 1) * row_len, :])
                xp_ref[0, lg, base + CONV_HALO + row_len:base + pitch, :] = zeros
            for b in range(1, SUBLANES):
                xp_ref[b, lg, 0:total - SUBLANES, :] = xp_ref[0, lg, b:b + total - SUBLANES, :]
            yield
        for i in range(ts // grp):
            base = i * grp + (i * grp // row_len) * (2 * CONV_HALO)
            for lg in range(CV_W // LANES):
                ls = slice(lg * LANES, (lg + 1) * LANES)
                acc = jnp.zeros((grp, LANES), F32) + ccb_ref[:, ls]
                for j in range(CONV_K):
                    shift = (lead + j) % SUBLANES
                    start = base + lead + j - shift
                    acc = acc + ccw_ref[j:j + 1, ls] * xp_ref[shift, lg, start:start + grp, :]
                cacc_ref[i * grp:(i + 1) * grp, ls] = acc
            yield
        c = cacc_ref[...]
        mu = jnp.mean(c, axis=-1, keepdims=True)
        var = jnp.mean(jnp.square(c - mu), axis=-1, keepdims=True)
        cn = (c - mu) * lax.rsqrt(var + EPS) * clg_ref[...] + clb_ref[...]
        cv_ref[rows, :] = _silu(cn).astype(BF16)

    n_sub = x_ref.shape[0] // ts
    carried = [dict() for _ in range(n_sub)]
    pending = None
    for s in range(n_sub + 1):
        mm = matmul_part(pl.ds(s * ts, ts), carried[s]) if s < n_sub else iter(())
        while True:
            more_mm = next(mm, StopIteration) is not StopIteration
            more_vec = pending is not None and next(pending, StopIteration) is not StopIteration
            if not more_mm and not more_vec:
                break
        pending = vector_part(pl.ds(s * ts, ts), carried[s]) if s < n_sub else None


def _ffn1_proj(x, mod, norm_g, wgu, wd, win, conv_w, alog_row, dtb_row, cv_w, cv_b, cv_ln_g, cv_ln_b,
               tiles_per_mod, row_len):
    n = x.shape[0]
    tm = FFN_TILE
    total = (SUB_TILE // row_len) * (row_len + 2 * CONV_HALO)
    row = lambda w: pl.BlockSpec((tm, w), lambda i: (i, 0))
    consts = (norm_g, wgu, wd, win, conv_w, alog_row, dtb_row, cv_w, cv_b, cv_ln_g, cv_ln_b)
    return pl.pallas_call(
        functools.partial(_ffn1_proj_kernel, row_len=row_len),
        grid=(n // tm,),
        in_specs=[row(D_MODEL),
                  pl.BlockSpec((None, N_MOD, D_MODEL), lambda i: (i // tiles_per_mod, 0, 0))]
                 + [_const_spec(c.shape) for c in consts],
        out_specs=[row(D_MODEL), row(QKV_W), row(DN_W), row(CV_W), row(LANES)],
        out_shape=[jax.ShapeDtypeStruct((n, D_MODEL), F32), jax.ShapeDtypeStruct((n, QKV_W), F32),
                   jax.ShapeDtypeStruct((n, DN_W), F32), jax.ShapeDtypeStruct((n, CV_W), BF16),
                   jax.ShapeDtypeStruct((n, LANES), F32)],
        scratch_shapes=[pltpu.VMEM((SUBLANES, CV_W // LANES, total, LANES), F32),
                        pltpu.VMEM((SUB_TILE, CV_W), F32)],
        compiler_params=_params(1),
        name="ffn1_proj",
    )(x, mod, *consts)


def _interleave(primary, secondary):
    more_a = more_b = True
    while more_a or more_b:
        if more_a:
            more_a = next(primary, StopIteration) is not StopIteration
        if more_b:
            more_b = secondary is not None and next(secondary, StopIteration) is not StopIteration


def _out_ffn2_kernel(x1_ref, o_ref, z_ref, cv_ref, mod_ref, g_ref, dng_ref, wout_ref, wgu_ref, wd_ref, y_ref):
    m = mod_ref[...]
    g = g_ref[...]
    dng = dng_ref[...]
    ts = SUB_TILE
    n_sub = x1_ref.shape[0] // ts

    def vector_part(s, out):
        rows = pl.ds(s * ts, ts)
        heads = []
        for h in range(DN_HEADS):
            hs = slice(h * DN_DK, (h + 1) * DN_DK)
            heads.append((_rms(o_ref[rows, hs], dng) * _silu(z_ref[rows, hs])).astype(BF16))
        out["og"] = jnp.concatenate(heads, axis=-1)
        yield

    def matmul_part(s, inp):
        rows = pl.ds(s * ts, ts)
        y = _dot(inp["og"], wout_ref[0:DN_W, :]) + _dot(cv_ref[rows, :], wout_ref[DN_W:, :])
        x2 = x1_ref[rows, :] + m[5:6] * _rms(y, g[3:4])
        hb = (_rms(x2, g[4:5]) * (1.0 + m[7:8]) + m[6:7]).astype(BF16)
        yield
        acc = None
        for c0, n in FF_CHUNKS:
            gt = _dot(hb, wgu_ref[:, c0:c0 + n])
            yield
            up = _dot(hb, wgu_ref[:, D_FF + c0:D_FF + c0 + n])
            a = (_silu(gt) * up).astype(BF16)
            yield
            p = _dot(a, wd_ref[c0:c0 + n, :])
            acc = p if acc is None else acc + p
            yield
        y_ref[rows, :] = x2 + 0.5 * (m[8:9] * _rms(acc, g[5:6]))

    carried = [dict() for _ in range(n_sub)]
    _interleave(vector_part(0, carried[0]), None)
    for s in range(n_sub):
        nxt = vector_part(s + 1, carried[s + 1]) if s + 1 < n_sub else None
        _interleave(matmul_part(s, carried[s]), nxt)


def _out_ffn2(x1, o, z, cv, mod, norm_g, dn_norm_g, wout, wgu, wd, tiles_per_mod):
    n = x1.shape[0]
    tm = FFN_TILE
    row = lambda w: pl.BlockSpec((tm, w), lambda i: (i, 0))
    consts = (norm_g, dn_norm_g, wout, wgu, wd)
    return pl.pallas_call(
        _out_ffn2_kernel,
        grid=(n // tm,),
        in_specs=[row(D_MODEL), row(DN_W), row(DN_W), row(CV_W),
                  pl.BlockSpec((None, N_MOD, D_MODEL), lambda i: (i // tiles_per_mod, 0, 0))]
                 + [_const_spec(c.shape) for c in consts],
        out_specs=row(D_MODEL),
        out_shape=jax.ShapeDtypeStruct((n, D_MODEL), F32),
        compiler_params=_params(1),
        name="out_ffn2",
    )(x1, o, z, cv, mod, *consts)


def _deltanet_kernel(*refs, seq_len, n_seq, zero_init):
    if zero_init:
        act_ref, gates_ref, o_ref, sfin_ref, st_ref = refs
        s0_ref = None
    else:
        act_ref, gates_ref, s0_ref, o_ref, sfin_ref, st_ref = refs
    n_blk = seq_len // BLOCK_T
    bt = BLOCK_T
    f32_inf = jnp.float32(jnp.inf)

    n_state = 2 * DN_HEADS
    if zero_init:
        st_ref[...] = jnp.zeros((n_seq, n_state, DN_DK, DN_DK), F32)
    else:
        st_ref[...] = s0_ref[...]
    if n_blk > 1:
        o_ref[...] = jnp.zeros((n_seq, seq_len, DN_W), F32)

    state_is_zero = zero_init and n_blk == 1
    pt = PAIR_T
    n_pair = bt // pt
    n_chunk = bt // CHUNK
    ri = lax.broadcasted_iota(jnp.int32, (pt, pt), 0)
    ci = lax.broadcasted_iota(jnp.int32, (pt, pt), 1)
    same = jnp.right_shift(ri, 6) == jnp.right_shift(ci, 6)
    incl = (same & (ri >= ci), same & (ri <= ci))
    strict = (same & (ri > ci), same & (ri < ci))
    first_chunk_lanes = ci < CHUNK

    def delta_blocks(jobs):
        gram = {}
        probs = {}
        for d, g, r0, key in jobs:
            rows = pl.ds(r0, bt)
            gates = gates_ref[g, rows, :]
            gates_t = jnp.transpose(gates)
            for h in range(DN_HEADS):
                c8 = d * DN_HEADS + h
                for p in range(n_pair):
                    ps = slice(p * pt, (p + 1) * pt)
                    prow = pl.ds(r0 + p * pt, pt)
                    qh = act_ref[g, prow, h * DN_DK:(h + 1) * DN_DK]
                    kh = act_ref[g, prow, DN_W + h * DN_DK:DN_W + (h + 1) * DN_DK]
                    vh = act_ref[g, prow, 2 * DN_W + h * DN_DK:2 * DN_W + (h + 1) * DN_DK]
                    if (key, h, p) not in gram:
                        khb = kh.astype(BF16)
                        gram[(key, h, p)] = (_dot_nt(jnp.concatenate([khb, qh.astype(BF16)], axis=0), khb),
                                             jnp.transpose(kh))
                    gr, kh_t = gram[(key, h, p)]
                    kk = gr[0:pt]
                    qk = gr[pt:2 * pt]
                    g_i = gates[ps, c8:c8 + 1]
                    g_j = gates_t[c8:c8 + 1, ps]
                    dec = jnp.exp(jnp.where(incl[d], g_i - g_j, -f32_inf))
                    b_i = gates[ps, GATE_BETA + c8:GATE_BETA + c8 + 1]
                    eg = jnp.exp(g_i)
                    kd_t = kh_t * jnp.exp(gates_t[GATE_TOT + c8:GATE_TOT + c8 + 1, ps] - g_j)
                    probs[(d, g, h, p)] = dict(
                        p=-(jnp.where(strict[d], kk * dec, 0.0) * b_i), qkm=(qk * dec).astype(BF16),
                        x=jnp.concatenate([vh * b_i, kh * (b_i * eg)], axis=-1), qd=qh * eg,
                        kd_t=[jnp.where(first_chunk_lanes, kd_t, 0.0).astype(BF16),
                              jnp.where(first_chunk_lanes, 0.0, kd_t).astype(BF16)],
                        tot=gates[ps, GATE_TOT + c8:GATE_TOT + c8 + 1])
        for pr in probs.values():
            pb = pr["p"].astype(BF16)
            pr["q"] = pr["p"]
            pr["p"] = _dot(pb, pb)
        for _ in range(4):
            for pr in probs.values():
                pb = pr["p"].astype(BF16)
                r = _dot(pb, jnp.concatenate([pb, pr["q"].astype(BF16)], axis=-1))
                pr["q"] = pr["q"] + pr["p"] + r[:, pt:2 * pt]
                pr["p"] = r[:, 0:pt]
        for pr in probs.values():
            r = _dot(pr["p"].astype(BF16), pr["q"].astype(BF16))
            qm = pr["q"] + pr["p"] + r
            pr["x"] = pr["x"] + _dot(qm.astype(BF16), pr["x"].astype(BF16))
        for pr in probs.values():
            xb = pr["x"].astype(BF16)
            qx = _dot(pr["qkm"], xb)
            pr["o0"] = qx[:, 0:DN_DK]
            pr["e"] = pr["qd"] - qx[:, DN_DK:2 * DN_DK]
            pr["kx"] = [_dot(kd_t, xb) for kd_t in pr["kd_t"]]
        chains = [(d, g, h) for d, g, _, _ in jobs for h in range(DN_HEADS)]
        state = {(d, g, h): (None if state_is_zero else st_ref[g, d * DN_HEADS + h]) for d, g, h in chains}
        out_c = {}
        for step in range(n_chunk):
            res = {}
            for ch in chains:
                if state[ch] is None:
                    continue
                c = step if ch[0] == 0 else n_chunk - 1 - step
                pr = probs[ch + (c // 2,)]
                cs = slice((c % 2) * CHUNK, (c % 2 + 1) * CHUNK)
                lhs = jnp.concatenate([-pr["kx"][c % 2][:, DN_DK:2 * DN_DK], pr["e"][cs]], axis=0)
                res[ch] = _dot(lhs.astype(BF16), state[ch].astype(BF16))
            for ch in chains:
                c = step if ch[0] == 0 else n_chunk - 1 - step
                pr = probs[ch + (c // 2,)]
                cs = slice((c % 2) * CHUNK, (c % 2 + 1) * CHUNK)
                b_c = pr["kx"][c % 2][:, 0:DN_DK]
                if state[ch] is None:
                    out_c[ch + (c,)] = pr["o0"][cs]
                    state[ch] = b_c
                else:
                    g_last = jnp.exp(pr["tot"][(c % 2) * CHUNK:(c % 2) * CHUNK + 1])
                    out_c[ch + (c,)] = res[ch][DN_DK:DN_DK + CHUNK] + pr["o0"][cs]
                    state[ch] = state[ch] * g_last + res[ch][0:DN_DK] + b_c
        outs = {}
        for ch in chains:
            d, g, h = ch
            st_ref[g, d * DN_HEADS + h] = state[ch]
            outs[ch] = jnp.concatenate([out_c[ch + (c,)] for c in range(n_chunk)], axis=0)
        return outs

    if n_blk == 1:
        outs = delta_blocks([(d, g, 0, g) for g in range(n_seq) for d in range(2)])
        for g in range(n_seq):
            for h in range(DN_HEADS):
                o_ref[g, :, h * DN_DK:(h + 1) * DN_DK] = outs[(0, g, h)] + outs[(1, g, h)]
    else:
        def scan_block(b, carry):
            r0 = (pl.multiple_of(b * bt, bt), pl.multiple_of((n_blk - 1 - b) * bt, bt))
            outs = delta_blocks([(d, g, r0[d], (g, d)) for g in range(n_seq) for d in range(2)])
            for (d, g, h), o in outs.items():
                o_ref[g, pl.ds(r0[d], bt), h * DN_DK:(h + 1) * DN_DK] += o
            return carry

        lax.fori_loop(0, n_blk, scan_block, 0)

    sfin_ref[...] = st_ref[...]


def _deltanet(act, gates, s0, n_seq):
    nb, seq_len, _ = act.shape
    n_state = 2 * DN_HEADS
    zero_init = s0 is None
    seq = lambda w, **kw: pl.BlockSpec((n_seq, seq_len, w), lambda b: (b, 0, 0), **kw)
    state = pl.BlockSpec((n_seq, n_state, DN_DK, DN_DK), lambda b: (b, 0, 0, 0))
    act_mode = dict(pipeline_mode=pl.Buffered(1)) if seq_len * n_seq > 4 * BLOCK_T else {}
    operands = (act, gates) + (() if zero_init else (s0,))
    return pl.pallas_call(
        functools.partial(_deltanet_kernel, seq_len=seq_len, n_seq=n_seq, zero_init=zero_init),
        grid=(nb // n_seq,),
        in_specs=[seq(QKV_W, **act_mode), seq(LANES)] + ([] if zero_init else [state]),
        out_specs=[seq(DN_W), state],
        out_shape=[jax.ShapeDtypeStruct((nb, seq_len, DN_W), F32),
                   jax.ShapeDtypeStruct((nb, n_state, DN_DK, DN_DK), F32)],
        scratch_shapes=[pltpu.VMEM((n_seq, n_state, DN_DK, DN_DK), F32)],
        compiler_params=_params(1),
        name="deltanet",
    )(*operands)


def _layer(x, mod, s0, row_len, n_seq, p):
    nb, seq_len, _ = x.shape
    n = nb * seq_len
    tiles_per_mod = (n // mod.shape[0]) // FFN_TILE
    x1, act, z, cv, gates = _ffn1_proj(x.reshape(n, D_MODEL), mod, p["norm_g"], p["wgu1"], p["wd1"],
                                       p["win"], p["dn_conv_w"], p["alog_row"], p["dtb_row"],
                                       p["cv_dw_w"], p["cv_dw_b"], p["cv_ln_g"], p["cv_ln_b"],
                                       tiles_per_mod, row_len)
    o, s_fin = _deltanet(act.reshape(nb, seq_len, QKV_W), gates.reshape(nb, seq_len, LANES), s0, n_seq)
    y = _out_ffn2(x1, o.reshape(n, DN_W), z, cv, mod, p["norm_g"], p["dn_norm_g"], p["wout"],
                  p["wgu2"], p["wd2"], tiles_per_mod)
    return y.reshape(nb, seq_len, D_MODEL), s_fin


def kernel(x_prompt, x_sample, state_delta, c, c_ctx, w_mod, b_mod, norm_g, ffn1_w_in, ffn1_w_out, w_in,
           dn_conv_w, dn_a_log, dn_dt_bias, dn_norm_g, cv_dw_w, cv_dw_b, cv_ln_g, cv_ln_b, w_out,
           ffn2_w_in, ffn2_w_out):
    depth = w_mod.shape[0]
    assert depth == 1, "one trunk layer"
    batch, seq_len, _ = x_prompt.shape
    dec_batch, dec_seq, _ = x_sample.shape
    n_state = 2 * DN_HEADS

    cond = jnp.zeros((8, D_MODEL), F32).at[0].set(c_ctx).at[1:1 + dec_batch].set(c)
    mod = _modulation(cond, w_mod[0], b_mod[0]).reshape(8, N_MOD, D_MODEL)
    mod_ctx, mod_lat = mod[0:1], mod[1:1 + dec_batch]

    win = _reorder_w_in(w_in[0])
    pad8 = lambda v: jnp.zeros((1, LANES), F32).at[0, 0:n_state].set(v.reshape(n_state))
    p = dict(norm_g=norm_g[0], wgu1=ffn1_w_in[0].astype(BF16), wd1=ffn1_w_out[0].astype(BF16), win=win,
             dn_conv_w=dn_conv_w[0], alog_row=pad8(dn_a_log[0]), dtb_row=pad8(dn_dt_bias[0]),
             dn_norm_g=dn_norm_g[0].reshape(1, DN_DK), cv_dw_w=cv_dw_w[0],
             cv_dw_b=cv_dw_b[0].reshape(1, CV_W), cv_ln_g=cv_ln_g[0].reshape(1, CV_W),
             cv_ln_b=cv_ln_b[0].reshape(1, CV_W), wout=w_out[0].astype(BF16),
             wgu2=ffn2_w_in[0].astype(BF16), wd2=ffn2_w_out[0].astype(BF16))

    y_p, s_ctx = _layer(x_prompt, mod_ctx, None, seq_len, CTX_SEQS_PER_STEP, p)
    s_lat = state_delta[:, 0].reshape(dec_batch, n_state, DN_DK, DN_DK)
    y_s, _ = _layer(x_sample, mod_lat, s_lat, GRID_W, 1, p)
    new_state = s_ctx.reshape(batch, 1, 2, DN_HEADS, DN_DK, DN_DK).astype(x_prompt.dtype)
    return (y_p, y_s, new_state)
```

```python
import functools

import jax
import jax.numpy as jnp
from jax import lax
from jax.experimental import pallas as pl
from jax.experimental.pallas import tpu as pltpu

F32 = jnp.float32
BF16 = jnp.bfloat16

D_MODEL = 1024
D_FF = 2816
N_MOD = 9
GRID_W = 64
DN_W = 512
CV_W = 512
DN_HEADS = 4
DN_DK = 128
CHUNK = 64
SHORT_CONV = 3
CONV_K = 31
EPS = 1e-6

LANES = 128
SUBLANES = 8
BLOCK_T = 256
PAIR_T = 2 * CHUNK
GATE_BETA = 2 * DN_HEADS
GATE_TOT = 4 * DN_HEADS
CTX_SEQS_PER_STEP = 1
MOD_STEPS = 4
FFN_TILE = 512
SUB_TILE = 256
CONV_HALO = 16
CONV_ROWS = 32
VMEM_LIMIT_BYTES = 56 * 1024 * 1024
FF_CHUNKS = ((0, 768), (768, 768), (1536, 768), (2304, 512))
QKV_W = 3 * DN_W
IN_QKV = (0, QKV_W)
IN_Z = (QKV_W, QKV_W + DN_W)
IN_GLU = (QKV_W + DN_W, QKV_W + DN_W + 2 * CV_W)
IN_AB = (QKV_W + DN_W + 2 * CV_W, QKV_W + DN_W + 2 * CV_W + LANES)
IN_COLS_PADDED = IN_AB[1]


def _dot(a, b):
    return jnp.dot(a, b, preferred_element_type=F32)


def _dot_nt(a, b):
    return lax.dot_general(a, b, (((1,), (1,)), ((), ())), preferred_element_type=F32)


def _silu(x):
    return x * jax.nn.sigmoid(x)


def _rms(x, g):
    return x * lax.rsqrt(jnp.mean(x * x, axis=-1, keepdims=True) + EPS) * g


def _const_spec(shape):
    nd = len(shape)
    return pl.BlockSpec(shape, lambda *_: (0,) * nd, pipeline_mode=pl.Buffered(1))


def _params(n_grid_dims):
    return pltpu.CompilerParams(dimension_semantics=("arbitrary",) * n_grid_dims,
                                vmem_limit_bytes=VMEM_LIMIT_BYTES)


def _mod_kernel(c_ref, w_ref, b_ref, o_ref):
    s = _silu(c_ref[...]).astype(BF16)
    o_ref[...] = _dot(s, w_ref[...].astype(BF16)) + b_ref[...]


def _modulation(cond, w_mod, b_mod):
    rows = cond.shape[0]
    n_out = w_mod.shape[1]
    tn = n_out // MOD_STEPS
    return pl.pallas_call(
        _mod_kernel,
        grid=(MOD_STEPS,),
        in_specs=[pl.BlockSpec((rows, D_MODEL), lambda j: (0, 0)),
                  pl.BlockSpec((D_MODEL, tn), lambda j: (0, j)),
                  pl.BlockSpec((1, tn), lambda j: (0, j))],
        out_specs=pl.BlockSpec((rows, tn), lambda j: (0, j)),
        out_shape=jax.ShapeDtypeStruct((rows, n_out), F32),
        compiler_params=_params(1),
        name="modulation",
    )(cond, w_mod, b_mod.reshape(1, n_out))


def _reorder_w_in_kernel(w_ref, o_ref):
    ab0 = QKV_W
    z0 = ab0 + 4 * DN_HEADS
    glu0 = z0 + DN_W
    rows = w_ref.shape[0]
    o_ref[:, IN_QKV[0]:IN_QKV[1]] = w_ref[:, 0:QKV_W].astype(BF16)
    o_ref[:, IN_Z[0]:IN_Z[1]] = w_ref[:, z0:z0 + DN_W].astype(BF16)
    o_ref[:, IN_GLU[0]:IN_GLU[1]] = w_ref[:, glu0:glu0 + 2 * CV_W].astype(BF16)
    ab = jnp.concatenate([w_ref[:, ab0:z0], jnp.zeros((rows, LANES - 4 * DN_HEADS), F32)], axis=1)
    o_ref[:, IN_AB[0]:IN_AB[1]] = ab.astype(BF16)


def _reorder_w_in(w_in):
    rows, cols = w_in.shape
    tr = SUB_TILE
    return pl.pallas_call(
        _reorder_w_in_kernel,
        grid=(rows // tr,),
        in_specs=[pl.BlockSpec((tr, cols), lambda i: (i, 0))],
        out_specs=pl.BlockSpec((tr, IN_COLS_PADDED), lambda i: (i, 0)),
        out_shape=jax.ShapeDtypeStruct((rows, IN_COLS_PADDED), BF16),
        compiler_params=_params(1),
        name="reorder_w_in",
    )(w_in)


def _interleave(primary, secondary):
    more_a = more_b = True
    while more_a or more_b:
        if more_a:
            more_a = next(primary, StopIteration) is not StopIteration
        if more_b:
            more_b = secondary is not None and next(secondary, StopIteration) is not StopIteration


def _ffn1_proj_kernel(x_ref, mod_ref, g_ref, wgu_ref, wd_ref, win_ref, cw_ref, alog_ref, dtb_ref,
                      ccw_ref, ccb_ref, clg_ref, clb_ref,
                      x1_ref, act_ref, z_ref, cv_ref, gates_ref, xp_ref, cacc_ref, *, row_len):
    m = mod_ref[...]
    g = g_ref[...]
    ts = SUB_TILE
    n_rows = ts // row_len
    pitch = row_len + 2 * CONV_HALO
    total = n_rows * pitch
    grp = CONV_ROWS
    lead = CONV_HALO - CONV_K // 2
    rowi = lax.broadcasted_iota(jnp.int32, (ts, LANES), 0)
    lanei = lax.broadcasted_iota(jnp.int32, (ts, LANES), 1)
    pos_row = jnp.bitwise_and(rowi, row_len - 1)
    keep_prev = pos_row != 0
    keep_next = pos_row != row_len - 1
    pos_chunk = jnp.bitwise_and(rowi, CHUNK - 1)

    def matmul_part(rows, out):
        x = x_ref[rows, :]
        hb = (_rms(x, g[0:1]) * (1.0 + m[1:2]) + m[0:1]).astype(BF16)
        acc = None
        for s, n in FF_CHUNKS:
            gt = _dot(hb, wgu_ref[:, s:s + n])
            up = _dot(hb, wgu_ref[:, D_FF + s:D_FF + s + n])
            a = (_silu(gt) * up).astype(BF16)
            p = _dot(a, wd_ref[s:s + n, :])
            acc = p if acc is None else acc + p
            yield
        x1 = x + 0.5 * (m[2:3] * _rms(acc, g[1:2]))
        x1_ref[rows, :] = x1
        h1 = (_rms(x1, g[2:3]) * (1.0 + m[4:5]) + m[3:4]).astype(BF16)
        out["qkv"] = _dot(h1, win_ref[:, IN_QKV[0]:IN_QKV[1]])
        out["ab"] = _dot(h1, win_ref[:, IN_AB[0]:IN_AB[1]])
        z_ref[rows, :] = _dot(h1, win_ref[:, IN_Z[0]:IN_Z[1]])
        out["glu"] = _dot(h1, win_ref[:, IN_GLU[0]:IN_GLU[1]])

    def vector_part(rows, inp):
        qkv, ab, glu = inp["qkv"], inp["ab"], inp["glu"]
        for s in range(QKV_W // LANES):
            ls = slice(s * LANES, (s + 1) * LANES)
            xs = qkv[:, ls]
            w = cw_ref[:, ls]
            xp = jnp.where(keep_prev, pltpu.roll(xs, 1, 0), 0.0)
            xn = jnp.where(keep_next, pltpu.roll(xs, ts - 1, 0), 0.0)
            y = _silu(w[0:1] * xp + w[1:2] * xs + w[2:3] * xn)
            if s < 2 * DN_HEADS:
                y = y * lax.rsqrt(jnp.sum(y * y, axis=-1, keepdims=True) + EPS)
            if s < DN_HEADS:
                y = y * (DN_DK ** -0.5)
            act_ref[rows, ls] = y
            if s % 4 == 3:
                yield
        t = ab + dtb_ref[...]
        softplus = jnp.maximum(t, 0.0) + jnp.log1p(jnp.exp(-jnp.abs(t)))
        gate = -jnp.exp(alog_ref[...]) * softplus
        pre = gate
        suf = gate
        for sh in (1, 2, 4, 8, 16, 32):
            pre = pre + jnp.where(pos_chunk >= sh, pltpu.roll(pre, sh, 0), 0.0)
            suf = suf + jnp.where(pos_chunk < CHUNK - sh, pltpu.roll(suf, ts - sh, 0), 0.0)
        gc = jnp.where(lanei < DN_HEADS, pre, suf)
        tot = pltpu.roll(pre + suf - gate, GATE_TOT, 1)
        gates_ref[rows, :] = jnp.where(lanei < GATE_BETA, gc,
                                       jnp.where(lanei < GATE_TOT, jax.nn.sigmoid(ab), tot))
        yield
        zeros = jnp.zeros((CONV_HALO, LANES), F32)
        for lg in range(CV_W // LANES):
            x = glu[:, lg * LANES:(lg + 1) * LANES] * jax.nn.sigmoid(
                glu[:, CV_W + lg * LANES:CV_W + (lg + 1) * LANES])
            for r in range(n_rows):
                base = r * pitch
                xp_ref[0, lg, base:base + CONV_HALO, :] = zeros
                xp_ref[0, lg, base + CONV_HALO:base + CONV_HALO + row_len, :] = (
                    x[r * row_len:(r + 1) * row_len, :])
                xp_ref[0, lg, base + CONV_HALO + row_len:base + pitch, :] = zeros
            for b in range(1, SUBLANES):
                xp_ref[b, lg, 0:total - SUBLANES, :] = xp_ref[0, lg, b:b + total - SUBLANES, :]
            yield
        for i in range(ts // grp):
            base = i * grp + (i * grp // row_len) * (2 * CONV_HALO)
            for lg in range(CV_W // LANES):
                ls = slice(lg * LANES, (lg + 1) * LANES)
                acc = jnp.zeros((grp, LANES), F32) + ccb_ref[:, ls]
                for j in range(CONV_K):
                    shift = (lead + j) % SUBLANES
                    start = base + lead + j - shift
                    acc = acc + ccw_ref[j:j + 1, ls] * xp_ref[shift, lg, start:start + grp, :]
                cacc_ref[i * grp:(i + 1) * grp, ls] = acc
            yield
        c = cacc_ref[...]
        mu = jnp.mean(c, axis=-1, keepdims=True)
        var = jnp.mean(jnp.square(c - mu), axis=-1, keepdims=True)
        cn = (c - mu) * lax.rsqrt(var + EPS) * clg_ref[...] + clb_ref[...]
        cv_ref[rows, :] = _silu(cn).astype(BF16)

    n_sub = x_ref.shape[0] // ts
    carried = [dict() for _ in range(n_sub)]
    pending = None
    for s in range(n_sub):
        rows = pl.ds(s * ts, ts)
        _interleave(matmul_part(rows, carried[s]), pending)
        pending = vector_part(rows, carried[s])
    _interleave(pending, None)


def _ffn1_proj(x, mod, norm_g, wgu, wd, win, conv_w, alog_row, dtb_row, cv_w, cv_b, cv_ln_g, cv_ln_b,
               tiles_per_mod, row_len):
    n = x.shape[0]
    tm = FFN_TILE
    total = (SUB_TILE // row_len) * (row_len + 2 * CONV_HALO)
    row = lambda w: pl.BlockSpec((tm, w), lambda i: (i, 0))
    consts = (norm_g, wgu, wd, win, conv_w, alog_row, dtb_row, cv_w, cv_b, cv_ln_g, cv_ln_b)
    return pl.pallas_call(
        functools.partial(_ffn1_proj_kernel, row_len=row_len),
        grid=(n // tm,),
        in_specs=[row(D_MODEL),
                  pl.BlockSpec((None, N_MOD, D_MODEL), lambda i: (i // tiles_per_mod, 0, 0))]
                 + [_const_spec(c.shape) for c in consts],
        out_specs=[row(D_MODEL), row(QKV_W), row(DN_W), row(CV_W), row(LANES)],
        out_shape=[jax.ShapeDtypeStruct((n, D_MODEL), F32), jax.ShapeDtypeStruct((n, QKV_W), F32),
                   jax.ShapeDtypeStruct((n, DN_W), F32), jax.ShapeDtypeStruct((n, CV_W), BF16),
                   jax.ShapeDtypeStruct((n, LANES), F32)],
        scratch_shapes=[pltpu.VMEM((SUBLANES, CV_W // LANES, total, LANES), F32),
                        pltpu.VMEM((SUB_TILE, CV_W), F32)],
        compiler_params=_params(1),
        name="ffn1_proj",
    )(x, mod, *consts)


def _out_ffn2_kernel(x1_ref, o_ref, z_ref, cv_ref, mod_ref, g_ref, dng_ref, wout_ref, wgu_ref, wd_ref, y_ref):
    m = mod_ref[...]
    g = g_ref[...]
    dng = dng_ref[...]
    ts = SUB_TILE
    n_sub = x1_ref.shape[0] // ts

    def vector_part(s, out):
        rows = pl.ds(s * ts, ts)
        heads = []
        for h in range(DN_HEADS):
            hs = slice(h * DN_DK, (h + 1) * DN_DK)
            heads.append((_rms(o_ref[rows, hs], dng) * _silu(z_ref[rows, hs])).astype(BF16))
        out["og"] = jnp.concatenate(heads, axis=-1)
        yield

    def matmul_part(s, inp):
        rows = pl.ds(s * ts, ts)
        y = _dot(inp["og"], wout_ref[0:DN_W, :]) + _dot(cv_ref[rows, :], wout_ref[DN_W:, :])
        x2 = x1_ref[rows, :] + m[5:6] * _rms(y, g[3:4])
        hb = (_rms(x2, g[4:5]) * (1.0 + m[7:8]) + m[6:7]).astype(BF16)
        yield
        acc = None
        for c0, n in FF_CHUNKS:
            gt = _dot(hb, wgu_ref[:, c0:c0 + n])
            yield
            up = _dot(hb, wgu_ref[:, D_FF + c0:D_FF + c0 + n])
            a = (_silu(gt) * up).astype(BF16)
            yield
            p = _dot(a, wd_ref[c0:c0 + n, :])
            acc = p if acc is None else acc + p
            yield
        y_ref[rows, :] = x2 + 0.5 * (m[8:9] * _rms(acc, g[5:6]))

    carried = [dict() for _ in range(n_sub)]
    _interleave(vector_part(0, carried[0]), None)
    for s in range(n_sub):
        nxt = vector_part(s + 1, carried[s + 1]) if s + 1 < n_sub else None
        _interleave(matmul_part(s, carried[s]), nxt)


def _out_ffn2(x1, o, z, cv, mod, norm_g, dn_norm_g, wout, wgu, wd, tiles_per_mod):
    n = x1.shape[0]
    tm = FFN_TILE
    row = lambda w: pl.BlockSpec((tm, w), lambda i: (i, 0))
    consts = (norm_g, dn_norm_g, wout, wgu, wd)
    return pl.pallas_call(
        _out_ffn2_kernel,
        grid=(n // tm,),
        in_specs=[row(D_MODEL), row(DN_W), row(DN_W), row(CV_W),
                  pl.BlockSpec((None, N_MOD, D_MODEL), lambda i: (i // tiles_per_mod, 0, 0))]
                 + [_const_spec(c.shape) for c in consts],
        out_specs=row(D_MODEL),
        out_shape=jax.ShapeDtypeStruct((n, D_MODEL), F32),
        compiler_params=_params(1),
        name="out_ffn2",
    )(x1, o, z, cv, mod, *consts)


def _deltanet_kernel(*refs, seq_len, n_seq, zero_init):
    if zero_init:
        act_ref, gates_ref, o_ref, sfin_ref, st_ref = refs
        s0_ref = None
    else:
        act_ref, gates_ref, s0_ref, o_ref, sfin_ref, st_ref = refs
    n_blk = seq_len // BLOCK_T
    bt = BLOCK_T
    f32_inf = jnp.float32(jnp.inf)

    n_state = 2 * DN_HEADS
    if zero_init:
        st_ref[...] = jnp.zeros((n_seq, n_state, DN_DK, DN_DK), F32)
    else:
        st_ref[...] = s0_ref[...]
    if n_blk > 1:
        o_ref[...] = jnp.zeros((n_seq, seq_len, DN_W), F32)

    state_is_zero = zero_init and n_blk == 1
    pt = PAIR_T
    n_pair = bt // pt
    n_chunk = bt // CHUNK
    ri = lax.broadcasted_iota(jnp.int32, (pt, pt), 0)
    ci = lax.broadcasted_iota(jnp.int32, (pt, pt), 1)
    same = jnp.right_shift(ri, 6) == jnp.right_shift(ci, 6)
    incl = (same & (ri >= ci), same & (ri <= ci))
    strict = (same & (ri > ci), same & (ri < ci))
    first_chunk_lanes = ci < CHUNK

    def delta_blocks(jobs):
        gram = {}
        probs = {}
        for d, g, r0, key in jobs:
            rows = pl.ds(r0, bt)
            gates = gates_ref[g, rows, :]
            gates_t = jnp.transpose(gates)
            for h in range(DN_HEADS):
                c8 = d * DN_HEADS + h
                for p in range(n_pair):
                    ps = slice(p * pt, (p + 1) * pt)
                    prow = pl.ds(r0 + p * pt, pt)
                    qh = act_ref[g, prow, h * DN_DK:(h + 1) * DN_DK]
                    kh = act_ref[g, prow, DN_W + h * DN_DK:DN_W + (h + 1) * DN_DK]
                    vh = act_ref[g, prow, 2 * DN_W + h * DN_DK:2 * DN_W + (h + 1) * DN_DK]
                    if (key, h, p) not in gram:
                        khb = kh.astype(BF16)
                        gram[(key, h, p)] = (_dot_nt(jnp.concatenate([khb, qh.astype(BF16)], axis=0), khb),
                                             jnp.transpose(kh))
                    gr, kh_t = gram[(key, h, p)]
                    kk = gr[0:pt]
                    qk = gr[pt:2 * pt]
                    g_i = gates[ps, c8:c8 + 1]
                    g_j = gates_t[c8:c8 + 1, ps]
                    dec = jnp.exp(jnp.where(incl[d], g_i - g_j, -f32_inf))
                    b_i = gates[ps, GATE_BETA + c8:GATE_BETA + c8 + 1]
                    eg = jnp.exp(g_i)
                    kd_t = kh_t * jnp.exp(gates_t[GATE_TOT + c8:GATE_TOT + c8 + 1, ps] - g_j)
                    probs[(d, g, h, p)] = dict(
                        p=-(jnp.where(strict[d], kk * dec, 0.0) * b_i), qkm=(qk * dec).astype(BF16),
                        x=jnp.concatenate([vh * b_i, kh * (b_i * eg)], axis=-1), qd=qh * eg,
                        kd_t=[jnp.where(first_chunk_lanes, kd_t, 0.0).astype(BF16),
                              jnp.where(first_chunk_lanes, 0.0, kd_t).astype(BF16)],
                        tot=gates[ps, GATE_TOT + c8:GATE_TOT + c8 + 1])
        for pr in probs.values():
            pb = pr["p"].astype(BF16)
            pr["q"] = pr["p"]
            pr["p"] = _dot(pb, pb)
        for _ in range(4):
            for pr in probs.values():
                pb = pr["p"].astype(BF16)
                r = _dot(pb, jnp.concatenate([pb, pr["q"].astype(BF16)], axis=-1))
                pr["q"] = pr["q"] + pr["p"] + r[:, pt:2 * pt]
                pr["p"] = r[:, 0:pt]
        for pr in probs.values():
            r = _dot(pr["p"].astype(BF16), pr["q"].astype(BF16))
            qm = pr["q"] + pr["p"] + r
            pr["x"] = pr["x"] + _dot(qm.astype(BF16), pr["x"].astype(BF16))
        for pr in probs.values():
            xb = pr["x"].astype(BF16)
            qx = _dot(pr["qkm"], xb)
            pr["o0"] = qx[:, 0:DN_DK]
            pr["e"] = pr["qd"] - qx[:, DN_DK:2 * DN_DK]
            pr["kx"] = [_dot(kd_t, xb) for kd_t in pr["kd_t"]]
        chains = [(d, g, h) for d, g, _, _ in jobs for h in range(DN_HEADS)]
        state = {(d, g, h): (None if state_is_zero else st_ref[g, d * DN_HEADS + h]) for d, g, h in chains}
        out_c = {}
        for step in range(n_chunk):
            res = {}
            for ch in chains:
                if state[ch] is None:
                    continue
                c = step if ch[0] == 0 else n_chunk - 1 - step
                pr = probs[ch + (c // 2,)]
                cs = slice((c % 2) * CHUNK, (c % 2 + 1) * CHUNK)
                lhs = jnp.concatenate([-pr["kx"][c % 2][:, DN_DK:2 * DN_DK], pr["e"][cs]], axis=0)
                res[ch] = _dot(lhs.astype(BF16), state[ch].astype(BF16))
            for ch in chains:
                c = step if ch[0] == 0 else n_chunk - 1 - step
                pr = probs[ch + (c // 2,)]
                cs = slice((c % 2) * CHUNK, (c % 2 + 1) * CHUNK)
                b_c = pr["kx"][c % 2][:, 0:DN_DK]
                if state[ch] is None:
                    out_c[ch + (c,)] = pr["o0"][cs]
                    state[ch] = b_c
                else:
                    g_last = jnp.exp(pr["tot"][(c % 2) * CHUNK:(c % 2) * CHUNK + 1])
                    out_c[ch + (c,)] = res[ch][DN_DK:DN_DK + CHUNK] + pr["o0"][cs]
                    state[ch] = state[ch] * g_last + res[ch][0:DN_DK] + b_c
        outs = {}
        for ch in chains:
            d, g, h = ch
            st_ref[g, d * DN_HEADS + h] = state[ch]
            outs[ch] = jnp.concatenate([out_c[ch + (c,)] for c in range(n_chunk)], axis=0)
        return outs

    if n_blk == 1:
        outs = delta_blocks([(d, g, 0, g) for g in range(n_seq) for d in range(2)])
        for g in range(n_seq):
            for h in range(DN_HEADS):
                o_ref[g, :, h * DN_DK:(h + 1) * DN_DK] = outs[(0, g, h)] + outs[(1, g, h)]
    else:
        def scan_block(b, carry):
            r0 = (pl.multiple_of(b * bt, bt), pl.multiple_of((n_blk - 1 - b) * bt, bt))
            outs = delta_blocks([(d, g, r0[d], (g, d)) for g in range(n_seq) for d in range(2)])
            for (d, g, h), o in outs.items():
                o_ref[g, pl.ds(r0[d], bt), h * DN_DK:(h + 1) * DN_DK] += o
            return carry

        lax.fori_loop(0, n_blk, scan_block, 0)

    sfin_ref[...] = st_ref[...]


def _deltanet(act, gates, s0, n_seq):
    nb, seq_len, _ = act.shape
    n_state = 2 * DN_HEADS
    zero_init = s0 is None
    seq = lambda w, **kw: pl.BlockSpec((n_seq, seq_len, w), lambda b: (b, 0, 0), **kw)
    state = pl.BlockSpec((n_seq, n_state, DN_DK, DN_DK), lambda b: (b, 0, 0, 0))
    act_mode = dict(pipeline_mode=pl.Buffered(1)) if seq_len * n_seq > 4 * BLOCK_T else {}
    operands = (act, gates) + (() if zero_init else (s0,))
    return pl.pallas_call(
        functools.partial(_deltanet_kernel, seq_len=seq_len, n_seq=n_seq, zero_init=zero_init),
        grid=(nb // n_seq,),
        in_specs=[seq(QKV_W, **act_mode), seq(LANES)] + ([] if zero_init else [state]),
        out_specs=[seq(DN_W), state],
        out_shape=[jax.ShapeDtypeStruct((nb, seq_len, DN_W), F32),
                   jax.ShapeDtypeStruct((nb, n_state, DN_DK, DN_DK), F32)],
        scratch_shapes=[pltpu.VMEM((n_seq, n_state, DN_DK, DN_DK), F32)],
        compiler_params=_params(1),
        name="deltanet",
    )(*operands)


def _layer(x, mod, s0, row_len, n_seq, p):
    nb, seq_len, _ = x.shape
    n = nb * seq_len
    tiles_per_mod = (n // mod.shape[0]) // FFN_TILE
    x1, act, z, cv, gates = _ffn1_proj(x.reshape(n, D_MODEL), mod, p["norm_g"], p["wgu1"], p["wd1"],
                                       p["win"], p["dn_conv_w"], p["alog_row"], p["dtb_row"],
                                       p["cv_dw_w"], p["cv_dw_b"], p["cv_ln_g"], p["cv_ln_b"],
                                       tiles_per_mod, row_len)
    o, s_fin = _deltanet(act.reshape(nb, seq_len, QKV_W), gates.reshape(nb, seq_len, LANES), s0, n_seq)
    y = _out_ffn2(x1, o.reshape(n, DN_W), z, cv, mod, p["norm_g"], p["dn_norm_g"], p["wout"],
                  p["wgu2"], p["wd2"], tiles_per_mod)
    return y.reshape(nb, seq_len, D_MODEL), s_fin


def kernel(x_prompt, x_sample, state_delta, c, c_ctx, w_mod, b_mod, norm_g, ffn1_w_in, ffn1_w_out, w_in,
           dn_conv_w, dn_a_log, dn_dt_bias, dn_norm_g, cv_dw_w, cv_dw_b, cv_ln_g, cv_ln_b, w_out,
           ffn2_w_in, ffn2_w_out):
    depth = w_mod.shape[0]
    assert depth == 1, "one trunk layer"
    batch, seq_len, _ = x_prompt.shape
    dec_batch, dec_seq, _ = x_sample.shape
    n_state = 2 * DN_HEADS

    assert 1 + dec_batch <= SUBLANES and dn_conv_w.shape[1] == SHORT_CONV and cv_dw_w.shape[1] == CONV_K
    cond = jnp.zeros((SUBLANES, D_MODEL), F32).at[0].set(c_ctx).at[1:1 + dec_batch].set(c)
    mod = _modulation(cond, w_mod[0], b_mod[0]).reshape(SUBLANES, N_MOD, D_MODEL)
    mod_ctx, mod_lat = mod[0:1], mod[1:1 + dec_batch]

    win = _reorder_w_in(w_in[0])
    pad8 = lambda v: jnp.zeros((1, LANES), F32).at[0, 0:n_state].set(v.reshape(n_state))
    p = dict(norm_g=norm_g[0], wgu1=ffn1_w_in[0].astype(BF16), wd1=ffn1_w_out[0].astype(BF16), win=win,
             dn_conv_w=dn_conv_w[0], alog_row=pad8(dn_a_log[0]), dtb_row=pad8(dn_dt_bias[0]),
             dn_norm_g=dn_norm_g[0].reshape(1, DN_DK), cv_dw_w=cv_dw_w[0],
             cv_dw_b=cv_dw_b[0].reshape(1, CV_W), cv_ln_g=cv_ln_g[0].reshape(1, CV_W),
             cv_ln_b=cv_ln_b[0].reshape(1, CV_W), wout=w_out[0].astype(BF16),
             wgu2=ffn2_w_in[0].astype(BF16), wd2=ffn2_w_out[0].astype(BF16))

    y_p, s_ctx = _layer(x_prompt, mod_ctx, None, seq_len, CTX_SEQS_PER_STEP, p)
    s_lat = state_delta[:, 0].reshape(dec_batch, n_state, DN_DK, DN_DK)
    y_s, _ = _layer(x_sample, mod_lat, s_lat, GRID_W, 1, p)
    new_state = s_ctx.reshape(batch, 1, 2, DN_HEADS, DN_DK, DN_DK).astype(x_prompt.dtype)
    return (y_p, y_s, new_state)
```

```python
import functools

import jax
import jax.numpy as jnp
from jax import lax
from jax.experimental import pallas as pl
from jax.experimental.pallas import tpu as pltpu

F32 = jnp.float32
BF16 = jnp.bfloat16

D_MODEL = 1024
D_FF = 2816
N_MOD = 9
GRID_W = 64
DN_W = 512
CV_W = 512
DN_HEADS = 4
DN_DK = 128
CHUNK = 64
SHORT_CONV = 3
CONV_K = 31
EPS = 1e-6

LANES = 128
SUBLANES = 8
BLOCK_T = 256
PAIR_T = 2 * CHUNK
GATE_BETA = 2 * DN_HEADS
GATE_TOT = 4 * DN_HEADS
CTX_SEQS_PER_STEP = 1
MOD_STEPS = 4
FFN_TILE = 512
SUB_TILE = 256
CONV_HALO = 16
CONV_ROWS = 32
VMEM_LIMIT_BYTES = 56 * 1024 * 1024
FF_CHUNKS = ((0, 768), (768, 768), (1536, 768), (2304, 512))
QKV_W = 3 * DN_W
IN_QKV = (0, QKV_W)
IN_Z = (QKV_W, QKV_W + DN_W)
IN_GLU = (QKV_W + DN_W, QKV_W + DN_W + 2 * CV_W)
IN_AB = (QKV_W + DN_W + 2 * CV_W, QKV_W + DN_W + 2 * CV_W + LANES)
IN_COLS_PADDED = IN_AB[1]


def _dot(a, b):
    return jnp.dot(a, b, preferred_element_type=F32)


def _dot_nt(a, b):
    return lax.dot_general(a, b, (((1,), (1,)), ((), ())), preferred_element_type=F32)


def _silu(x):
    return x * jax.nn.sigmoid(x)


def _rms(x, g):
    return x * lax.rsqrt(jnp.mean(x * x, axis=-1, keepdims=True) + EPS) * g


def _const_spec(shape):
    nd = len(shape)
    return pl.BlockSpec(shape, lambda *_: (0,) * nd, pipeline_mode=pl.Buffered(1))


def _params(n_grid_dims):
    return pltpu.CompilerParams(dimension_semantics=("arbitrary",) * n_grid_dims,
                                vmem_limit_bytes=VMEM_LIMIT_BYTES)


def _mod_kernel(c_ref, w_ref, b_ref, o_ref):
    s = _silu(c_ref[...]).astype(BF16)
    o_ref[...] = _dot(s, w_ref[...].astype(BF16)) + b_ref[...]


def _modulation(cond, w_mod, b_mod):
    rows = cond.shape[0]
    n_out = w_mod.shape[1]
    tn = n_out // MOD_STEPS
    return pl.pallas_call(
        _mod_kernel,
        grid=(MOD_STEPS,),
        in_specs=[pl.BlockSpec((rows, D_MODEL), lambda j: (0, 0)),
                  pl.BlockSpec((D_MODEL, tn), lambda j: (0, j)),
                  pl.BlockSpec((1, tn), lambda j: (0, j))],
        out_specs=pl.BlockSpec((rows, tn), lambda j: (0, j)),
        out_shape=jax.ShapeDtypeStruct((rows, n_out), F32),
        compiler_params=_params(1),
        name="modulation",
    )(cond, w_mod, b_mod.reshape(1, n_out))


def _reorder_w_in_kernel(w_ref, o_ref):
    ab0 = QKV_W
    z0 = ab0 + 4 * DN_HEADS
    glu0 = z0 + DN_W
    rows = w_ref.shape[0]
    o_ref[:, IN_QKV[0]:IN_QKV[1]] = w_ref[:, 0:QKV_W].astype(BF16)
    o_ref[:, IN_Z[0]:IN_Z[1]] = w_ref[:, z0:z0 + DN_W].astype(BF16)
    o_ref[:, IN_GLU[0]:IN_GLU[1]] = w_ref[:, glu0:glu0 + 2 * CV_W].astype(BF16)
    ab = jnp.concatenate([w_ref[:, ab0:z0], jnp.zeros((rows, LANES - 4 * DN_HEADS), F32)], axis=1)
    o_ref[:, IN_AB[0]:IN_AB[1]] = ab.astype(BF16)


def _reorder_w_in(w_in):
    _, rows, cols = w_in.shape
    tr = SUB_TILE
    return pl.pallas_call(
        _reorder_w_in_kernel,
        grid=(rows // tr,),
        in_specs=[pl.BlockSpec((None, tr, cols), lambda i: (0, i, 0))],
        out_specs=pl.BlockSpec((tr, IN_COLS_PADDED), lambda i: (i, 0)),
        out_shape=jax.ShapeDtypeStruct((rows, IN_COLS_PADDED), BF16),
        compiler_params=_params(1),
        name="reorder_w_in",
    )(w_in)


def _interleave(primary, secondary):
    more_a = more_b = True
    while more_a or more_b:
        if more_a:
            more_a = next(primary, StopIteration) is not StopIteration
        if more_b:
            more_b = secondary is not None and next(secondary, StopIteration) is not StopIteration


def _ffn1_proj_kernel(x_ref, mod_ref, g_ref, wgu_ref, wd_ref, win_ref, cw_ref, alog_ref, dtb_ref,
                      ccw_ref, ccb_ref, clg_ref, clb_ref,
                      x1_ref, act_ref, z_ref, cv_ref, gates_ref, xp_ref, cacc_ref, *, row_len):
    m = mod_ref[...]
    g = g_ref[...]
    ts = SUB_TILE
    n_rows = ts // row_len
    pitch = row_len + 2 * CONV_HALO
    total = n_rows * pitch
    grp = CONV_ROWS
    lead = CONV_HALO - CONV_K // 2
    rowi = lax.broadcasted_iota(jnp.int32, (ts, LANES), 0)
    lanei = lax.broadcasted_iota(jnp.int32, (ts, LANES), 1)
    pos_row = jnp.bitwise_and(rowi, row_len - 1)
    keep_prev = pos_row != 0
    keep_next = pos_row != row_len - 1
    pos_chunk = jnp.bitwise_and(rowi, CHUNK - 1)

    def matmul_part(rows, out):
        x = x_ref[rows, :]
        hb = (_rms(x, g[0:1]) * (1.0 + m[1:2]) + m[0:1]).astype(BF16)
        acc = None
        for s, n in FF_CHUNKS:
            gt = _dot(hb, wgu_ref[:, s:s + n])
            up = _dot(hb, wgu_ref[:, D_FF + s:D_FF + s + n])
            a = (_silu(gt) * up).astype(BF16)
            p = _dot(a, wd_ref[s:s + n, :])
            acc = p if acc is None else acc + p
            yield
        x1 = x + 0.5 * (m[2:3] * _rms(acc, g[1:2]))
        x1_ref[rows, :] = x1
        h1 = (_rms(x1, g[2:3]) * (1.0 + m[4:5]) + m[3:4]).astype(BF16)
        out["qkv"] = _dot(h1, win_ref[:, IN_QKV[0]:IN_QKV[1]])
        out["ab"] = _dot(h1, win_ref[:, IN_AB[0]:IN_AB[1]])
        z_ref[rows, :] = _dot(h1, win_ref[:, IN_Z[0]:IN_Z[1]])
        out["glu"] = _dot(h1, win_ref[:, IN_GLU[0]:IN_GLU[1]])

    def vector_part(rows, inp):
        qkv, ab, glu = inp["qkv"], inp["ab"], inp["glu"]
        for s in range(QKV_W // LANES):
            ls = slice(s * LANES, (s + 1) * LANES)
            xs = qkv[:, ls]
            w = cw_ref[:, ls]
            xp = jnp.where(keep_prev, pltpu.roll(xs, 1, 0), 0.0)
            xn = jnp.where(keep_next, pltpu.roll(xs, ts - 1, 0), 0.0)
            y = _silu(w[0:1] * xp + w[1:2] * xs + w[2:3] * xn)
            if s < 2 * DN_HEADS:
                y = y * lax.rsqrt(jnp.sum(y * y, axis=-1, keepdims=True) + EPS)
            if s < DN_HEADS:
                y = y * (DN_DK ** -0.5)
            act_ref[rows, ls] = y
            if s % 4 == 3:
                yield
        t = ab + dtb_ref[...]
        softplus = jnp.maximum(t, 0.0) + jnp.log1p(jnp.exp(-jnp.abs(t)))
        gate = -jnp.exp(alog_ref[...]) * softplus
        pre = gate
        suf = gate
        for sh in (1, 2, 4, 8, 16, 32):
            pre = pre + jnp.where(pos_chunk >= sh, pltpu.roll(pre, sh, 0), 0.0)
            suf = suf + jnp.where(pos_chunk < CHUNK - sh, pltpu.roll(suf, ts - sh, 0), 0.0)
        gc = jnp.where(lanei < DN_HEADS, pre, suf)
        tot = pltpu.roll(pre + suf - gate, GATE_TOT, 1)
        gates_ref[rows, :] = jnp.where(lanei < GATE_BETA, gc,
                                       jnp.where(lanei < GATE_TOT, jax.nn.sigmoid(ab), tot))
        yield
        zeros = jnp.zeros((CONV_HALO, LANES), F32)
        for lg in range(CV_W // LANES):
            x = glu[:, lg * LANES:(lg + 1) * LANES] * jax.nn.sigmoid(
                glu[:, CV_W + lg * LANES:CV_W + (lg + 1) * LANES])
            for r in range(n_rows):
                base = r * pitch
                xp_ref[0, lg, base:base + CONV_HALO, :] = zeros
                xp_ref[0, lg, base + CONV_HALO:base + CONV_HALO + row_len, :] = (
                    x[r * row_len:(r + 1) * row_len, :])
                xp_ref[0, lg, base + CONV_HALO + row_len:base + pitch, :] = zeros
            for b in range(1, SUBLANES):
                xp_ref[b, lg, 0:total - SUBLANES, :] = xp_ref[0, lg, b:b + total - SUBLANES, :]
            yield
        for i in range(ts // grp):
            base = i * grp + (i * grp // row_len) * (2 * CONV_HALO)
            for lg in range(CV_W // LANES):
                ls = slice(lg * LANES, (lg + 1) * LANES)
                acc = jnp.zeros((grp, LANES), F32) + ccb_ref[:, ls]
                for j in range(CONV_K):
                    shift = (lead + j) % SUBLANES
                    start = base + lead + j - shift
                    acc = acc + ccw_ref[j:j + 1, ls] * xp_ref[shift, lg, start:start + grp, :]
                cacc_ref[i * grp:(i + 1) * grp, ls] = acc
            yield
        c = cacc_ref[...]
        mu = jnp.mean(c, axis=-1, keepdims=True)
        var = jnp.mean(jnp.square(c - mu), axis=-1, keepdims=True)
        cn = (c - mu) * lax.rsqrt(var + EPS) * clg_ref[...] + clb_ref[...]
        cv_ref[rows, :] = _silu(cn).astype(BF16)

    n_sub = x_ref.shape[0] // ts
    carried = [dict() for _ in range(n_sub)]
    pending = None
    for s in range(n_sub):
        rows = pl.ds(s * ts, ts)
        _interleave(matmul_part(rows, carried[s]), pending)
        pending = vector_part(rows, carried[s])
    _interleave(pending, None)


def _ffn1_proj(x, mod, norm_g, wgu, wd, win, conv_w, alog_row, dtb_row, cv_w, cv_b, cv_ln_g, cv_ln_b,
               tiles_per_mod, row_len):
    n = x.shape[0]
    tm = FFN_TILE
    total = (SUB_TILE // row_len) * (row_len + 2 * CONV_HALO)
    row = lambda w: pl.BlockSpec((tm, w), lambda i: (i, 0))
    consts = (norm_g, wgu, wd, win, conv_w, alog_row, dtb_row, cv_w, cv_b, cv_ln_g, cv_ln_b)
    return pl.pallas_call(
        functools.partial(_ffn1_proj_kernel, row_len=row_len),
        grid=(n // tm,),
        in_specs=[row(D_MODEL),
                  pl.BlockSpec((None, N_MOD, D_MODEL), lambda i: (i // tiles_per_mod, 0, 0))]
                 + [_const_spec(c.shape) for c in consts],
        out_specs=[row(D_MODEL), row(QKV_W), row(DN_W), row(CV_W), row(LANES)],
        out_shape=[jax.ShapeDtypeStruct((n, D_MODEL), F32), jax.ShapeDtypeStruct((n, QKV_W), F32),
                   jax.ShapeDtypeStruct((n, DN_W), F32), jax.ShapeDtypeStruct((n, CV_W), BF16),
                   jax.ShapeDtypeStruct((n, LANES), F32)],
        scratch_shapes=[pltpu.VMEM((SUBLANES, CV_W // LANES, total, LANES), F32),
                        pltpu.VMEM((SUB_TILE, CV_W), F32)],
        compiler_params=_params(1),
        name="ffn1_proj",
    )(x, mod, *consts)


def _out_ffn2_kernel(x1_ref, o_ref, z_ref, cv_ref, mod_ref, g_ref, dng_ref, wout_ref, wgu_ref, wd_ref, y_ref):
    m = mod_ref[...]
    g = g_ref[...]
    dng = dng_ref[...]
    ts = SUB_TILE
    n_sub = x1_ref.shape[0] // ts

    def vector_part(s, out):
        rows = pl.ds(s * ts, ts)
        heads = []
        for h in range(DN_HEADS):
            hs = slice(h * DN_DK, (h + 1) * DN_DK)
            heads.append((_rms(o_ref[rows, hs], dng) * _silu(z_ref[rows, hs])).astype(BF16))
        out["og"] = jnp.concatenate(heads, axis=-1)
        yield

    def matmul_part(s, inp):
        rows = pl.ds(s * ts, ts)
        y = _dot(inp["og"], wout_ref[0:DN_W, :]) + _dot(cv_ref[rows, :], wout_ref[DN_W:, :])
        x2 = x1_ref[rows, :] + m[5:6] * _rms(y, g[3:4])
        hb = (_rms(x2, g[4:5]) * (1.0 + m[7:8]) + m[6:7]).astype(BF16)
        yield
        acc = None
        for c0, n in FF_CHUNKS:
            gt = _dot(hb, wgu_ref[:, c0:c0 + n])
            yield
            up = _dot(hb, wgu_ref[:, D_FF + c0:D_FF + c0 + n])
            a = (_silu(gt) * up).astype(BF16)
            yield
            p = _dot(a, wd_ref[c0:c0 + n, :])
            acc = p if acc is None else acc + p
            yield
        y_ref[rows, :] = x2 + 0.5 * (m[8:9] * _rms(acc, g[5:6]))

    carried = [dict() for _ in range(n_sub)]
    _interleave(vector_part(0, carried[0]), None)
    for s in range(n_sub):
        nxt = vector_part(s + 1, carried[s + 1]) if s + 1 < n_sub else None
        _interleave(matmul_part(s, carried[s]), nxt)


def _out_ffn2(x1, o, z, cv, mod, norm_g, dn_norm_g, wout, wgu, wd, tiles_per_mod):
    n = x1.shape[0]
    tm = FFN_TILE
    row = lambda w: pl.BlockSpec((tm, w), lambda i: (i, 0))
    consts = (norm_g, dn_norm_g, wout, wgu, wd)
    return pl.pallas_call(
        _out_ffn2_kernel,
        grid=(n // tm,),
        in_specs=[row(D_MODEL), row(DN_W), row(DN_W), row(CV_W),
                  pl.BlockSpec((None, N_MOD, D_MODEL), lambda i: (i // tiles_per_mod, 0, 0))]
                 + [_const_spec(c.shape) for c in consts],
        out_specs=row(D_MODEL),
        out_shape=jax.ShapeDtypeStruct((n, D_MODEL), F32),
        compiler_params=_params(1),
        name="out_ffn2",
    )(x1, o, z, cv, mod, *consts)


def _deltanet_kernel(*refs, seq_len, n_seq, zero_init):
    if zero_init:
        act_ref, gates_ref, o_ref, sfin_ref, st_ref = refs
        s0_ref = None
    else:
        act_ref, gates_ref, s0_ref, o_ref, sfin_ref, st_ref = refs
    n_blk = seq_len // BLOCK_T
    bt = BLOCK_T
    f32_inf = jnp.float32(jnp.inf)

    n_state = 2 * DN_HEADS
    if zero_init:
        st_ref[...] = jnp.zeros((n_seq, n_state, DN_DK, DN_DK), F32)
    else:
        st_ref[...] = s0_ref[...]
    if n_blk > 1:
        o_ref[...] = jnp.zeros((n_seq, seq_len, DN_W), F32)

    state_is_zero = zero_init and n_blk == 1
    pt = PAIR_T
    n_pair = bt // pt
    n_chunk = bt // CHUNK
    ri = lax.broadcasted_iota(jnp.int32, (pt, pt), 0)
    ci = lax.broadcasted_iota(jnp.int32, (pt, pt), 1)
    same = jnp.right_shift(ri, 6) == jnp.right_shift(ci, 6)
    incl = (same & (ri >= ci), same & (ri <= ci))
    strict = (same & (ri > ci), same & (ri < ci))
    first_chunk_lanes = ci < CHUNK

    def delta_blocks(jobs):
        gram = {}
        probs = {}
        for d, g, r0, key in jobs:
            rows = pl.ds(r0, bt)
            gates = gates_ref[g, rows, :]
            gates_t = jnp.transpose(gates)
            for h in range(DN_HEADS):
                c8 = d * DN_HEADS + h
                for p in range(n_pair):
                    ps = slice(p * pt, (p + 1) * pt)
                    prow = pl.ds(r0 + p * pt, pt)
                    qh = act_ref[g, prow, h * DN_DK:(h + 1) * DN_DK]
                    kh = act_ref[g, prow, DN_W + h * DN_DK:DN_W + (h + 1) * DN_DK]
                    vh = act_ref[g, prow, 2 * DN_W + h * DN_DK:2 * DN_W + (h + 1) * DN_DK]
                    if (key, h, p) not in gram:
                        khb = kh.astype(BF16)
                        gram[(key, h, p)] = (_dot_nt(jnp.concatenate([khb, qh.astype(BF16)], axis=0), khb),
                                             jnp.transpose(kh))
                    gr, kh_t = gram[(key, h, p)]
                    kk = gr[0:pt]
                    qk = gr[pt:2 * pt]
                    g_i = gates[ps, c8:c8 + 1]
                    g_j = gates_t[c8:c8 + 1, ps]
                    dec = jnp.exp(jnp.where(incl[d], g_i - g_j, -f32_inf))
                    b_i = gates[ps, GATE_BETA + c8:GATE_BETA + c8 + 1]
                    eg = jnp.exp(g_i)
                    kd_t = kh_t * jnp.exp(gates_t[GATE_TOT + c8:GATE_TOT + c8 + 1, ps] - g_j)
                    probs[(d, g, h, p)] = dict(
                        p=-(jnp.where(strict[d], kk * dec, 0.0) * b_i), qkm=(qk * dec).astype(BF16),
                        x=jnp.concatenate([vh * b_i, kh * (b_i * eg)], axis=-1), qd=qh * eg,
                        kd_t=[jnp.where(first_chunk_lanes, kd_t, 0.0).astype(BF16),
                              jnp.where(first_chunk_lanes, 0.0, kd_t).astype(BF16)],
                        tot=gates[ps, GATE_TOT + c8:GATE_TOT + c8 + 1])
        for pr in probs.values():
            pb = pr["p"].astype(BF16)
            pr["q"] = pr["p"]
            pr["p"] = _dot(pb, pb)
        for _ in range(4):
            for pr in probs.values():
                pb = pr["p"].astype(BF16)
                r = _dot(pb, jnp.concatenate([pb, pr["q"].astype(BF16)], axis=-1))
                pr["q"] = pr["q"] + pr["p"] + r[:, pt:2 * pt]
                pr["p"] = r[:, 0:pt]
        for pr in probs.values():
            r = _dot(pr["p"].astype(BF16), pr["q"].astype(BF16))
            qm = pr["q"] + pr["p"] + r
            pr["x"] = pr["x"] + _dot(qm.astype(BF16), pr["x"].astype(BF16))
        for pr in probs.values():
            xb = pr["x"].astype(BF16)
            qx = _dot(pr["qkm"], xb)
            pr["o0"] = qx[:, 0:DN_DK]
            pr["e"] = pr["qd"] - qx[:, DN_DK:2 * DN_DK]
            pr["kx"] = [_dot(kd_t, xb) for kd_t in pr["kd_t"]]
        chains = [(d, g, h) for d, g, _, _ in jobs for h in range(DN_HEADS)]
        state = {(d, g, h): (None if state_is_zero else st_ref[g, d * DN_HEADS + h]) for d, g, h in chains}
        out_c = {}
        for step in range(n_chunk):
            res = {}
            for ch in chains:
                if state[ch] is None:
                    continue
                c = step if ch[0] == 0 else n_chunk - 1 - step
                pr = probs[ch + (c // 2,)]
                cs = slice((c % 2) * CHUNK, (c % 2 + 1) * CHUNK)
                lhs = jnp.concatenate([-pr["kx"][c % 2][:, DN_DK:2 * DN_DK], pr["e"][cs]], axis=0)
                res[ch] = _dot(lhs.astype(BF16), state[ch].astype(BF16))
            for ch in chains:
                c = step if ch[0] == 0 else n_chunk - 1 - step
                pr = probs[ch + (c // 2,)]
                cs = slice((c % 2) * CHUNK, (c % 2 + 1) * CHUNK)
                b_c = pr["kx"][c % 2][:, 0:DN_DK]
                if state[ch] is None:
                    out_c[ch + (c,)] = pr["o0"][cs]
                    state[ch] = b_c
                else:
                    g_last = jnp.exp(pr["tot"][(c % 2) * CHUNK:(c % 2) * CHUNK + 1])
                    out_c[ch + (c,)] = res[ch][DN_DK:DN_DK + CHUNK] + pr["o0"][cs]
                    state[ch] = state[ch] * g_last + res[ch][0:DN_DK] + b_c
        outs = {}
        for ch in chains:
            d, g, h = ch
            st_ref[g, d * DN_HEADS + h] = state[ch]
            outs[ch] = jnp.concatenate([out_c[ch + (c,)] for c in range(n_chunk)], axis=0)
        return outs

    if n_blk == 1:
        outs = delta_blocks([(d, g, 0, g) for g in range(n_seq) for d in range(2)])
        for g in range(n_seq):
            for h in range(DN_HEADS):
                o_ref[g, :, h * DN_DK:(h + 1) * DN_DK] = outs[(0, g, h)] + outs[(1, g, h)]
    else:
        def scan_block(b, carry):
            r0 = (pl.multiple_of(b * bt, bt), pl.multiple_of((n_blk - 1 - b) * bt, bt))
            outs = delta_blocks([(d, g, r0[d], (g, d)) for g in range(n_seq) for d in range(2)])
            for (d, g, h), o in outs.items():
                o_ref[g, pl.ds(r0[d], bt), h * DN_DK:(h + 1) * DN_DK] += o
            return carry

        lax.fori_loop(0, n_blk, scan_block, 0)

    sfin_ref[...] = st_ref[...]


def _deltanet(act, gates, s0, n_seq):
    nb, seq_len, _ = act.shape
    n_state = 2 * DN_HEADS
    zero_init = s0 is None
    seq = lambda w: pl.BlockSpec((n_seq, seq_len, w), lambda b: (b, 0, 0))
    state = pl.BlockSpec((n_seq, n_state, DN_DK, DN_DK), lambda b: (b, 0, 0, 0))
    operands = (act, gates) + (() if zero_init else (s0,))
    return pl.pallas_call(
        functools.partial(_deltanet_kernel, seq_len=seq_len, n_seq=n_seq, zero_init=zero_init),
        grid=(nb // n_seq,),
        in_specs=[seq(QKV_W), seq(LANES)] + ([] if zero_init else [state]),
        out_specs=[seq(DN_W), state],
        out_shape=[jax.ShapeDtypeStruct((nb, seq_len, DN_W), F32),
                   jax.ShapeDtypeStruct((nb, n_state, DN_DK, DN_DK), F32)],
        scratch_shapes=[pltpu.VMEM((n_seq, n_state, DN_DK, DN_DK), F32)],
        compiler_params=_params(1),
        name="deltanet",
    )(*operands)


def _layer(x, mod, s0, row_len, n_seq, p):
    nb, seq_len, _ = x.shape
    n = nb * seq_len
    tiles_per_mod = (n // mod.shape[0]) // FFN_TILE
    x1, act, z, cv, gates = _ffn1_proj(x.reshape(n, D_MODEL), mod, p["norm_g"], p["wgu1"], p["wd1"],
                                       p["win"], p["dn_conv_w"], p["alog_row"], p["dtb_row"],
                                       p["cv_dw_w"], p["cv_dw_b"], p["cv_ln_g"], p["cv_ln_b"],
                                       tiles_per_mod, row_len)
    o, s_fin = _deltanet(act.reshape(nb, seq_len, QKV_W), gates.reshape(nb, seq_len, LANES), s0, n_seq)
    y = _out_ffn2(x1, o.reshape(n, DN_W), z, cv, mod, p["norm_g"], p["dn_norm_g"], p["wout"],
                  p["wgu2"], p["wd2"], tiles_per_mod)
    return y.reshape(nb, seq_len, D_MODEL), s_fin


def kernel(x_prompt, x_sample, state_delta, c, c_ctx, w_mod, b_mod, norm_g, ffn1_w_in, ffn1_w_out, w_in,
           dn_conv_w, dn_a_log, dn_dt_bias, dn_norm_g, cv_dw_w, cv_dw_b, cv_ln_g, cv_ln_b, w_out,
           ffn2_w_in, ffn2_w_out):
    depth = w_mod.shape[0]
    assert depth == 1, "one trunk layer"
    batch, seq_len, _ = x_prompt.shape
    dec_batch, dec_seq, _ = x_sample.shape
    n_state = 2 * DN_HEADS

    assert 1 + dec_batch <= SUBLANES and dn_conv_w.shape[1] == SHORT_CONV and cv_dw_w.shape[1] == CONV_K
    cond = jnp.zeros((SUBLANES, D_MODEL), F32).at[0].set(c_ctx).at[1:1 + dec_batch].set(c)
    mod = _modulation(cond, w_mod[0], b_mod[0]).reshape(SUBLANES, N_MOD, D_MODEL)
    mod_ctx, mod_lat = mod[0:1], mod[1:1 + dec_batch]

    win = _reorder_w_in(w_in)
    pad8 = lambda v: jnp.zeros((1, LANES), F32).at[0, 0:n_state].set(v.reshape(n_state))
    p = dict(norm_g=norm_g[0], wgu1=ffn1_w_in[0].astype(BF16), wd1=ffn1_w_out[0].astype(BF16), win=win,
             dn_conv_w=dn_conv_w[0], alog_row=pad8(dn_a_log[0]), dtb_row=pad8(dn_dt_bias[0]),
             dn_norm_g=dn_norm_g[0].reshape(1, DN_DK), cv_dw_w=cv_dw_w[0],
             cv_dw_b=cv_dw_b[0].reshape(1, CV_W), cv_ln_g=cv_ln_g[0].reshape(1, CV_W),
             cv_ln_b=cv_ln_b[0].reshape(1, CV_W), wout=w_out[0].astype(BF16),
             wgu2=ffn2_w_in[0].astype(BF16), wd2=ffn2_w_out[0].astype(BF16))

    y_p, s_ctx = _layer(x_prompt, mod_ctx, None, seq_len, CTX_SEQS_PER_STEP, p)
    s_lat = state_delta[:, 0].reshape(dec_batch, n_state, DN_DK, DN_DK)
    y_s, _ = _layer(x_sample, mod_lat, s_lat, GRID_W, 1, p)
    new_state = s_ctx.reshape(batch, 1, 2, DN_HEADS, DN_DK, DN_DK).astype(x_prompt.dtype)
    return (y_p, y_s, new_state)
```

```python
import functools

import jax
import jax.numpy as jnp
from jax import lax
from jax.experimental import pallas as pl
from jax.experimental.pallas import tpu as pltpu

F32 = jnp.float32
BF16 = jnp.bfloat16

D_MODEL = 1024
D_FF = 2816
N_MOD = 9
GRID_W = 64
DN_W = 512
CV_W = 512
DN_HEADS = 4
DN_DK = 128
CHUNK = 64
SHORT_CONV = 3
CONV_K = 31
EPS = 1e-6

LANES = 128
SUBLANES = 8
BLOCK_T = 256
PAIR_T = 2 * CHUNK
GATE_BETA = 2 * DN_HEADS
GATE_TOT = 4 * DN_HEADS
BF16_ROWS = 16
CAST_PIECES = 4
CTX_SEQS_PER_STEP = 1
MOD_STEPS = 4
FFN_TILE = 512
SUB_TILE = 256
CONV_HALO = 16
CONV_ROWS = 32
VMEM_LIMIT_BYTES = 56 * 1024 * 1024
FF_CHUNKS = ((0, 768), (768, 768), (1536, 768), (2304, 512))
QKV_W = 3 * DN_W
IN_QKV = (0, QKV_W)
IN_Z = (QKV_W, QKV_W + DN_W)
IN_GLU = (QKV_W + DN_W, QKV_W + DN_W + 2 * CV_W)
IN_AB = (QKV_W + DN_W + 2 * CV_W, QKV_W + DN_W + 2 * CV_W + LANES)
IN_COLS_PADDED = IN_AB[1]


def _dot(a, b):
    return jnp.dot(a, b, preferred_element_type=F32)


def _dot_nt(a, b):
    return lax.dot_general(a, b, (((1,), (1,)), ((), ())), preferred_element_type=F32)


def _silu(x):
    return x * jax.nn.sigmoid(x)


def _rms(x, g):
    return x * lax.rsqrt(jnp.mean(x * x, axis=-1, keepdims=True) + EPS) * g


def _const_spec(shape):
    nd = len(shape)
    return pl.BlockSpec(shape, lambda *_: (0,) * nd, pipeline_mode=pl.Buffered(1))


def _params(n_grid_dims):
    return pltpu.CompilerParams(dimension_semantics=("arbitrary",) * n_grid_dims,
                                vmem_limit_bytes=VMEM_LIMIT_BYTES)


def _mod_kernel(c_ref, w_ref, b_ref, o_ref):
    s = _silu(c_ref[...]).astype(BF16)
    o_ref[...] = _dot(s, w_ref[...].astype(BF16)) + b_ref[...]


def _modulation(cond, w_mod, b_mod):
    rows = cond.shape[0]
    n_out = w_mod.shape[1]
    tn = n_out // MOD_STEPS
    return pl.pallas_call(
        _mod_kernel,
        grid=(MOD_STEPS,),
        in_specs=[pl.BlockSpec((rows, D_MODEL), lambda j: (0, 0)),
                  pl.BlockSpec((D_MODEL, tn), lambda j: (0, j)),
                  pl.BlockSpec((1, tn), lambda j: (0, j))],
        out_specs=pl.BlockSpec((rows, tn), lambda j: (0, j)),
        out_shape=jax.ShapeDtypeStruct((rows, n_out), F32),
        compiler_params=_params(1),
        name="modulation",
    )(cond, w_mod, b_mod.reshape(1, n_out))


def _reorder_w_in_kernel(w_ref, o_ref):
    ab0 = QKV_W
    z0 = ab0 + 4 * DN_HEADS
    glu0 = z0 + DN_W
    rows = w_ref.shape[0]
    o_ref[:, IN_QKV[0]:IN_QKV[1]] = w_ref[:, 0:QKV_W].astype(BF16)
    o_ref[:, IN_Z[0]:IN_Z[1]] = w_ref[:, z0:z0 + DN_W].astype(BF16)
    o_ref[:, IN_GLU[0]:IN_GLU[1]] = w_ref[:, glu0:glu0 + 2 * CV_W].astype(BF16)
    ab = jnp.concatenate([w_ref[:, ab0:z0], jnp.zeros((rows, LANES - 4 * DN_HEADS), F32)], axis=1)
    o_ref[:, IN_AB[0]:IN_AB[1]] = ab.astype(BF16)


def _reorder_w_in(w_in):
    _, rows, cols = w_in.shape
    tr = SUB_TILE
    return pl.pallas_call(
        _reorder_w_in_kernel,
        grid=(rows // tr,),
        in_specs=[pl.BlockSpec((None, tr, cols), lambda i: (0, i, 0))],
        out_specs=pl.BlockSpec((tr, IN_COLS_PADDED), lambda i: (i, 0)),
        out_shape=jax.ShapeDtypeStruct((rows, IN_COLS_PADDED), BF16),
        compiler_params=_params(1),
        name="reorder_w_in",
    )(w_in)


def _interleave(primary, secondary):
    more_a = more_b = True
    while more_a or more_b:
        if more_a:
            more_a = next(primary, StopIteration) is not StopIteration
        if more_b:
            more_b = secondary is not None and next(secondary, StopIteration) is not StopIteration


def _ffn1_proj_kernel(x_ref, mod_ref, g_ref, wgu_ref, wd_ref, win_ref, cw_ref, alog_ref, dtb_ref,
                      ccw_ref, ccb_ref, clg_ref, clb_ref,
                      x1_ref, act_ref, z_ref, cv_ref, gates_ref, xp_ref, cacc_ref, *, row_len):
    m = mod_ref[...]
    g = g_ref[...]
    ts = SUB_TILE
    n_rows = ts // row_len
    pitch = row_len + 2 * CONV_HALO
    total = n_rows * pitch
    grp = CONV_ROWS
    lead = CONV_HALO - CONV_K // 2
    rowi = lax.broadcasted_iota(jnp.int32, (ts, LANES), 0)
    lanei = lax.broadcasted_iota(jnp.int32, (ts, LANES), 1)
    pos_row = jnp.bitwise_and(rowi, row_len - 1)
    keep_prev = pos_row != 0
    keep_next = pos_row != row_len - 1
    pos_chunk = jnp.bitwise_and(rowi, CHUNK - 1)

    def matmul_part(rows, out):
        x = x_ref[rows, :]
        hb = (_rms(x, g[0:1]) * (1.0 + m[1:2]) + m[0:1]).astype(BF16)
        acc = None
        for s, n in FF_CHUNKS:
            gt = _dot(hb, wgu_ref[:, s:s + n])
            up = _dot(hb, wgu_ref[:, D_FF + s:D_FF + s + n])
            a = (_silu(gt) * up).astype(BF16)
            p = _dot(a, wd_ref[s:s + n, :])
            acc = p if acc is None else acc + p
            yield
        x1 = x + 0.5 * (m[2:3] * _rms(acc, g[1:2]))
        x1_ref[rows, :] = x1
        h1 = (_rms(x1, g[2:3]) * (1.0 + m[4:5]) + m[3:4]).astype(BF16)
        out["qkv"] = _dot(h1, win_ref[:, IN_QKV[0]:IN_QKV[1]])
        out["ab"] = _dot(h1, win_ref[:, IN_AB[0]:IN_AB[1]])
        z_ref[rows, :] = _dot(h1, win_ref[:, IN_Z[0]:IN_Z[1]])
        out["glu"] = _dot(h1, win_ref[:, IN_GLU[0]:IN_GLU[1]])

    def vector_part(rows, inp):
        qkv, ab, glu = inp["qkv"], inp["ab"], inp["glu"]
        for s in range(QKV_W // LANES):
            ls = slice(s * LANES, (s + 1) * LANES)
            xs = qkv[:, ls]
            w = cw_ref[:, ls]
            xp = jnp.where(keep_prev, pltpu.roll(xs, 1, 0), 0.0)
            xn = jnp.where(keep_next, pltpu.roll(xs, ts - 1, 0), 0.0)
            y = _silu(w[0:1] * xp + w[1:2] * xs + w[2:3] * xn)
            if s < 2 * DN_HEADS:
                y = y * lax.rsqrt(jnp.sum(y * y, axis=-1, keepdims=True) + EPS)
            if s < DN_HEADS:
                y = y * (DN_DK ** -0.5)
            act_ref[rows, ls] = y
            if s % 4 == 3:
                yield
        t = ab + dtb_ref[...]
        softplus = jnp.maximum(t, 0.0) + jnp.log1p(jnp.exp(-jnp.abs(t)))
        gate = -jnp.exp(alog_ref[...]) * softplus
        pre = gate
        suf = gate
        for sh in (1, 2, 4, 8, 16, 32):
            pre = pre + jnp.where(pos_chunk >= sh, pltpu.roll(pre, sh, 0), 0.0)
            suf = suf + jnp.where(pos_chunk < CHUNK - sh, pltpu.roll(suf, ts - sh, 0), 0.0)
        gc = jnp.where(lanei < DN_HEADS, pre, suf)
        tot = pltpu.roll(pre + suf - gate, GATE_TOT, 1)
        gates_ref[rows, :] = jnp.where(lanei < GATE_BETA, gc,
                                       jnp.where(lanei < GATE_TOT, jax.nn.sigmoid(ab), tot))
        yield
        zeros = jnp.zeros((CONV_HALO, LANES), F32)
        for lg in range(CV_W // LANES):
            x = glu[:, lg * LANES:(lg + 1) * LANES] * jax.nn.sigmoid(
                glu[:, CV_W + lg * LANES:CV_W + (lg + 1) * LANES])
            for r in range(n_rows):
                base = r * pitch
                xp_ref[0, lg, base:base + CONV_HALO, :] = zeros
                xp_ref[0, lg, base + CONV_HALO:base + CONV_HALO + row_len, :] = (
                    x[r * row_len:(r + 1) * row_len, :])
                xp_ref[0, lg, base + CONV_HALO + row_len:base + pitch, :] = zeros
            for b in range(1, SUBLANES):
                xp_ref[b, lg, 0:total - SUBLANES, :] = xp_ref[0, lg, b:b + total - SUBLANES, :]
            yield
        for i in range(ts // grp):
            base = i * grp + (i * grp // row_len) * (2 * CONV_HALO)
            for lg in range(CV_W // LANES):
                ls = slice(lg * LANES, (lg + 1) * LANES)
                acc = jnp.zeros((grp, LANES), F32) + ccb_ref[:, ls]
                for j in range(CONV_K):
                    shift = (lead + j) % SUBLANES
                    start = base + lead + j - shift
                    acc = acc + ccw_ref[j:j + 1, ls] * xp_ref[shift, lg, start:start + grp, :]
                cacc_ref[i * grp:(i + 1) * grp, ls] = acc
            yield
        c = cacc_ref[...]
        mu = jnp.mean(c, axis=-1, keepdims=True)
        var = jnp.mean(jnp.square(c - mu), axis=-1, keepdims=True)
        cn = (c - mu) * lax.rsqrt(var + EPS) * clg_ref[...] + clb_ref[...]
        cv_ref[rows, :] = _silu(cn).astype(BF16)

    n_sub = x_ref.shape[0] // ts
    carried = [dict() for _ in range(n_sub)]
    pending = None
    for s in range(n_sub):
        rows = pl.ds(s * ts, ts)
        _interleave(matmul_part(rows, carried[s]), pending)
        pending = vector_part(rows, carried[s])
    _interleave(pending, None)


def _ffn1_proj(x, mod, norm_g, wgu, wd, win, conv_w, alog_row, dtb_row, cv_w, cv_b, cv_ln_g, cv_ln_b,
               tiles_per_mod, row_len):
    n = x.shape[0]
    tm = FFN_TILE
    total = (SUB_TILE // row_len) * (row_len + 2 * CONV_HALO)
    row = lambda w: pl.BlockSpec((tm, w), lambda i: (i, 0))
    consts = (norm_g, wgu, wd, win, conv_w, alog_row, dtb_row, cv_w, cv_b, cv_ln_g, cv_ln_b)
    return pl.pallas_call(
        functools.partial(_ffn1_proj_kernel, row_len=row_len),
        grid=(n // tm,),
        in_specs=[row(D_MODEL),
                  pl.BlockSpec((None, N_MOD, D_MODEL), lambda i: (i // tiles_per_mod, 0, 0))]
                 + [_const_spec(c.shape) for c in consts],
        out_specs=[row(D_MODEL), row(QKV_W), row(DN_W), row(CV_W), row(LANES)],
        out_shape=[jax.ShapeDtypeStruct((n, D_MODEL), F32), jax.ShapeDtypeStruct((n, QKV_W), F32),
                   jax.ShapeDtypeStruct((n, DN_W), F32), jax.ShapeDtypeStruct((n, CV_W), BF16),
                   jax.ShapeDtypeStruct((n, LANES), F32)],
        scratch_shapes=[pltpu.VMEM((SUBLANES, CV_W // LANES, total, LANES), F32),
                        pltpu.VMEM((SUB_TILE, CV_W), F32)],
        compiler_params=_params(1),
        name="ffn1_proj",
    )(x, mod, *consts)


def _out_ffn2_kernel(x1_ref, o_ref, z_ref, cv_ref, mod_ref, g_ref, dng_ref, wout_ref, wgu_ref, wd_ref, y_ref):
    m = mod_ref[...]
    g = g_ref[...]
    dng = dng_ref[...]
    ts = SUB_TILE
    n_sub = x1_ref.shape[0] // ts

    def vector_part(s, out):
        rows = pl.ds(s * ts, ts)
        heads = []
        for h in range(DN_HEADS):
            hs = slice(h * DN_DK, (h + 1) * DN_DK)
            heads.append((_rms(o_ref[rows, hs], dng) * _silu(z_ref[rows, hs])).astype(BF16))
        out["og"] = jnp.concatenate(heads, axis=-1)
        yield

    def matmul_part(s, inp):
        rows = pl.ds(s * ts, ts)
        y = _dot(inp["og"], wout_ref[0:DN_W, :]) + _dot(cv_ref[rows, :], wout_ref[DN_W:, :])
        x2 = x1_ref[rows, :] + m[5:6] * _rms(y, g[3:4])
        hb = (_rms(x2, g[4:5]) * (1.0 + m[7:8]) + m[6:7]).astype(BF16)
        yield
        acc = None
        for c0, n in FF_CHUNKS:
            gt = _dot(hb, wgu_ref[:, c0:c0 + n])
            yield
            up = _dot(hb, wgu_ref[:, D_FF + c0:D_FF + c0 + n])
            a = (_silu(gt) * up).astype(BF16)
            yield
            p = _dot(a, wd_ref[c0:c0 + n, :])
            acc = p if acc is None else acc + p
            yield
        y_ref[rows, :] = x2 + 0.5 * (m[8:9] * _rms(acc, g[5:6]))

    carried = [dict() for _ in range(n_sub)]
    _interleave(vector_part(0, carried[0]), None)
    for s in range(n_sub):
        nxt = vector_part(s + 1, carried[s + 1]) if s + 1 < n_sub else None
        _interleave(matmul_part(s, carried[s]), nxt)


def _out_ffn2(x1, o, z, cv, mod, norm_g, dn_norm_g, wout, wgu, wd, tiles_per_mod):
    n = x1.shape[0]
    tm = FFN_TILE
    row = lambda w: pl.BlockSpec((tm, w), lambda i: (i, 0))
    consts = (norm_g, dn_norm_g, wout, wgu, wd)
    return pl.pallas_call(
        _out_ffn2_kernel,
        grid=(n // tm,),
        in_specs=[row(D_MODEL), row(DN_W), row(DN_W), row(CV_W),
                  pl.BlockSpec((None, N_MOD, D_MODEL), lambda i: (i // tiles_per_mod, 0, 0))]
                 + [_const_spec(c.shape) for c in consts],
        out_specs=row(D_MODEL),
        out_shape=jax.ShapeDtypeStruct((n, D_MODEL), F32),
        compiler_params=_params(1),
        name="out_ffn2",
    )(x1, o, z, cv, mod, *consts)


def _deltanet_kernel(*refs, seq_len, n_seq, zero_init, n_cast):
    refs = list(refs)
    act_ref, gates_ref = refs[0:2]
    n_in = 2 if zero_init else 3
    s0_ref = None if zero_init else refs[2]
    cast_in = refs[n_in:n_in + n_cast]
    o_ref, sfin_ref = refs[n_in + n_cast:n_in + n_cast + 2]
    cast_out = refs[n_in + n_cast + 2:n_in + 2 * n_cast + 2]
    st_ref = refs[-1]
    n_blk = seq_len // BLOCK_T

    def cast_pieces():
        for src, dst in zip(cast_in, cast_out):
            step = src.shape[1] // CAST_PIECES
            for j in range(CAST_PIECES):
                dst[:, j * step:(j + 1) * step] = src[:, j * step:(j + 1) * step].astype(BF16)
                yield

    side = cast_pieces()
    bt = BLOCK_T
    f32_inf = jnp.float32(jnp.inf)

    n_state = 2 * DN_HEADS
    if zero_init:
        st_ref[...] = jnp.zeros((n_seq, n_state, DN_DK, DN_DK), F32)
    else:
        st_ref[...] = s0_ref[...]
    if n_blk > 1:
        o_ref[...] = jnp.zeros((n_seq, seq_len, DN_W), F32)

    state_is_zero = zero_init and n_blk == 1
    pt = PAIR_T
    n_pair = bt // pt
    n_chunk = bt // CHUNK
    ri = lax.broadcasted_iota(jnp.int32, (pt, pt), 0)
    ci = lax.broadcasted_iota(jnp.int32, (pt, pt), 1)
    same = jnp.right_shift(ri, 6) == jnp.right_shift(ci, 6)
    incl = (same & (ri >= ci), same & (ri <= ci))
    strict = (same & (ri > ci), same & (ri < ci))
    first_chunk_lanes = ci < CHUNK

    def delta_blocks(jobs, side_jobs=None):
        def side_piece():
            if side_jobs is not None:
                next(side_jobs, None)

        gram = {}
        probs = {}
        for d, g, r0, key in jobs:
            rows = pl.ds(r0, bt)
            gates = gates_ref[g, rows, :]
            gates_t = jnp.transpose(gates)
            for h in range(DN_HEADS):
                c8 = d * DN_HEADS + h
                for p in range(n_pair):
                    ps = slice(p * pt, (p + 1) * pt)
                    prow = pl.ds(r0 + p * pt, pt)
                    qh = act_ref[g, prow, h * DN_DK:(h + 1) * DN_DK]
                    kh = act_ref[g, prow, DN_W + h * DN_DK:DN_W + (h + 1) * DN_DK]
                    vh = act_ref[g, prow, 2 * DN_W + h * DN_DK:2 * DN_W + (h + 1) * DN_DK]
                    if (key, h, p) not in gram:
                        khb = kh.astype(BF16)
                        gram[(key, h, p)] = (_dot_nt(jnp.concatenate([khb, qh.astype(BF16)], axis=0), khb),
                                             jnp.transpose(kh))
                    gr, kh_t = gram[(key, h, p)]
                    kk = gr[0:pt]
                    qk = gr[pt:2 * pt]
                    g_i = gates[ps, c8:c8 + 1]
                    g_j = gates_t[c8:c8 + 1, ps]
                    dec = jnp.exp(jnp.where(incl[d], g_i - g_j, -f32_inf))
                    b_i = gates[ps, GATE_BETA + c8:GATE_BETA + c8 + 1]
                    eg = jnp.exp(g_i)
                    kd_t = kh_t * jnp.exp(gates_t[GATE_TOT + c8:GATE_TOT + c8 + 1, ps] - g_j)
                    probs[(d, g, h, p)] = dict(
                        p=-(jnp.where(strict[d], kk * dec, 0.0) * b_i), qkm=(qk * dec).astype(BF16),
                        x=jnp.concatenate([vh * b_i, kh * (b_i * eg)], axis=-1), qd=qh * eg,
                        kd_t=[jnp.where(first_chunk_lanes, kd_t, 0.0).astype(BF16),
                              jnp.where(first_chunk_lanes, 0.0, kd_t).astype(BF16)],
                        tot=gates[ps, GATE_TOT + c8:GATE_TOT + c8 + 1])
        side_piece()
        for pr in probs.values():
            pb = pr["p"].astype(BF16)
            pr["q"] = pr["p"]
            pr["p"] = _dot(pb, pb)
        side_piece()
        for _ in range(4):
            for pr in probs.values():
                pb = pr["p"].astype(BF16)
                r = _dot(pb, jnp.concatenate([pb, pr["q"].astype(BF16)], axis=-1))
                pr["q"] = pr["q"] + pr["p"] + r[:, pt:2 * pt]
                pr["p"] = r[:, 0:pt]
            side_piece()
        for pr in probs.values():
            r = _dot(pr["p"].astype(BF16), pr["q"].astype(BF16))
            qm = pr["q"] + pr["p"] + r
            pr["x"] = pr["x"] + _dot(qm.astype(BF16), pr["x"].astype(BF16))
        side_piece()
        for pr in probs.values():
            xb = pr["x"].astype(BF16)
            qx = _dot(pr["qkm"], xb)
            pr["o0"] = qx[:, 0:DN_DK]
            pr["e"] = pr["qd"] - qx[:, DN_DK:2 * DN_DK]
            pr["kx"] = [_dot(kd_t, xb) for kd_t in pr["kd_t"]]
        side_piece()
        chains = [(d, g, h) for d, g, _, _ in jobs for h in range(DN_HEADS)]
        state = {(d, g, h): (None if state_is_zero else st_ref[g, d * DN_HEADS + h]) for d, g, h in chains}
        out_c = {}
        for step in range(n_chunk):
            res = {}
            for ch in chains:
                if state[ch] is None:
                    continue
                c = step if ch[0] == 0 else n_chunk - 1 - step
                pr = probs[ch + (c // 2,)]
                cs = slice((c % 2) * CHUNK, (c % 2 + 1) * CHUNK)
                lhs = jnp.concatenate([-pr["kx"][c % 2][:, DN_DK:2 * DN_DK], pr["e"][cs]], axis=0)
                res[ch] = _dot(lhs.astype(BF16), state[ch].astype(BF16))
            for ch in chains:
                c = step if ch[0] == 0 else n_chunk - 1 - step
                pr = probs[ch + (c // 2,)]
                cs = slice((c % 2) * CHUNK, (c % 2 + 1) * CHUNK)
                b_c = pr["kx"][c % 2][:, 0:DN_DK]
                if state[ch] is None:
                    out_c[ch + (c,)] = pr["o0"][cs]
                    state[ch] = b_c
                else:
                    g_last = jnp.exp(pr["tot"][(c % 2) * CHUNK:(c % 2) * CHUNK + 1])
                    out_c[ch + (c,)] = res[ch][DN_DK:DN_DK + CHUNK] + pr["o0"][cs]
                    state[ch] = state[ch] * g_last + res[ch][0:DN_DK] + b_c
            side_piece()
        outs = {}
        for ch in chains:
            d, g, h = ch
            st_ref[g, d * DN_HEADS + h] = state[ch]
            outs[ch] = jnp.concatenate([out_c[ch + (c,)] for c in range(n_chunk)], axis=0)
        return outs

    if n_blk == 1:
        outs = delta_blocks([(d, g, 0, g) for g in range(n_seq) for d in range(2)], side)
        for g in range(n_seq):
            for h in range(DN_HEADS):
                o_ref[g, :, h * DN_DK:(h + 1) * DN_DK] = outs[(0, g, h)] + outs[(1, g, h)]
    else:
        def scan_block(b, carry):
            r0 = (pl.multiple_of(b * bt, bt), pl.multiple_of((n_blk - 1 - b) * bt, bt))
            outs = delta_blocks([(d, g, r0[d], (g, d)) for g in range(n_seq) for d in range(2)])
            for (d, g, h), o in outs.items():
                o_ref[g, pl.ds(r0[d], bt), h * DN_DK:(h + 1) * DN_DK] += o
            return carry

        lax.fori_loop(0, n_blk, scan_block, 0)

    for _ in side:
        pass
    sfin_ref[...] = st_ref[...]


def _deltanet(act, gates, s0, n_seq, cast=()):
    nb, seq_len, _ = act.shape
    n_state = 2 * DN_HEADS
    zero_init = s0 is None
    n_steps = nb // n_seq
    seq = lambda w: pl.BlockSpec((n_seq, seq_len, w), lambda b: (b, 0, 0))
    state = pl.BlockSpec((n_seq, n_state, DN_DK, DN_DK), lambda b: (b, 0, 0, 0))
    cast_in, cast_out = [], []
    for w in cast:
        _, r, cols = w.shape
        hold = 1 if (r // n_steps) % BF16_ROWS == 0 else 2
        assert (r * hold) % (n_steps * BF16_ROWS) == 0 and cols % (CAST_PIECES * LANES) == 0
        rows = r * hold // n_steps
        cast_in.append(pl.BlockSpec((None, rows, cols), lambda b, hold=hold: (0, b // hold, 0)))
        cast_out.append(pl.BlockSpec((rows, cols), lambda b, hold=hold: (b // hold, 0)))
    operands = (act, gates) + (() if zero_init else (s0,)) + tuple(cast)
    return pl.pallas_call(
        functools.partial(_deltanet_kernel, seq_len=seq_len, n_seq=n_seq, zero_init=zero_init,
                          n_cast=len(cast)),
        grid=(n_steps,),
        in_specs=[seq(QKV_W), seq(LANES)] + ([] if zero_init else [state]) + cast_in,
        out_specs=[seq(DN_W), state] + cast_out,
        out_shape=[jax.ShapeDtypeStruct((nb, seq_len, DN_W), F32),
                   jax.ShapeDtypeStruct((nb, n_state, DN_DK, DN_DK), F32)]
                  + [jax.ShapeDtypeStruct(w.shape[1:], BF16) for w in cast],
        scratch_shapes=[pltpu.VMEM((n_seq, n_state, DN_DK, DN_DK), F32)],
        compiler_params=_params(1),
        name="deltanet",
    )(*operands)


def _layer(x, mod, s0, row_len, n_seq, p, w_tail):
    nb, seq_len, _ = x.shape
    n = nb * seq_len
    tiles_per_mod = (n // mod.shape[0]) // FFN_TILE
    x1, act, z, cv, gates = _ffn1_proj(x.reshape(n, D_MODEL), mod, p["norm_g"], p["wgu1"], p["wd1"],
                                       p["win"], p["dn_conv_w"], p["alog_row"], p["dtb_row"],
                                       p["cv_dw_w"], p["cv_dw_b"], p["cv_ln_g"], p["cv_ln_b"],
                                       tiles_per_mod, row_len)
    to_cast = tuple(w_tail) if w_tail[0].dtype != BF16 else ()
    o, s_fin, *cast = _deltanet(act.reshape(nb, seq_len, QKV_W), gates.reshape(nb, seq_len, LANES), s0,
                                n_seq, to_cast)
    wout, wgu2, wd2 = cast if to_cast else w_tail
    y = _out_ffn2(x1, o.reshape(n, DN_W), z, cv, mod, p["norm_g"], p["dn_norm_g"], wout, wgu2, wd2,
                  tiles_per_mod)
    return y.reshape(nb, seq_len, D_MODEL), s_fin, (wout, wgu2, wd2)


def kernel(x_prompt, x_sample, state_delta, c, c_ctx, w_mod, b_mod, norm_g, ffn1_w_in, ffn1_w_out, w_in,
           dn_conv_w, dn_a_log, dn_dt_bias, dn_norm_g, cv_dw_w, cv_dw_b, cv_ln_g, cv_ln_b, w_out,
           ffn2_w_in, ffn2_w_out):
    depth = w_mod.shape[0]
    assert depth == 1, "one trunk layer"
    batch, seq_len, _ = x_prompt.shape
    dec_batch, dec_seq, _ = x_sample.shape
    n_state = 2 * DN_HEADS

    assert 1 + dec_batch <= SUBLANES and dn_conv_w.shape[1] == SHORT_CONV and cv_dw_w.shape[1] == CONV_K
    cond = jnp.zeros((SUBLANES, D_MODEL), F32).at[0].set(c_ctx).at[1:1 + dec_batch].set(c)
    mod = _modulation(cond, w_mod[0], b_mod[0]).reshape(SUBLANES, N_MOD, D_MODEL)
    mod_ctx, mod_lat = mod[0:1], mod[1:1 + dec_batch]

    win = _reorder_w_in(w_in)
    pad8 = lambda v: jnp.zeros((1, LANES), F32).at[0, 0:n_state].set(v.reshape(n_state))
    p = dict(norm_g=norm_g[0], wgu1=ffn1_w_in[0].astype(BF16), wd1=ffn1_w_out[0].astype(BF16), win=win,
             dn_conv_w=dn_conv_w[0], alog_row=pad8(dn_a_log[0]), dtb_row=pad8(dn_dt_bias[0]),
             dn_norm_g=dn_norm_g[0].reshape(1, DN_DK), cv_dw_w=cv_dw_w[0],
             cv_dw_b=cv_dw_b[0].reshape(1, CV_W), cv_ln_g=cv_ln_g[0].reshape(1, CV_W),
             cv_ln_b=cv_ln_b[0].reshape(1, CV_W))

    y_p, s_ctx, w_tail = _layer(x_prompt, mod_ctx, None, seq_len, CTX_SEQS_PER_STEP, p,
                                (w_out, ffn2_w_in, ffn2_w_out))
    s_lat = state_delta[:, 0].reshape(dec_batch, n_state, DN_DK, DN_DK)
    y_s, _, _ = _layer(x_sample, mod_lat, s_lat, GRID_W, 1, p, w_tail)
    new_state = s_ctx.reshape(batch, 1, 2, DN_HEADS, DN_DK, DN_DK).astype(x_prompt.dtype)
    return (y_p, y_s, new_state)
```

```python
import functools

import jax
import jax.numpy as jnp
from jax import lax
from jax.experimental import pallas as pl
from jax.experimental.pallas import tpu as pltpu

F32 = jnp.float32
BF16 = jnp.bfloat16

D_MODEL = 1024
D_FF = 2816
N_MOD = 9
GRID_W = 64
DN_W = 512
CV_W = 512
DN_HEADS = 4
DN_DK = 128
CHUNK = 64
SHORT_CONV = 3
CONV_K = 31
EPS = 1e-6

LANES = 128
SUBLANES = 8
BLOCK_T = 256
PAIR_T = 2 * CHUNK
GATE_BETA = 2 * DN_HEADS
GATE_TOT = 4 * DN_HEADS
BF16_ROWS = 16
CAST_PIECES = 4
CTX_SEQS_PER_STEP = 1
MOD_STEPS = 4
FFN_TILE = 512
SUB_TILE = 256
CONV_HALO = 16
CONV_ROWS = 32
VMEM_LIMIT_BYTES = 56 * 1024 * 1024
FF_CHUNKS = ((0, 768), (768, 768), (1536, 768), (2304, 512))
QKV_W = 3 * DN_W
IN_QKV = (0, QKV_W)
IN_Z = (QKV_W, QKV_W + DN_W)
IN_GLU = (QKV_W + DN_W, QKV_W + DN_W + 2 * CV_W)
IN_AB = (QKV_W + DN_W + 2 * CV_W, QKV_W + DN_W + 2 * CV_W + LANES)
IN_COLS_PADDED = IN_AB[1]


def _dot(a, b):
    return jnp.dot(a, b, preferred_element_type=F32)


def _dot_nt(a, b):
    return lax.dot_general(a, b, (((1,), (1,)), ((), ())), preferred_element_type=F32)


def _silu(x):
    return x * jax.nn.sigmoid(x)


def _rms(x, g):
    return x * lax.rsqrt(jnp.mean(x * x, axis=-1, keepdims=True) + EPS) * g


def _const_spec(shape):
    nd = len(shape)
    return pl.BlockSpec(shape, lambda *_: (0,) * nd, pipeline_mode=pl.Buffered(1))


def _params(n_grid_dims):
    return pltpu.CompilerParams(dimension_semantics=("arbitrary",) * n_grid_dims,
                                vmem_limit_bytes=VMEM_LIMIT_BYTES)


def _mod_kernel(c_ref, w_ref, b_ref, o_ref):
    s = _silu(c_ref[...]).astype(BF16)
    o_ref[...] = _dot(s, w_ref[...].astype(BF16)) + b_ref[...]


def _modulation(cond, w_mod, b_mod):
    rows = cond.shape[0]
    n_out = w_mod.shape[1]
    tn = n_out // MOD_STEPS
    return pl.pallas_call(
        _mod_kernel,
        grid=(MOD_STEPS,),
        in_specs=[pl.BlockSpec((rows, D_MODEL), lambda j: (0, 0)),
                  pl.BlockSpec((D_MODEL, tn), lambda j: (0, j)),
                  pl.BlockSpec((1, tn), lambda j: (0, j))],
        out_specs=pl.BlockSpec((rows, tn), lambda j: (0, j)),
        out_shape=jax.ShapeDtypeStruct((rows, n_out), F32),
        compiler_params=_params(1),
        name="modulation",
    )(cond, w_mod, b_mod.reshape(1, n_out))


def _reorder_w_in_kernel(w_ref, o_ref):
    ab0 = QKV_W
    z0 = ab0 + 4 * DN_HEADS
    glu0 = z0 + DN_W
    cols = w_ref.shape[1]

    def put(dst, src):
        for r in range(0, dst[1] - dst[0], LANES):
            o_ref[:, dst[0] + r:dst[0] + r + LANES] = jnp.transpose(w_ref[src + r:src + r + LANES, :]).astype(BF16)

    put(IN_QKV, 0)
    put(IN_Z, z0)
    put(IN_GLU, glu0)
    ab = jnp.concatenate([w_ref[ab0:z0, :], jnp.zeros((LANES - 4 * DN_HEADS, cols), F32)], axis=0)
    o_ref[:, IN_AB[0]:IN_AB[1]] = jnp.transpose(ab).astype(BF16)


def _reorder_w_in(w_in_t):
    _, rows, cols = w_in_t.shape
    tc = SUB_TILE
    return pl.pallas_call(
        _reorder_w_in_kernel,
        grid=(cols // tc,),
        in_specs=[pl.BlockSpec((None, rows, tc), lambda i: (0, 0, i))],
        out_specs=pl.BlockSpec((tc, IN_COLS_PADDED), lambda i: (i, 0)),
        out_shape=jax.ShapeDtypeStruct((cols, IN_COLS_PADDED), BF16),
        compiler_params=_params(1),
        name="reorder_w_in",
    )(w_in_t)


def _interleave(primary, secondary):
    more_a = more_b = True
    while more_a or more_b:
        if more_a:
            more_a = next(primary, StopIteration) is not StopIteration
        if more_b:
            more_b = secondary is not None and next(secondary, StopIteration) is not StopIteration


def _ffn1_proj_kernel(x_ref, mod_ref, g_ref, wgu_ref, wd_ref, win_ref, cw_ref, alog_ref, dtb_ref,
                      ccw_ref, ccb_ref, clg_ref, clb_ref,
                      x1_ref, act_ref, z_ref, cv_ref, gates_ref, xp_ref, cacc_ref, *, row_len):
    m = mod_ref[...]
    g = g_ref[...]
    ts = SUB_TILE
    n_rows = ts // row_len
    pitch = row_len + 2 * CONV_HALO
    total = n_rows * pitch
    grp = CONV_ROWS
    lead = CONV_HALO - CONV_K // 2
    rowi = lax.broadcasted_iota(jnp.int32, (ts, LANES), 0)
    lanei = lax.broadcasted_iota(jnp.int32, (ts, LANES), 1)
    pos_row = jnp.bitwise_and(rowi, row_len - 1)
    keep_prev = pos_row != 0
    keep_next = pos_row != row_len - 1
    pos_chunk = jnp.bitwise_and(rowi, CHUNK - 1)

    def matmul_part(rows, out):
        x = x_ref[rows, :]
        hb = (_rms(x, g[0:1]) * (1.0 + m[1:2]) + m[0:1]).astype(BF16)
        acc = None
        for s, n in FF_CHUNKS:
            gt = _dot(hb, wgu_ref[:, s:s + n])
            up = _dot(hb, wgu_ref[:, D_FF + s:D_FF + s + n])
            a = (_silu(gt) * up).astype(BF16)
            p = _dot(a, wd_ref[s:s + n, :])
            acc = p if acc is None else acc + p
            yield
        x1 = x + 0.5 * (m[2:3] * _rms(acc, g[1:2]))
        x1_ref[rows, :] = x1
        h1 = (_rms(x1, g[2:3]) * (1.0 + m[4:5]) + m[3:4]).astype(BF16)
        out["qkv"] = _dot(h1, win_ref[:, IN_QKV[0]:IN_QKV[1]])
        out["ab"] = _dot(h1, win_ref[:, IN_AB[0]:IN_AB[1]])
        z_ref[rows, :] = _dot(h1, win_ref[:, IN_Z[0]:IN_Z[1]])
        out["glu"] = _dot(h1, win_ref[:, IN_GLU[0]:IN_GLU[1]])

    def vector_part(rows, inp):
        qkv, ab, glu = inp["qkv"], inp["ab"], inp["glu"]
        for s in range(QKV_W // LANES):
            ls = slice(s * LANES, (s + 1) * LANES)
            xs = qkv[:, ls]
            w = cw_ref[:, ls]
            xp = jnp.where(keep_prev, pltpu.roll(xs, 1, 0), 0.0)
            xn = jnp.where(keep_next, pltpu.roll(xs, ts - 1, 0), 0.0)
            y = _silu(w[0:1] * xp + w[1:2] * xs + w[2:3] * xn)
            if s < 2 * DN_HEADS:
                y = y * lax.rsqrt(jnp.sum(y * y, axis=-1, keepdims=True) + EPS)
            if s < DN_HEADS:
                y = y * (DN_DK ** -0.5)
            act_ref[rows, ls] = y
            if s % 4 == 3:
                yield
        t = ab + dtb_ref[...]
        softplus = jnp.maximum(t, 0.0) + jnp.log1p(jnp.exp(-jnp.abs(t)))
        gate = -jnp.exp(alog_ref[...]) * softplus
        pre = gate
        suf = gate
        for sh in (1, 2, 4, 8, 16, 32):
            pre = pre + jnp.where(pos_chunk >= sh, pltpu.roll(pre, sh, 0), 0.0)
            suf = suf + jnp.where(pos_chunk < CHUNK - sh, pltpu.roll(suf, ts - sh, 0), 0.0)
        gc = jnp.where(lanei < DN_HEADS, pre, suf)
        tot = pltpu.roll(pre + suf - gate, GATE_TOT, 1)
        gates_ref[rows, :] = jnp.where(lanei < GATE_BETA, gc,
                                       jnp.where(lanei < GATE_TOT, jax.nn.sigmoid(ab), tot))
        yield
        zeros = jnp.zeros((CONV_HALO, LANES), F32)
        for lg in range(CV_W // LANES):
            x = glu[:, lg * LANES:(lg + 1) * LANES] * jax.nn.sigmoid(
                glu[:, CV_W + lg * LANES:CV_W + (lg + 1) * LANES])
            for r in range(n_rows):
                base = r * pitch
                xp_ref[0, lg, base:base + CONV_HALO, :] = zeros
                xp_ref[0, lg, base + CONV_HALO:base + CONV_HALO + row_len, :] = (
                    x[r * row_len:(r + 1) * row_len, :])
                xp_ref[0, lg, base + CONV_HALO + row_len:base + pitch, :] = zeros
            for b in range(1, SUBLANES):
                xp_ref[b, lg, 0:total - SUBLANES, :] = xp_ref[0, lg, b:b + total - SUBLANES, :]
            yield
        for i in range(ts // grp):
            base = i * grp + (i * grp // row_len) * (2 * CONV_HALO)
            for lg in range(CV_W // LANES):
                ls = slice(lg * LANES, (lg + 1) * LANES)
                acc = jnp.zeros((grp, LANES), F32) + ccb_ref[:, ls]
                for j in range(CONV_K):
                    shift = (lead + j) % SUBLANES
                    start = base + lead + j - shift
                    acc = acc + ccw_ref[j:j + 1, ls] * xp_ref[shift, lg, start:start + grp, :]
                cacc_ref[i * grp:(i + 1) * grp, ls] = acc
            yield
        c = cacc_ref[...]
        mu = jnp.mean(c, axis=-1, keepdims=True)
        var = jnp.mean(jnp.square(c - mu), axis=-1, keepdims=True)
        cn = (c - mu) * lax.rsqrt(var + EPS) * clg_ref[...] + clb_ref[...]
        cv_ref[rows, :] = _silu(cn).astype(BF16)

    n_sub = x_ref.shape[0] // ts
    carried = [dict() for _ in range(n_sub)]
    pending = None
    for s in range(n_sub):
        rows = pl.ds(s * ts, ts)
        _interleave(matmul_part(rows, carried[s]), pending)
        pending = vector_part(rows, carried[s])
    _interleave(pending, None)


def _ffn1_proj(x, mod, norm_g, wgu, wd, win, conv_w, alog_row, dtb_row, cv_w, cv_b, cv_ln_g, cv_ln_b,
               tiles_per_mod, row_len):
    n = x.shape[0]
    tm = FFN_TILE
    total = (SUB_TILE // row_len) * (row_len + 2 * CONV_HALO)
    row = lambda w: pl.BlockSpec((tm, w), lambda i: (i, 0))
    consts = (norm_g, wgu, wd, win, conv_w, alog_row, dtb_row, cv_w, cv_b, cv_ln_g, cv_ln_b)
    return pl.pallas_call(
        functools.partial(_ffn1_proj_kernel, row_len=row_len),
        grid=(n // tm,),
        in_specs=[row(D_MODEL),
                  pl.BlockSpec((None, N_MOD, D_MODEL), lambda i: (i // tiles_per_mod, 0, 0))]
                 + [_const_spec(c.shape) for c in consts],
        out_specs=[row(D_MODEL), row(QKV_W), row(DN_W), row(CV_W), row(LANES)],
        out_shape=[jax.ShapeDtypeStruct((n, D_MODEL), F32), jax.ShapeDtypeStruct((n, QKV_W), F32),
                   jax.ShapeDtypeStruct((n, DN_W), F32), jax.ShapeDtypeStruct((n, CV_W), BF16),
                   jax.ShapeDtypeStruct((n, LANES), F32)],
        scratch_shapes=[pltpu.VMEM((SUBLANES, CV_W // LANES, total, LANES), F32),
                        pltpu.VMEM((SUB_TILE, CV_W), F32)],
        compiler_params=_params(1),
        name="ffn1_proj",
    )(x, mod, *consts)


def _out_ffn2_kernel(x1_ref, o_ref, z_ref, cv_ref, mod_ref, g_ref, dng_ref, wout_ref, wgu_ref, wd_ref, y_ref):
    m = mod_ref[...]
    g = g_ref[...]
    dng = dng_ref[...]
    ts = SUB_TILE
    n_sub = x1_ref.shape[0] // ts

    def vector_part(s, out):
        rows = pl.ds(s * ts, ts)
        heads = []
        for h in range(DN_HEADS):
            hs = slice(h * DN_DK, (h + 1) * DN_DK)
            heads.append((_rms(o_ref[rows, hs], dng) * _silu(z_ref[rows, hs])).astype(BF16))
        out["og"] = jnp.concatenate(heads, axis=-1)
        yield

    def matmul_part(s, inp):
        rows = pl.ds(s * ts, ts)
        y = _dot(inp["og"], wout_ref[0:DN_W, :]) + _dot(cv_ref[rows, :], wout_ref[DN_W:, :])
        x2 = x1_ref[rows, :] + m[5:6] * _rms(y, g[3:4])
        hb = (_rms(x2, g[4:5]) * (1.0 + m[7:8]) + m[6:7]).astype(BF16)
        yield
        acc = None
        for c0, n in FF_CHUNKS:
            gt = _dot(hb, wgu_ref[:, c0:c0 + n])
            yield
            up = _dot(hb, wgu_ref[:, D_FF + c0:D_FF + c0 + n])
            a = (_silu(gt) * up).astype(BF16)
            yield
            p = _dot(a, wd_ref[c0:c0 + n, :])
            acc = p if acc is None else acc + p
            yield
        y_ref[rows, :] = x2 + 0.5 * (m[8:9] * _rms(acc, g[5:6]))

    carried = [dict() for _ in range(n_sub)]
    _interleave(vector_part(0, carried[0]), None)
    for s in range(n_sub):
        nxt = vector_part(s + 1, carried[s + 1]) if s + 1 < n_sub else None
        _interleave(matmul_part(s, carried[s]), nxt)


def _out_ffn2(x1, o, z, cv, mod, norm_g, dn_norm_g, wout, wgu, wd, tiles_per_mod):
    n = x1.shape[0]
    tm = FFN_TILE
    row = lambda w: pl.BlockSpec((tm, w), lambda i: (i, 0))
    consts = (norm_g, dn_norm_g, wout, wgu, wd)
    return pl.pallas_call(
        _out_ffn2_kernel,
        grid=(n // tm,),
        in_specs=[row(D_MODEL), row(DN_W), row(DN_W), row(CV_W),
                  pl.BlockSpec((None, N_MOD, D_MODEL), lambda i: (i // tiles_per_mod, 0, 0))]
                 + [_const_spec(c.shape) for c in consts],
        out_specs=row(D_MODEL),
        out_shape=jax.ShapeDtypeStruct((n, D_MODEL), F32),
        compiler_params=_params(1),
        name="out_ffn2",
    )(x1, o, z, cv, mod, *consts)


def _deltanet_kernel(*refs, seq_len, n_seq, zero_init, n_cast):
    refs = list(refs)
    act_ref, gates_ref = refs[0:2]
    n_in = 2 if zero_init else 3
    s0_ref = None if zero_init else refs[2]
    cast_in = refs[n_in:n_in + n_cast]
    o_ref, sfin_ref = refs[n_in + n_cast:n_in + n_cast + 2]
    cast_out = refs[n_in + n_cast + 2:n_in + 2 * n_cast + 2]
    st_ref = refs[-1]
    n_blk = seq_len // BLOCK_T

    def cast_pieces():
        for src, dst in zip(cast_in, cast_out):
            step = src.shape[1] // CAST_PIECES
            for j in range(CAST_PIECES):
                dst[:, j * step:(j + 1) * step] = src[:, j * step:(j + 1) * step].astype(BF16)
                yield

    side = cast_pieces()
    bt = BLOCK_T
    f32_inf = jnp.float32(jnp.inf)

    n_state = 2 * DN_HEADS
    if zero_init:
        st_ref[...] = jnp.zeros((n_seq, n_state, DN_DK, DN_DK), F32)
    else:
        st_ref[...] = s0_ref[...]
    if n_blk > 1:
        o_ref[...] = jnp.zeros((n_seq, seq_len, DN_W), F32)

    state_is_zero = zero_init and n_blk == 1
    pt = PAIR_T
    n_pair = bt // pt
    n_chunk = bt // CHUNK
    ri = lax.broadcasted_iota(jnp.int32, (pt, pt), 0)
    ci = lax.broadcasted_iota(jnp.int32, (pt, pt), 1)
    same = jnp.right_shift(ri, 6) == jnp.right_shift(ci, 6)
    incl = (same & (ri >= ci), same & (ri <= ci))
    strict = (same & (ri > ci), same & (ri < ci))
    first_chunk_lanes = ci < CHUNK

    def delta_blocks(jobs, side_jobs=None):
        def side_piece():
            if side_jobs is not None:
                next(side_jobs, None)

        gram = {}
        probs = {}
        for d, g, r0, key in jobs:
            rows = pl.ds(r0, bt)
            gates = gates_ref[g, rows, :]
            gates_t = jnp.transpose(gates)
            for h in range(DN_HEADS):
                c8 = d * DN_HEADS + h
                for p in range(n_pair):
                    ps = slice(p * pt, (p + 1) * pt)
                    prow = pl.ds(r0 + p * pt, pt)
                    qh = act_ref[g, prow, h * DN_DK:(h + 1) * DN_DK]
                    kh = act_ref[g, prow, DN_W + h * DN_DK:DN_W + (h + 1) * DN_DK]
                    vh = act_ref[g, prow, 2 * DN_W + h * DN_DK:2 * DN_W + (h + 1) * DN_DK]
                    if (key, h, p) not in gram:
                        khb = kh.astype(BF16)
                        gram[(key, h, p)] = (_dot_nt(jnp.concatenate([khb, qh.astype(BF16)], axis=0), khb),
                                             jnp.transpose(kh))
                    gr, kh_t = gram[(key, h, p)]
                    kk = gr[0:pt]
                    qk = gr[pt:2 * pt]
                    g_i = gates[ps, c8:c8 + 1]
                    g_j = gates_t[c8:c8 + 1, ps]
                    dec = jnp.exp(jnp.where(incl[d], g_i - g_j, -f32_inf))
                    b_i = gates[ps, GATE_BETA + c8:GATE_BETA + c8 + 1]
                    eg = jnp.exp(g_i)
                    kd_t = kh_t * jnp.exp(gates_t[GATE_TOT + c8:GATE_TOT + c8 + 1, ps] - g_j)
                    probs[(d, g, h, p)] = dict(
                        p=-(jnp.where(strict[d], kk * dec, 0.0) * b_i), qkm=(qk * dec).astype(BF16),
                        x=jnp.concatenate([vh * b_i, kh * (b_i * eg)], axis=-1), qd=qh * eg,
                        kd_t=[jnp.where(first_chunk_lanes, kd_t, 0.0).astype(BF16),
                              jnp.where(first_chunk_lanes, 0.0, kd_t).astype(BF16)],
                        tot=gates[ps, GATE_TOT + c8:GATE_TOT + c8 + 1])
        side_piece()
        for pr in probs.values():
            pb = pr["p"].astype(BF16)
            pr["q"] = pr["p"]
            pr["p"] = _dot(pb, pb)
        side_piece()
        for _ in range(4):
            for pr in probs.values():
                pb = pr["p"].astype(BF16)
                r = _dot(pb, jnp.concatenate([pb, pr["q"].astype(BF16)], axis=-1))
                pr["q"] = pr["q"] + pr["p"] + r[:, pt:2 * pt]
                pr["p"] = r[:, 0:pt]
            side_piece()
        for pr in probs.values():
            r = _dot(pr["p"].astype(BF16), pr["q"].astype(BF16))
            qm = pr["q"] + pr["p"] + r
            pr["x"] = pr["x"] + _dot(qm.astype(BF16), pr["x"].astype(BF16))
        side_piece()
        for pr in probs.values():
            xb = pr["x"].astype(BF16)
            qx = _dot(pr["qkm"], xb)
            pr["o0"] = qx[:, 0:DN_DK]
            pr["e"] = pr["qd"] - qx[:, DN_DK:2 * DN_DK]
            pr["kx"] = [_dot(kd_t, xb) for kd_t in pr["kd_t"]]
        side_piece()
        chains = [(d, g, h) for d, g, _, _ in jobs for h in range(DN_HEADS)]
        state = {(d, g, h): (None if state_is_zero else st_ref[g, d * DN_HEADS + h]) for d, g, h in chains}
        out_c = {}
        for step in range(n_chunk):
            res = {}
            for ch in chains:
                if state[ch] is None:
                    continue
                c = step if ch[0] == 0 else n_chunk - 1 - step
                pr = probs[ch + (c // 2,)]
                cs = slice((c % 2) * CHUNK, (c % 2 + 1) * CHUNK)
                lhs = jnp.concatenate([-pr["kx"][c % 2][:, DN_DK:2 * DN_DK], pr["e"][cs]], axis=0)
                res[ch] = _dot(lhs.astype(BF16), state[ch].astype(BF16))
            for ch in chains:
                c = step if ch[0] == 0 else n_chunk - 1 - step
                pr = probs[ch + (c // 2,)]
                cs = slice((c % 2) * CHUNK, (c % 2 + 1) * CHUNK)
                b_c = pr["kx"][c % 2][:, 0:DN_DK]
                if state[ch] is None:
                    out_c[ch + (c,)] = pr["o0"][cs]
                    state[ch] = b_c
                else:
                    g_last = jnp.exp(pr["tot"][(c % 2) * CHUNK:(c % 2) * CHUNK + 1])
                    out_c[ch + (c,)] = res[ch][DN_DK:DN_DK + CHUNK] + pr["o0"][cs]
                    state[ch] = state[ch] * g_last + res[ch][0:DN_DK] + b_c
            side_piece()
        outs = {}
        for ch in chains:
            d, g, h = ch
            st_ref[g, d * DN_HEADS + h] = state[ch]
            outs[ch] = jnp.concatenate([out_c[ch + (c,)] for c in range(n_chunk)], axis=0)
        return outs

    if n_blk == 1:
        outs = delta_blocks([(d, g, 0, g) for g in range(n_seq) for d in range(2)], side)
        for g in range(n_seq):
            for h in range(DN_HEADS):
                o_ref[g, :, h * DN_DK:(h + 1) * DN_DK] = outs[(0, g, h)] + outs[(1, g, h)]
    else:
        def scan_block(b, carry):
            r0 = (pl.multiple_of(b * bt, bt), pl.multiple_of((n_blk - 1 - b) * bt, bt))
            outs = delta_blocks([(d, g, r0[d], (g, d)) for g in range(n_seq) for d in range(2)])
            for (d, g, h), o in outs.items():
                o_ref[g, pl.ds(r0[d], bt), h * DN_DK:(h + 1) * DN_DK] += o
            return carry

        lax.fori_loop(0, n_blk, scan_block, 0)

    for _ in side:
        pass
    sfin_ref[...] = st_ref[...]


def _deltanet(act, gates, s0, n_seq, cast=()):
    nb, seq_len, _ = act.shape
    n_state = 2 * DN_HEADS
    zero_init = s0 is None
    n_steps = nb // n_seq
    seq = lambda w: pl.BlockSpec((n_seq, seq_len, w), lambda b: (b, 0, 0))
    state = pl.BlockSpec((n_seq, n_state, DN_DK, DN_DK), lambda b: (b, 0, 0, 0))
    cast_in, cast_out = [], []
    for w in cast:
        _, r, cols = w.shape
        hold = 1 if (r // n_steps) % BF16_ROWS == 0 else 2
        assert (r * hold) % (n_steps * BF16_ROWS) == 0 and cols % (CAST_PIECES * LANES) == 0
        rows = r * hold // n_steps
        cast_in.append(pl.BlockSpec((None, rows, cols), lambda b, hold=hold: (0, b // hold, 0)))
        cast_out.append(pl.BlockSpec((rows, cols), lambda b, hold=hold: (b // hold, 0)))
    operands = (act, gates) + (() if zero_init else (s0,)) + tuple(cast)
    return pl.pallas_call(
        functools.partial(_deltanet_kernel, seq_len=seq_len, n_seq=n_seq, zero_init=zero_init,
                          n_cast=len(cast)),
        grid=(n_steps,),
        in_specs=[seq(QKV_W), seq(LANES)] + ([] if zero_init else [state]) + cast_in,
        out_specs=[seq(DN_W), state] + cast_out,
        out_shape=[jax.ShapeDtypeStruct((nb, seq_len, DN_W), F32),
                   jax.ShapeDtypeStruct((nb, n_state, DN_DK, DN_DK), F32)]
                  + [jax.ShapeDtypeStruct(w.shape[1:], BF16) for w in cast],
        scratch_shapes=[pltpu.VMEM((n_seq, n_state, DN_DK, DN_DK), F32)],
        compiler_params=_params(1),
        name="deltanet",
    )(*operands)


def _layer(x, mod, s0, row_len, n_seq, p, w_tail):
    nb, seq_len, _ = x.shape
    n = nb * seq_len
    tiles_per_mod = (n // mod.shape[0]) // FFN_TILE
    x1, act, z, cv, gates = _ffn1_proj(x.reshape(n, D_MODEL), mod, p["norm_g"], p["wgu1"], p["wd1"],
                                       p["win"], p["dn_conv_w"], p["alog_row"], p["dtb_row"],
                                       p["cv_dw_w"], p["cv_dw_b"], p["cv_ln_g"], p["cv_ln_b"],
                                       tiles_per_mod, row_len)
    to_cast = tuple(w_tail) if w_tail[0].dtype != BF16 else ()
    o, s_fin, *cast = _deltanet(act.reshape(nb, seq_len, QKV_W), gates.reshape(nb, seq_len, LANES), s0,
                                n_seq, to_cast)
    wout, wgu2, wd2 = cast if to_cast else w_tail
    y = _out_ffn2(x1, o.reshape(n, DN_W), z, cv, mod, p["norm_g"], p["dn_norm_g"], wout, wgu2, wd2,
                  tiles_per_mod)
    return y.reshape(nb, seq_len, D_MODEL), s_fin, (wout, wgu2, wd2)


def kernel(x_prompt, x_sample, state_delta, c, c_ctx, w_mod, b_mod, norm_g, ffn1_w_in, ffn1_w_out, w_in,
           dn_conv_w, dn_a_log, dn_dt_bias, dn_norm_g, cv_dw_w, cv_dw_b, cv_ln_g, cv_ln_b, w_out,
           ffn2_w_in, ffn2_w_out):
    depth = w_mod.shape[0]
    assert depth == 1, "one trunk layer"
    batch, seq_len, _ = x_prompt.shape
    dec_batch, dec_seq, _ = x_sample.shape
    n_state = 2 * DN_HEADS

    assert 1 + dec_batch <= SUBLANES and dn_conv_w.shape[1] == SHORT_CONV and cv_dw_w.shape[1] == CONV_K
    cond = jnp.zeros((SUBLANES, D_MODEL), F32).at[0].set(c_ctx).at[1:1 + dec_batch].set(c)
    mod = _modulation(cond, w_mod[0], b_mod[0]).reshape(SUBLANES, N_MOD, D_MODEL)
    mod_ctx, mod_lat = mod[0:1], mod[1:1 + dec_batch]

    win = _reorder_w_in(jnp.transpose(w_in, (0, 2, 1)))
    pad8 = lambda v: jnp.zeros((1, LANES), F32).at[0, 0:n_state].set(v.reshape(n_state))
    p = dict(norm_g=norm_g[0], wgu1=ffn1_w_in[0].astype(BF16), wd1=ffn1_w_out[0].astype(BF16), win=win,
             dn_conv_w=dn_conv_w[0], alog_row=pad8(dn_a_log[0]), dtb_row=pad8(dn_dt_bias[0]),
             dn_norm_g=dn_norm_g[0].reshape(1, DN_DK), cv_dw_w=cv_dw_w[0],
             cv_dw_b=cv_dw_b[0].reshape(1, CV_W), cv_ln_g=cv_ln_g[0].reshape(1, CV_W),
             cv_ln_b=cv_ln_b[0].reshape(1, CV_W))

    y_p, s_ctx, w_tail = _layer(x_prompt, mod_ctx, None, seq_len, CTX_SEQS_PER_STEP, p,
                                (w_out, ffn2_w_in, ffn2_w_out))
    s_lat = state_delta[:, 0].reshape(dec_batch, n_state, DN_DK, DN_DK)
    y_s, _, _ = _layer(x_sample, mod_lat, s_lat, GRID_W, 1, p, w_tail)
    new_state = s_ctx.reshape(batch, 1, 2, DN_HEADS, DN_DK, DN_DK).astype(x_prompt.dtype)
    return (y_p, y_s, new_state)
```

```python
import functools

import jax
import jax.numpy as jnp
from jax import lax
from jax.experimental import pallas as pl
from jax.experimental.pallas import tpu as pltpu

F32 = jnp.float32
BF16 = jnp.bfloat16

D_MODEL = 1024
D_FF = 2816
N_MOD = 9
GRID_W = 64
DN_W = 512
CV_W = 512
DN_HEADS = 4
DN_DK = 128
CHUNK = 64
SHORT_CONV = 3
CONV_K = 31
EPS = 1e-6

LANES = 128
SUBLANES = 8
BLOCK_T = 256
PAIR_T = 2 * CHUNK
GATE_BETA = 2 * DN_HEADS
GATE_TOT = 4 * DN_HEADS
BF16_ROWS = 16
CAST_PIECES = 4
CTX_SEQS_PER_STEP = 1
MOD_STEPS = 4
FFN_TILE = 512
SUB_TILE = 256
CONV_HALO = 16
CONV_ROWS = 32
VMEM_LIMIT_BYTES = 56 * 1024 * 1024
FF_CHUNKS = ((0, 768), (768, 768), (1536, 768), (2304, 512))
QKV_W = 3 * DN_W
IN_QKV = (0, QKV_W)
IN_Z = (QKV_W, QKV_W + DN_W)
IN_GLU = (QKV_W + DN_W, QKV_W + DN_W + 2 * CV_W)
IN_AB = (QKV_W + DN_W + 2 * CV_W, QKV_W + DN_W + 2 * CV_W + LANES)
IN_COLS_PADDED = IN_AB[1]


def _dot(a, b):
    return jnp.dot(a, b, preferred_element_type=F32)


def _dot_nt(a, b):
    return lax.dot_general(a, b, (((1,), (1,)), ((), ())), preferred_element_type=F32)


def _silu(x):
    return x * jax.nn.sigmoid(x)


def _rms(x, g):
    return x * lax.rsqrt(jnp.mean(x * x, axis=-1, keepdims=True) + EPS) * g


def _const_spec(shape):
    nd = len(shape)
    return pl.BlockSpec(shape, lambda *_: (0,) * nd, pipeline_mode=pl.Buffered(1))


def _params(n_grid_dims):
    return pltpu.CompilerParams(dimension_semantics=("arbitrary",) * n_grid_dims,
                                vmem_limit_bytes=VMEM_LIMIT_BYTES)


def _mod_kernel(c_ref, w_ref, b_ref, o_ref):
    s = _silu(c_ref[...]).astype(BF16)
    o_ref[...] = _dot(s, w_ref[...].astype(BF16)) + b_ref[...]


def _modulation(cond, w_mod, b_mod):
    rows = cond.shape[0]
    n_out = w_mod.shape[1]
    tn = n_out // MOD_STEPS
    return pl.pallas_call(
        _mod_kernel,
        grid=(MOD_STEPS,),
        in_specs=[pl.BlockSpec((rows, D_MODEL), lambda j: (0, 0)),
                  pl.BlockSpec((D_MODEL, tn), lambda j: (0, j)),
                  pl.BlockSpec((1, tn), lambda j: (0, j))],
        out_specs=pl.BlockSpec((rows, tn), lambda j: (0, j)),
        out_shape=jax.ShapeDtypeStruct((rows, n_out), F32),
        compiler_params=_params(1),
        name="modulation",
    )(cond, w_mod, b_mod.reshape(1, n_out))


def _reorder_w_in_kernel(w_ref, o_ref):
    ab0 = QKV_W
    z0 = ab0 + 4 * DN_HEADS
    glu0 = z0 + DN_W
    cols = w_ref.shape[1]

    def put(dst, src):
        for r in range(0, dst[1] - dst[0], LANES):
            o_ref[:, dst[0] + r:dst[0] + r + LANES] = jnp.transpose(w_ref[src + r:src + r + LANES, :]).astype(BF16)

    put(IN_QKV, 0)
    put(IN_Z, z0)
    put(IN_GLU, glu0)
    ab = jnp.concatenate([w_ref[ab0:z0, :], jnp.zeros((LANES - 4 * DN_HEADS, cols), F32)], axis=0)
    o_ref[:, IN_AB[0]:IN_AB[1]] = jnp.transpose(ab).astype(BF16)


def _reorder_w_in(w_in_t):
    _, rows, cols = w_in_t.shape
    tc = SUB_TILE
    return pl.pallas_call(
        _reorder_w_in_kernel,
        grid=(cols // tc,),
        in_specs=[pl.BlockSpec((None, rows, tc), lambda i: (0, 0, i))],
        out_specs=pl.BlockSpec((tc, IN_COLS_PADDED), lambda i: (i, 0)),
        out_shape=jax.ShapeDtypeStruct((cols, IN_COLS_PADDED), BF16),
        compiler_params=_params(1),
        name="reorder_w_in",
    )(w_in_t)


def _interleave(primary, secondary):
    more_a = more_b = True
    while more_a or more_b:
        if more_a:
            more_a = next(primary, StopIteration) is not StopIteration
        if more_b:
            more_b = secondary is not None and next(secondary, StopIteration) is not StopIteration


def _conv_scratch(row_len):
    total = (SUB_TILE // row_len) * (row_len + 2 * CONV_HALO)
    return [pltpu.VMEM((SUBLANES, CV_W // LANES, total, LANES), F32), pltpu.VMEM((SUB_TILE, CV_W), F32)]


def _conv_module(glu_slab, store_cv, xp_ref, cacc_ref, ccw_ref, ccb_ref, clg_ref, clb_ref, row_len):
    ts = SUB_TILE
    n_rows = ts // row_len
    pitch = row_len + 2 * CONV_HALO
    total = n_rows * pitch
    grp = CONV_ROWS
    lead = CONV_HALO - CONV_K // 2
    zeros = jnp.zeros((CONV_HALO, LANES), F32)
    for lg in range(CV_W // LANES):
        x = glu_slab(lg * LANES) * jax.nn.sigmoid(glu_slab(CV_W + lg * LANES))
        for r in range(n_rows):
            base = r * pitch
            xp_ref[0, lg, base:base + CONV_HALO, :] = zeros
            xp_ref[0, lg, base + CONV_HALO:base + CONV_HALO + row_len, :] = (
                x[r * row_len:(r + 1) * row_len, :])
            xp_ref[0, lg, base + CONV_HALO + row_len:base + pitch, :] = zeros
        for b in range(1, SUBLANES):
            xp_ref[b, lg, 0:total - SUBLANES, :] = xp_ref[0, lg, b:b + total - SUBLANES, :]
        yield
    for i in range(ts // grp):
        base = i * grp + (i * grp // row_len) * (2 * CONV_HALO)
        for lg in range(CV_W // LANES):
            ls = slice(lg * LANES, (lg + 1) * LANES)
            acc = jnp.zeros((grp, LANES), F32) + ccb_ref[:, ls]
            for j in range(CONV_K):
                shift = (lead + j) % SUBLANES
                start = base + lead + j - shift
                acc = acc + ccw_ref[j:j + 1, ls] * xp_ref[shift, lg, start:start + grp, :]
            cacc_ref[i * grp:(i + 1) * grp, ls] = acc
        yield
    c = cacc_ref[...]
    mu = jnp.mean(c, axis=-1, keepdims=True)
    var = jnp.mean(jnp.square(c - mu), axis=-1, keepdims=True)
    cn = (c - mu) * lax.rsqrt(var + EPS) * clg_ref[...] + clb_ref[...]
    store_cv(_silu(cn).astype(BF16))


def _ffn1_proj_kernel(x_ref, mod_ref, g_ref, wgu_ref, wd_ref, win_ref, cw_ref, alog_ref, dtb_ref,
                      ccw_ref, ccb_ref, clg_ref, clb_ref,
                      x1_ref, act_ref, z_ref, cv_ref, gates_ref, *conv_scratch, row_len, with_conv):
    m = mod_ref[...]
    g = g_ref[...]
    ts = SUB_TILE
    rowi = lax.broadcasted_iota(jnp.int32, (ts, LANES), 0)
    lanei = lax.broadcasted_iota(jnp.int32, (ts, LANES), 1)
    pos_row = jnp.bitwise_and(rowi, row_len - 1)
    keep_prev = pos_row != 0
    keep_next = pos_row != row_len - 1
    pos_chunk = jnp.bitwise_and(rowi, CHUNK - 1)

    def matmul_part(rows, out):
        x = x_ref[rows, :]
        hb = (_rms(x, g[0:1]) * (1.0 + m[1:2]) + m[0:1]).astype(BF16)
        acc = None
        for s, n in FF_CHUNKS:
            gt = _dot(hb, wgu_ref[:, s:s + n])
            up = _dot(hb, wgu_ref[:, D_FF + s:D_FF + s + n])
            a = (_silu(gt) * up).astype(BF16)
            p = _dot(a, wd_ref[s:s + n, :])
            acc = p if acc is None else acc + p
            yield
        x1 = x + 0.5 * (m[2:3] * _rms(acc, g[1:2]))
        x1_ref[rows, :] = x1
        h1 = (_rms(x1, g[2:3]) * (1.0 + m[4:5]) + m[3:4]).astype(BF16)
        out["qkv"] = _dot(h1, win_ref[:, IN_QKV[0]:IN_QKV[1]])
        out["ab"] = _dot(h1, win_ref[:, IN_AB[0]:IN_AB[1]])
        z_ref[rows, :] = _dot(h1, win_ref[:, IN_Z[0]:IN_Z[1]])
        glu = _dot(h1, win_ref[:, IN_GLU[0]:IN_GLU[1]])
        if with_conv:
            out["glu"] = glu
        else:
            cv_ref[rows, :] = glu

    def vector_part(rows, inp):
        qkv, ab = inp["qkv"], inp["ab"]
        for s in range(QKV_W // LANES):
            ls = slice(s * LANES, (s + 1) * LANES)
            xs = qkv[:, ls]
            w = cw_ref[:, ls]
            xp = jnp.where(keep_prev, pltpu.roll(xs, 1, 0), 0.0)
            xn = jnp.where(keep_next, pltpu.roll(xs, ts - 1, 0), 0.0)
            y = _silu(w[0:1] * xp + w[1:2] * xs + w[2:3] * xn)
            if s < 2 * DN_HEADS:
                y = y * lax.rsqrt(jnp.sum(y * y, axis=-1, keepdims=True) + EPS)
            if s < DN_HEADS:
                y = y * (DN_DK ** -0.5)
            act_ref[rows, ls] = y
            if s % 4 == 3:
                yield
        t = ab + dtb_ref[...]
        softplus = jnp.maximum(t, 0.0) + jnp.log1p(jnp.exp(-jnp.abs(t)))
        gate = -jnp.exp(alog_ref[...]) * softplus
        pre = gate
        suf = gate
        for sh in (1, 2, 4, 8, 16, 32):
            pre = pre + jnp.where(pos_chunk >= sh, pltpu.roll(pre, sh, 0), 0.0)
            suf = suf + jnp.where(pos_chunk < CHUNK - sh, pltpu.roll(suf, ts - sh, 0), 0.0)
        gc = jnp.where(lanei < DN_HEADS, pre, suf)
        tot = pltpu.roll(pre + suf - gate, GATE_TOT, 1)
        gates_ref[rows, :] = jnp.where(lanei < GATE_BETA, gc,
                                       jnp.where(lanei < GATE_TOT, jax.nn.sigmoid(ab), tot))
        yield
        if with_conv:
            glu = inp["glu"]

            def store_cv(v):
                cv_ref[rows, :] = v

            yield from _conv_module(lambda c: glu[:, c:c + LANES], store_cv, *conv_scratch,
                                    ccw_ref, ccb_ref, clg_ref, clb_ref, row_len)

    n_sub = x_ref.shape[0] // ts
    carried = [dict() for _ in range(n_sub)]
    pending = None
    for s in range(n_sub):
        rows = pl.ds(s * ts, ts)
        _interleave(matmul_part(rows, carried[s]), pending)
        pending = vector_part(rows, carried[s])
    _interleave(pending, None)


def _ffn1_proj(x, mod, norm_g, wgu, wd, win, conv_w, alog_row, dtb_row, cv_w, cv_b, cv_ln_g, cv_ln_b,
               tiles_per_mod, row_len, with_conv):
    n = x.shape[0]
    tm = FFN_TILE
    row = lambda w: pl.BlockSpec((tm, w), lambda i: (i, 0))
    consts = (norm_g, wgu, wd, win, conv_w, alog_row, dtb_row, cv_w, cv_b, cv_ln_g, cv_ln_b)
    cv_w_out, cv_dtype = (CV_W, BF16) if with_conv else (2 * CV_W, F32)
    return pl.pallas_call(
        functools.partial(_ffn1_proj_kernel, row_len=row_len, with_conv=with_conv),
        grid=(n // tm,),
        in_specs=[row(D_MODEL),
                  pl.BlockSpec((None, N_MOD, D_MODEL), lambda i: (i // tiles_per_mod, 0, 0))]
                 + [_const_spec(c.shape) for c in consts],
        out_specs=[row(D_MODEL), row(QKV_W), row(DN_W), row(cv_w_out), row(LANES)],
        out_shape=[jax.ShapeDtypeStruct((n, D_MODEL), F32), jax.ShapeDtypeStruct((n, QKV_W), F32),
                   jax.ShapeDtypeStruct((n, DN_W), F32), jax.ShapeDtypeStruct((n, cv_w_out), cv_dtype),
                   jax.ShapeDtypeStruct((n, LANES), F32)],
        scratch_shapes=_conv_scratch(row_len) if with_conv else [],
        compiler_params=_params(1),
        name="ffn1_proj",
    )(x, mod, *consts)


def _out_ffn2_kernel(x1_ref, o_ref, z_ref, cv_ref, mod_ref, g_ref, dng_ref, wout_ref, wgu_ref, wd_ref, y_ref):
    m = mod_ref[...]
    g = g_ref[...]
    dng = dng_ref[...]
    ts = SUB_TILE
    n_sub = x1_ref.shape[0] // ts

    def vector_part(s, out):
        rows = pl.ds(s * ts, ts)
        heads = []
        for h in range(DN_HEADS):
            hs = slice(h * DN_DK, (h + 1) * DN_DK)
            heads.append((_rms(o_ref[rows, hs], dng) * _silu(z_ref[rows, hs])).astype(BF16))
        out["og"] = jnp.concatenate(heads, axis=-1)
        yield

    def matmul_part(s, inp):
        rows = pl.ds(s * ts, ts)
        y = _dot(inp["og"], wout_ref[0:DN_W, :]) + _dot(cv_ref[rows, :], wout_ref[DN_W:, :])
        x2 = x1_ref[rows, :] + m[5:6] * _rms(y, g[3:4])
        hb = (_rms(x2, g[4:5]) * (1.0 + m[7:8]) + m[6:7]).astype(BF16)
        yield
        acc = None
        for c0, n in FF_CHUNKS:
            gt = _dot(hb, wgu_ref[:, c0:c0 + n])
            yield
            up = _dot(hb, wgu_ref[:, D_FF + c0:D_FF + c0 + n])
            a = (_silu(gt) * up).astype(BF16)
            yield
            p = _dot(a, wd_ref[c0:c0 + n, :])
            acc = p if acc is None else acc + p
            yield
        y_ref[rows, :] = x2 + 0.5 * (m[8:9] * _rms(acc, g[5:6]))

    carried = [dict() for _ in range(n_sub)]
    _interleave(vector_part(0, carried[0]), None)
    for s in range(n_sub):
        nxt = vector_part(s + 1, carried[s + 1]) if s + 1 < n_sub else None
        _interleave(matmul_part(s, carried[s]), nxt)


def _out_ffn2(x1, o, z, cv, mod, norm_g, dn_norm_g, wout, wgu, wd, tiles_per_mod):
    n = x1.shape[0]
    tm = FFN_TILE
    row = lambda w: pl.BlockSpec((tm, w), lambda i: (i, 0))
    consts = (norm_g, dn_norm_g, wout, wgu, wd)
    return pl.pallas_call(
        _out_ffn2_kernel,
        grid=(n // tm,),
        in_specs=[row(D_MODEL), row(DN_W), row(DN_W), row(CV_W),
                  pl.BlockSpec((None, N_MOD, D_MODEL), lambda i: (i // tiles_per_mod, 0, 0))]
                 + [_const_spec(c.shape) for c in consts],
        out_specs=row(D_MODEL),
        out_shape=jax.ShapeDtypeStruct((n, D_MODEL), F32),
        compiler_params=_params(1),
        name="out_ffn2",
    )(x1, o, z, cv, mod, *consts)


def _deltanet_kernel(*refs, seq_len, n_seq, zero_init, n_cast, with_conv):
    refs = list(refs)
    act_ref, gates_ref = refs[0:2]
    n_in = 2 if zero_init else 3
    s0_ref = None if zero_init else refs[2]
    if with_conv:
        glu_ref, ccw_ref, ccb_ref, clg_ref, clb_ref = refs[n_in:n_in + 5]
        n_in += 5
    cast_in = refs[n_in:n_in + n_cast]
    o_ref, sfin_ref = refs[n_in + n_cast:n_in + n_cast + 2]
    n_out = n_in + n_cast + 2
    if with_conv:
        cv_ref = refs[n_out]
        n_out += 1
        xp_ref, cacc_ref = refs[-2:]
        st_ref = refs[-3]
    else:
        st_ref = refs[-1]
    cast_out = refs[n_out:n_out + n_cast]
    n_blk = seq_len // BLOCK_T

    def cast_pieces():
        for src, dst in zip(cast_in, cast_out):
            step = src.shape[1] // CAST_PIECES
            for j in range(CAST_PIECES):
                dst[:, j * step:(j + 1) * step] = src[:, j * step:(j + 1) * step].astype(BF16)
                yield

    def conv_pieces():
        for g in range(n_seq if with_conv else 0):
            def store_cv(v, g=g):
                cv_ref[g] = v

            yield from _conv_module(lambda c, g=g: glu_ref[g, :, c:c + LANES], store_cv, xp_ref, cacc_ref,
                                    ccw_ref, ccb_ref, clg_ref, clb_ref, seq_len)
            yield

    side = [cast_pieces(), conv_pieces()]
    bt = BLOCK_T
    f32_inf = jnp.float32(jnp.inf)

    n_state = 2 * DN_HEADS
    if zero_init:
        st_ref[...] = jnp.zeros((n_seq, n_state, DN_DK, DN_DK), F32)
    else:
        st_ref[...] = s0_ref[...]
    if n_blk > 1:
        o_ref[...] = jnp.zeros((n_seq, seq_len, DN_W), F32)

    state_is_zero = zero_init and n_blk == 1
    pt = PAIR_T
    n_pair = bt // pt
    n_chunk = bt // CHUNK
    ri = lax.broadcasted_iota(jnp.int32, (pt, pt), 0)
    ci = lax.broadcasted_iota(jnp.int32, (pt, pt), 1)
    same = jnp.right_shift(ri, 6) == jnp.right_shift(ci, 6)
    incl = (same & (ri >= ci), same & (ri <= ci))
    strict = (same & (ri > ci), same & (ri < ci))
    first_chunk_lanes = ci < CHUNK

    def delta_blocks(jobs, side_jobs=None):
        def side_piece():
            for job in side_jobs or ():
                next(job, None)

        side_piece()
        gram = {}
        probs = {}
        for d, g, r0, key in jobs:
            rows = pl.ds(r0, bt)
            gates = gates_ref[g, rows, :]
            gates_t = jnp.transpose(gates)
            for h in range(DN_HEADS):
                c8 = d * DN_HEADS + h
                for p in range(n_pair):
                    ps = slice(p * pt, (p + 1) * pt)
                    prow = pl.ds(r0 + p * pt, pt)
                    qh = act_ref[g, prow, h * DN_DK:(h + 1) * DN_DK]
                    kh = act_ref[g, prow, DN_W + h * DN_DK:DN_W + (h + 1) * DN_DK]
                    vh = act_ref[g, prow, 2 * DN_W + h * DN_DK:2 * DN_W + (h + 1) * DN_DK]
                    if (key, h, p) not in gram:
                        khb = kh.astype(BF16)
                        gram[(key, h, p)] = (_dot_nt(jnp.concatenate([khb, qh.astype(BF16)], axis=0), khb),
                                             jnp.transpose(kh))
                    gr, kh_t = gram[(key, h, p)]
                    kk = gr[0:pt]
                    qk = gr[pt:2 * pt]
                    g_i = gates[ps, c8:c8 + 1]
                    g_j = gates_t[c8:c8 + 1, ps]
                    dec = jnp.exp(jnp.where(incl[d], g_i - g_j, -f32_inf))
                    b_i = gates[ps, GATE_BETA + c8:GATE_BETA + c8 + 1]
                    eg = jnp.exp(g_i)
                    kd_t = kh_t * jnp.exp(gates_t[GATE_TOT + c8:GATE_TOT + c8 + 1, ps] - g_j)
                    probs[(d, g, h, p)] = dict(
                        p=-(jnp.where(strict[d], kk * dec, 0.0) * b_i), qkm=(qk * dec).astype(BF16),
                        x=jnp.concatenate([vh * b_i, kh * (b_i * eg)], axis=-1), qd=qh * eg,
                        kd_t=[jnp.where(first_chunk_lanes, kd_t, 0.0).astype(BF16),
                              jnp.where(first_chunk_lanes, 0.0, kd_t).astype(BF16)],
                        tot=gates[ps, GATE_TOT + c8:GATE_TOT + c8 + 1])
        side_piece()
        for pr in probs.values():
            pb = pr["p"].astype(BF16)
            pr["q"] = pr["p"]
            pr["p"] = _dot(pb, pb)
        side_piece()
        for _ in range(4):
            for pr in probs.values():
                pb = pr["p"].astype(BF16)
                r = _dot(pb, jnp.concatenate([pb, pr["q"].astype(BF16)], axis=-1))
                pr["q"] = pr["q"] + pr["p"] + r[:, pt:2 * pt]
                pr["p"] = r[:, 0:pt]
            side_piece()
        for pr in probs.values():
            r = _dot(pr["p"].astype(BF16), pr["q"].astype(BF16))
            qm = pr["q"] + pr["p"] + r
            pr["x"] = pr["x"] + _dot(qm.astype(BF16), pr["x"].astype(BF16))
        side_piece()
        for pr in probs.values():
            xb = pr["x"].astype(BF16)
            qx = _dot(pr["qkm"], xb)
            pr["o0"] = qx[:, 0:DN_DK]
            pr["e"] = pr["qd"] - qx[:, DN_DK:2 * DN_DK]
            pr["kx"] = [_dot(kd_t, xb) for kd_t in pr["kd_t"]]
        side_piece()
        chains = [(d, g, h) for d, g, _, _ in jobs for h in range(DN_HEADS)]
        state = {(d, g, h): (None if state_is_zero else st_ref[g, d * DN_HEADS + h]) for d, g, h in chains}
        out_c = {}
        for step in range(n_chunk):
            res = {}
            for ch in chains:
                if state[ch] is None:
                    continue
                c = step if ch[0] == 0 else n_chunk - 1 - step
                pr = probs[ch + (c // 2,)]
                cs = slice((c % 2) * CHUNK, (c % 2 + 1) * CHUNK)
                lhs = jnp.concatenate([-pr["kx"][c % 2][:, DN_DK:2 * DN_DK], pr["e"][cs]], axis=0)
                res[ch] = _dot(lhs.astype(BF16), state[ch].astype(BF16))
            for ch in chains:
                c = step if ch[0] == 0 else n_chunk - 1 - step
                pr = probs[ch + (c // 2,)]
                cs = slice((c % 2) * CHUNK, (c % 2 + 1) * CHUNK)
                b_c = pr["kx"][c % 2][:, 0:DN_DK]
                if state[ch] is None:
                    out_c[ch + (c,)] = pr["o0"][cs]
                    state[ch] = b_c
                else:
                    g_last = jnp.exp(pr["tot"][(c % 2) * CHUNK:(c % 2) * CHUNK + 1])
                    out_c[ch + (c,)] = res[ch][DN_DK:DN_DK + CHUNK] + pr["o0"][cs]
                    state[ch] = state[ch] * g_last + res[ch][0:DN_DK] + b_c
            side_piece()
        outs = {}
        for ch in chains:
            d, g, h = ch
            st_ref[g, d * DN_HEADS + h] = state[ch]
            outs[ch] = jnp.concatenate([out_c[ch + (c,)] for c in range(n_chunk)], axis=0)
        return outs

    if n_blk == 1:
        outs = delta_blocks([(d, g, 0, g) for g in range(n_seq) for d in range(2)], side)
        for g in range(n_seq):
            for h in range(DN_HEADS):
                o_ref[g, :, h * DN_DK:(h + 1) * DN_DK] = outs[(0, g, h)] + outs[(1, g, h)]
    else:
        def scan_block(b, carry):
            r0 = (pl.multiple_of(b * bt, bt), pl.multiple_of((n_blk - 1 - b) * bt, bt))
            outs = delta_blocks([(d, g, r0[d], (g, d)) for g in range(n_seq) for d in range(2)])
            for (d, g, h), o in outs.items():
                o_ref[g, pl.ds(r0[d], bt), h * DN_DK:(h + 1) * DN_DK] += o
            return carry

        lax.fori_loop(0, n_blk, scan_block, 0)

    for job in side:
        for _ in job:
            pass
    sfin_ref[...] = st_ref[...]


def _deltanet(act, gates, s0, n_seq, cast=(), conv=()):
    nb, seq_len, _ = act.shape
    n_state = 2 * DN_HEADS
    zero_init = s0 is None
    n_steps = nb // n_seq
    seq = lambda w: pl.BlockSpec((n_seq, seq_len, w), lambda b: (b, 0, 0))
    state = pl.BlockSpec((n_seq, n_state, DN_DK, DN_DK), lambda b: (b, 0, 0, 0))
    cast_in, cast_out = [], []
    for w in cast:
        _, r, cols = w.shape
        hold = 1 if (r // n_steps) % BF16_ROWS == 0 else 2
        assert (r * hold) % (n_steps * BF16_ROWS) == 0 and cols % (CAST_PIECES * LANES) == 0
        rows = r * hold // n_steps
        cast_in.append(pl.BlockSpec((None, rows, cols), lambda b, hold=hold: (0, b // hold, 0)))
        cast_out.append(pl.BlockSpec((rows, cols), lambda b, hold=hold: (b // hold, 0)))
    with_conv = bool(conv)
    assert not with_conv or seq_len == SUB_TILE
    conv_in = [seq(2 * CV_W)] + [_const_spec(c.shape) for c in conv[1:]] if with_conv else []
    conv_out = [seq(CV_W)] if with_conv else []
    conv_shape = [jax.ShapeDtypeStruct((nb, seq_len, CV_W), BF16)] if with_conv else []
    operands = (act, gates) + (() if zero_init else (s0,)) + tuple(conv) + tuple(cast)
    return pl.pallas_call(
        functools.partial(_deltanet_kernel, seq_len=seq_len, n_seq=n_seq, zero_init=zero_init,
                          n_cast=len(cast), with_conv=with_conv),
        grid=(n_steps,),
        in_specs=[seq(QKV_W), seq(LANES)] + ([] if zero_init else [state]) + conv_in + cast_in,
        out_specs=[seq(DN_W), state] + conv_out + cast_out,
        out_shape=[jax.ShapeDtypeStruct((nb, seq_len, DN_W), F32),
                   jax.ShapeDtypeStruct((nb, n_state, DN_DK, DN_DK), F32)] + conv_shape
                  + [jax.ShapeDtypeStruct(w.shape[1:], BF16) for w in cast],
        scratch_shapes=[pltpu.VMEM((n_seq, n_state, DN_DK, DN_DK), F32)]
                       + (_conv_scratch(seq_len) if with_conv else []),
        compiler_params=_params(1),
        name="deltanet",
    )(*operands)


def _layer(x, mod, s0, row_len, n_seq, p, w_tail):
    nb, seq_len, _ = x.shape
    n = nb * seq_len
    tiles_per_mod = (n // mod.shape[0]) // FFN_TILE
    conv_in_scan = seq_len == row_len == SUB_TILE
    conv_params = (p["cv_dw_w"], p["cv_dw_b"], p["cv_ln_g"], p["cv_ln_b"])
    x1, act, z, cv, gates = _ffn1_proj(x.reshape(n, D_MODEL), mod, p["norm_g"], p["wgu1"], p["wd1"],
                                       p["win"], p["dn_conv_w"], p["alog_row"], p["dtb_row"], *conv_params,
                                       tiles_per_mod, row_len, not conv_in_scan)
    to_cast = tuple(w_tail) if w_tail[0].dtype != BF16 else ()
    conv = (cv.reshape(nb, seq_len, 2 * CV_W),) + conv_params if conv_in_scan else ()
    o, s_fin, *rest = _deltanet(act.reshape(nb, seq_len, QKV_W), gates.reshape(nb, seq_len, LANES), s0,
                                n_seq, to_cast, conv)
    if conv_in_scan:
        cv, cast = rest[0].reshape(n, CV_W), rest[1:]
    else:
        cast = rest
    wout, wgu2, wd2 = cast if to_cast else w_tail
    y = _out_ffn2(x1, o.reshape(n, DN_W), z, cv, mod, p["norm_g"], p["dn_norm_g"], wout, wgu2, wd2,
                  tiles_per_mod)
    return y.reshape(nb, seq_len, D_MODEL), s_fin, (wout, wgu2, wd2)


def kernel(x_prompt, x_sample, state_delta, c, c_ctx, w_mod, b_mod, norm_g, ffn1_w_in, ffn1_w_out, w_in,
           dn_conv_w, dn_a_log, dn_dt_bias, dn_norm_g, cv_dw_w, cv_dw_b, cv_ln_g, cv_ln_b, w_out,
           ffn2_w_in, ffn2_w_out):
    depth = w_mod.shape[0]
    assert depth == 1, "one trunk layer"
    batch, seq_len, _ = x_prompt.shape
    dec_batch, dec_seq, _ = x_sample.shape
    n_state = 2 * DN_HEADS

    assert 1 + dec_batch <= SUBLANES and dn_conv_w.shape[1] == SHORT_CONV and cv_dw_w.shape[1] == CONV_K
    cond = jnp.zeros((SUBLANES, D_MODEL), F32).at[0].set(c_ctx).at[1:1 + dec_batch].set(c)
    mod = _modulation(cond, w_mod[0], b_mod[0]).reshape(SUBLANES, N_MOD, D_MODEL)
    mod_ctx, mod_lat = mod[0:1], mod[1:1 + dec_batch]

    win = _reorder_w_in(jnp.transpose(w_in, (0, 2, 1)))
    pad8 = lambda v: jnp.zeros((1, LANES), F32).at[0, 0:n_state].set(v.reshape(n_state))
    p = dict(norm_g=norm_g[0], wgu1=ffn1_w_in[0].astype(BF16), wd1=ffn1_w_out[0].astype(BF16), win=win,
             dn_conv_w=dn_conv_w[0], alog_row=pad8(dn_a_log[0]), dtb_row=pad8(dn_dt_bias[0]),
             dn_norm_g=dn_norm_g[0].reshape(1, DN_DK), cv_dw_w=cv_dw_w[0],
             cv_dw_b=cv_dw_b[0].reshape(1, CV_W), cv_ln_g=cv_ln_g[0].reshape(1, CV_W),
             cv_ln_b=cv_ln_b[0].reshape(1, CV_W))

    y_p, s_ctx, w_tail = _layer(x_prompt, mod_ctx, None, seq_len, CTX_SEQS_PER_STEP, p,
                                (w_out, ffn2_w_in, ffn2_w_out))
    s_lat = state_delta[:, 0].reshape(dec_batch, n_state, DN_DK, DN_DK)
    y_s, _, _ = _layer(x_sample, mod_lat, s_lat, GRID_W, 1, p, w_tail)
    new_state = s_ctx.reshape(batch, 1, 2, DN_HEADS, DN_DK, DN_DK).astype(x_prompt.dtype)
    return (y_p, y_s, new_state)
```

```python
import functools

import jax
import jax.numpy as jnp
from jax import lax
from jax.experimental import pallas as pl
from jax.experimental.pallas import tpu as pltpu

F32 = jnp.float32
BF16 = jnp.bfloat16

D_MODEL = 1024
D_FF = 2816
N_MOD = 9
GRID_W = 64
DN_W = 512
CV_W = 512
DN_HEADS = 4
DN_DK = 128
CHUNK = 64
SHORT_CONV = 3
CONV_K = 31
EPS = 1e-6

LANES = 128
SUBLANES = 8
BLOCK_T = 256
PAIR_T = 2 * CHUNK
GATE_BETA = 2 * DN_HEADS
GATE_TOT = 4 * DN_HEADS
BF16_ROWS = 16
CAST_PIECES = 4
CTX_SEQS_PER_STEP = 2
MOD_STEPS = 4
FFN_TILE = 512
SUB_TILE = 256
CONV_HALO = 16
CONV_ROWS = 32
VMEM_LIMIT_BYTES = 56 * 1024 * 1024
FF_CHUNKS = ((0, 768), (768, 768), (1536, 768), (2304, 512))
QKV_W = 3 * DN_W
IN_QKV = (0, QKV_W)
IN_Z = (QKV_W, QKV_W + DN_W)
IN_GLU = (QKV_W + DN_W, QKV_W + DN_W + 2 * CV_W)
IN_AB = (QKV_W + DN_W + 2 * CV_W, QKV_W + DN_W + 2 * CV_W + LANES)
IN_COLS_PADDED = IN_AB[1]


def _dot(a, b):
    return jnp.dot(a, b, preferred_element_type=F32)


def _dot_nt(a, b):
    return lax.dot_general(a, b, (((1,), (1,)), ((), ())), preferred_element_type=F32)


def _silu(x):
    return x * jax.nn.sigmoid(x)


def _rms(x, g):
    return x * lax.rsqrt(jnp.mean(x * x, axis=-1, keepdims=True) + EPS) * g


def _const_spec(shape):
    nd = len(shape)
    return pl.BlockSpec(shape, lambda *_: (0,) * nd, pipeline_mode=pl.Buffered(1))


def _params(n_grid_dims):
    return pltpu.CompilerParams(dimension_semantics=("arbitrary",) * n_grid_dims,
                                vmem_limit_bytes=VMEM_LIMIT_BYTES)


def _mod_kernel(c_ref, w_ref, b_ref, o_ref):
    s = _silu(c_ref[...]).astype(BF16)
    o_ref[...] = _dot(s, w_ref[...].astype(BF16)) + b_ref[...]


def _modulation(cond, w_mod, b_mod):
    rows = cond.shape[0]
    n_out = w_mod.shape[1]
    tn = n_out // MOD_STEPS
    return pl.pallas_call(
        _mod_kernel,
        grid=(MOD_STEPS,),
        in_specs=[pl.BlockSpec((rows, D_MODEL), lambda j: (0, 0)),
                  pl.BlockSpec((D_MODEL, tn), lambda j: (0, j)),
                  pl.BlockSpec((1, tn), lambda j: (0, j))],
        out_specs=pl.BlockSpec((rows, tn), lambda j: (0, j)),
        out_shape=jax.ShapeDtypeStruct((rows, n_out), F32),
        compiler_params=_params(1),
        name="modulation",
    )(cond, w_mod, b_mod.reshape(1, n_out))


def _reorder_w_in_kernel(w_ref, o_ref):
    ab0 = QKV_W
    z0 = ab0 + 4 * DN_HEADS
    glu0 = z0 + DN_W
    cols = w_ref.shape[1]

    def put(dst, src):
        for r in range(0, dst[1] - dst[0], LANES):
            o_ref[:, dst[0] + r:dst[0] + r + LANES] = jnp.transpose(w_ref[src + r:src + r + LANES, :]).astype(BF16)

    put(IN_QKV, 0)
    put(IN_Z, z0)
    put(IN_GLU, glu0)
    ab = jnp.concatenate([w_ref[ab0:z0, :], jnp.zeros((LANES - 4 * DN_HEADS, cols), F32)], axis=0)
    o_ref[:, IN_AB[0]:IN_AB[1]] = jnp.transpose(ab).astype(BF16)


def _reorder_w_in(w_in_t):
    _, rows, cols = w_in_t.shape
    tc = SUB_TILE
    return pl.pallas_call(
        _reorder_w_in_kernel,
        grid=(cols // tc,),
        in_specs=[pl.BlockSpec((None, rows, tc), lambda i: (0, 0, i))],
        out_specs=pl.BlockSpec((tc, IN_COLS_PADDED), lambda i: (i, 0)),
        out_shape=jax.ShapeDtypeStruct((cols, IN_COLS_PADDED), BF16),
        compiler_params=_params(1),
        name="reorder_w_in",
    )(w_in_t)


def _interleave(primary, secondary):
    more_a = more_b = True
    while more_a or more_b:
        if more_a:
            more_a = next(primary, StopIteration) is not StopIteration
        if more_b:
            more_b = secondary is not None and next(secondary, StopIteration) is not StopIteration


def _ffn1_proj_kernel(x_ref, mod_ref, g_ref, wgu_ref, wd_ref, win_ref, cw_ref, alog_ref, dtb_ref,
                      ccw_ref, ccb_ref, clg_ref, clb_ref,
                      x1_ref, act_ref, z_ref, cv_ref, gates_ref, xp_ref, cacc_ref, *, row_len):
    m = mod_ref[...]
    g = g_ref[...]
    ts = SUB_TILE
    n_rows = ts // row_len
    pitch = row_len + 2 * CONV_HALO
    total = n_rows * pitch
    grp = CONV_ROWS
    lead = CONV_HALO - CONV_K // 2
    rowi = lax.broadcasted_iota(jnp.int32, (ts, LANES), 0)
    lanei = lax.broadcasted_iota(jnp.int32, (ts, LANES), 1)
    pos_row = jnp.bitwise_and(rowi, row_len - 1)
    keep_prev = pos_row != 0
    keep_next = pos_row != row_len - 1
    pos_chunk = jnp.bitwise_and(rowi, CHUNK - 1)

    def matmul_part(rows, out):
        x = x_ref[rows, :]
        hb = (_rms(x, g[0:1]) * (1.0 + m[1:2]) + m[0:1]).astype(BF16)
        acc = None
        for s, n in FF_CHUNKS:
            gt = _dot(hb, wgu_ref[:, s:s + n])
            up = _dot(hb, wgu_ref[:, D_FF + s:D_FF + s + n])
            a = (_silu(gt) * up).astype(BF16)
            p = _dot(a, wd_ref[s:s + n, :])
            acc = p if acc is None else acc + p
            yield
        x1 = x + 0.5 * (m[2:3] * _rms(acc, g[1:2]))
        x1_ref[rows, :] = x1
        h1 = (_rms(x1, g[2:3]) * (1.0 + m[4:5]) + m[3:4]).astype(BF16)
        out["qkv"] = _dot(h1, win_ref[:, IN_QKV[0]:IN_QKV[1]])
        out["ab"] = _dot(h1, win_ref[:, IN_AB[0]:IN_AB[1]])
        z_ref[rows, :] = _dot(h1, win_ref[:, IN_Z[0]:IN_Z[1]])
        out["glu"] = _dot(h1, win_ref[:, IN_GLU[0]:IN_GLU[1]])

    def vector_part(rows, inp):
        qkv, ab, glu = inp["qkv"], inp["ab"], inp["glu"]
        for s in range(QKV_W // LANES):
            ls = slice(s * LANES, (s + 1) * LANES)
            xs = qkv[:, ls]
            w = cw_ref[:, ls]
            xp = jnp.where(keep_prev, pltpu.roll(xs, 1, 0), 0.0)
            xn = jnp.where(keep_next, pltpu.roll(xs, ts - 1, 0), 0.0)
            y = _silu(w[0:1] * xp + w[1:2] * xs + w[2:3] * xn)
            if s < 2 * DN_HEADS:
                y = y * lax.rsqrt(jnp.sum(y * y, axis=-1, keepdims=True) + EPS)
            if s < DN_HEADS:
                y = y * (DN_DK ** -0.5)
            act_ref[rows, ls] = y
            if s % 4 == 3:
                yield
        t = ab + dtb_ref[...]
        softplus = jnp.maximum(t, 0.0) + jnp.log1p(jnp.exp(-jnp.abs(t)))
        gate = -jnp.exp(alog_ref[...]) * softplus
        pre = gate
        suf = gate
        for sh in (1, 2, 4, 8, 16, 32):
            pre = pre + jnp.where(pos_chunk >= sh, pltpu.roll(pre, sh, 0), 0.0)
            suf = suf + jnp.where(pos_chunk < CHUNK - sh, pltpu.roll(suf, ts - sh, 0), 0.0)
        gc = jnp.where(lanei < DN_HEADS, pre, suf)
        tot = pltpu.roll(pre + suf - gate, GATE_TOT, 1)
        gates_ref[rows, :] = jnp.where(lanei < GATE_BETA, gc,
                                       jnp.where(lanei < GATE_TOT, jax.nn.sigmoid(ab), tot))
        yield
        zeros = jnp.zeros((CONV_HALO, LANES), F32)
        for lg in range(CV_W // LANES):
            x = glu[:, lg * LANES:(lg + 1) * LANES] * jax.nn.sigmoid(
                glu[:, CV_W + lg * LANES:CV_W + (lg + 1) * LANES])
            for r in range(n_rows):
                base = r * pitch
                xp_ref[0, lg, base:base + CONV_HALO, :] = zeros
                xp_ref[0, lg, base + CONV_HALO:base + CONV_HALO + row_len, :] = (
                    x[r * row_len:(r + 1) * row_len, :])
                xp_ref[0, lg, base + CONV_HALO + row_len:base + pitch, :] = zeros
            for b in range(1, SUBLANES):
                xp_ref[b, lg, 0:total - SUBLANES, :] = xp_ref[0, lg, b:b + total - SUBLANES, :]
            yield
        for i in range(ts // grp):
            base = i * grp + (i * grp // row_len) * (2 * CONV_HALO)
            for lg in range(CV_W // LANES):
                ls = slice(lg * LANES, (lg + 1) * LANES)
                acc = jnp.zeros((grp, LANES), F32) + ccb_ref[:, ls]
                for j in range(CONV_K):
                    shift = (lead + j) % SUBLANES
                    start = base + lead + j - shift
                    acc = acc + ccw_ref[j:j + 1, ls] * xp_ref[shift, lg, start:start + grp, :]
                cacc_ref[i * grp:(i + 1) * grp, ls] = acc
            yield
        c = cacc_ref[...]
        mu = jnp.mean(c, axis=-1, keepdims=True)
        var = jnp.mean(jnp.square(c - mu), axis=-1, keepdims=True)
        cn = (c - mu) * lax.rsqrt(var + EPS) * clg_ref[...] + clb_ref[...]
        cv_ref[rows, :] = _silu(cn).astype(BF16)

    n_sub = x_ref.shape[0] // ts
    carried = [dict() for _ in range(n_sub)]
    pending = None
    for s in range(n_sub):
        rows = pl.ds(s * ts, ts)
        _interleave(matmul_part(rows, carried[s]), pending)
        pending = vector_part(rows, carried[s])
    _interleave(pending, None)


def _ffn1_proj(x, mod, norm_g, wgu, wd, win, conv_w, alog_row, dtb_row, cv_w, cv_b, cv_ln_g, cv_ln_b,
               tiles_per_mod, row_len):
    n = x.shape[0]
    tm = FFN_TILE
    total = (SUB_TILE // row_len) * (row_len + 2 * CONV_HALO)
    row = lambda w: pl.BlockSpec((tm, w), lambda i: (i, 0))
    consts = (norm_g, wgu, wd, win, conv_w, alog_row, dtb_row, cv_w, cv_b, cv_ln_g, cv_ln_b)
    return pl.pallas_call(
        functools.partial(_ffn1_proj_kernel, row_len=row_len),
        grid=(n // tm,),
        in_specs=[row(D_MODEL),
                  pl.BlockSpec((None, N_MOD, D_MODEL), lambda i: (i // tiles_per_mod, 0, 0))]
                 + [_const_spec(c.shape) for c in consts],
        out_specs=[row(D_MODEL), row(QKV_W), row(DN_W), row(CV_W), row(LANES)],
        out_shape=[jax.ShapeDtypeStruct((n, D_MODEL), F32), jax.ShapeDtypeStruct((n, QKV_W), F32),
                   jax.ShapeDtypeStruct((n, DN_W), F32), jax.ShapeDtypeStruct((n, CV_W), BF16),
                   jax.ShapeDtypeStruct((n, LANES), F32)],
        scratch_shapes=[pltpu.VMEM((SUBLANES, CV_W // LANES, total, LANES), F32),
                        pltpu.VMEM((SUB_TILE, CV_W), F32)],
        compiler_params=_params(1),
        name="ffn1_proj",
    )(x, mod, *consts)


def _out_ffn2_kernel(x1_ref, o_ref, z_ref, cv_ref, mod_ref, g_ref, dng_ref, wout_ref, wgu_ref, wd_ref, y_ref):
    m = mod_ref[...]
    g = g_ref[...]
    dng = dng_ref[...]
    ts = SUB_TILE
    n_sub = x1_ref.shape[0] // ts

    def vector_part(s, out):
        rows = pl.ds(s * ts, ts)
        heads = []
        for h in range(DN_HEADS):
            hs = slice(h * DN_DK, (h + 1) * DN_DK)
            heads.append((_rms(o_ref[rows, hs], dng) * _silu(z_ref[rows, hs])).astype(BF16))
        out["og"] = jnp.concatenate(heads, axis=-1)
        yield

    def matmul_part(s, inp):
        rows = pl.ds(s * ts, ts)
        y = _dot(inp["og"], wout_ref[0:DN_W, :]) + _dot(cv_ref[rows, :], wout_ref[DN_W:, :])
        x2 = x1_ref[rows, :] + m[5:6] * _rms(y, g[3:4])
        hb = (_rms(x2, g[4:5]) * (1.0 + m[7:8]) + m[6:7]).astype(BF16)
        yield
        acc = None
        for c0, n in FF_CHUNKS:
            gt = _dot(hb, wgu_ref[:, c0:c0 + n])
            yield
            up = _dot(hb, wgu_ref[:, D_FF + c0:D_FF + c0 + n])
            a = (_silu(gt) * up).astype(BF16)
            yield
            p = _dot(a, wd_ref[c0:c0 + n, :])
            acc = p if acc is None else acc + p
            yield
        y_ref[rows, :] = x2 + 0.5 * (m[8:9] * _rms(acc, g[5:6]))

    carried = [dict() for _ in range(n_sub)]
    _interleave(vector_part(0, carried[0]), None)
    for s in range(n_sub):
        nxt = vector_part(s + 1, carried[s + 1]) if s + 1 < n_sub else None
        _interleave(matmul_part(s, carried[s]), nxt)


def _out_ffn2(x1, o, z, cv, mod, norm_g, dn_norm_g, wout, wgu, wd, tiles_per_mod):
    n = x1.shape[0]
    tm = FFN_TILE
    row = lambda w: pl.BlockSpec((tm, w), lambda i: (i, 0))
    consts = (norm_g, dn_norm_g, wout, wgu, wd)
    return pl.pallas_call(
        _out_ffn2_kernel,
        grid=(n // tm,),
        in_specs=[row(D_MODEL), row(DN_W), row(DN_W), row(CV_W),
                  pl.BlockSpec((None, N_MOD, D_MODEL), lambda i: (i // tiles_per_mod, 0, 0))]
                 + [_const_spec(c.shape) for c in consts],
        out_specs=row(D_MODEL),
        out_shape=jax.ShapeDtypeStruct((n, D_MODEL), F32),
        compiler_params=_params(1),
        name="out_ffn2",
    )(x1, o, z, cv, mod, *consts)


def _deltanet_kernel(*refs, seq_len, n_seq, zero_init, n_cast):
    refs = list(refs)
    act_ref, gates_ref = refs[0:2]
    n_in = 2 if zero_init else 3
    s0_ref = None if zero_init else refs[2]
    cast_in = refs[n_in:n_in + n_cast]
    o_ref, sfin_ref = refs[n_in + n_cast:n_in + n_cast + 2]
    cast_out = refs[n_in + n_cast + 2:n_in + 2 * n_cast + 2]
    st_ref = refs[-1]
    n_blk = seq_len // BLOCK_T

    def cast_pieces():
        for src, dst in zip(cast_in, cast_out):
            step = src.shape[1] // CAST_PIECES
            for j in range(CAST_PIECES):
                dst[:, j * step:(j + 1) * step] = src[:, j * step:(j + 1) * step].astype(BF16)
                yield

    side = cast_pieces()
    bt = BLOCK_T
    f32_inf = jnp.float32(jnp.inf)

    n_state = 2 * DN_HEADS
    if zero_init:
        st_ref[...] = jnp.zeros((n_seq, n_state, DN_DK, DN_DK), F32)
    else:
        st_ref[...] = s0_ref[...]
    if n_blk > 1:
        o_ref[...] = jnp.zeros((n_seq, seq_len, DN_W), F32)

    state_is_zero = zero_init and n_blk == 1
    pt = PAIR_T
    n_pair = bt // pt
    n_chunk = bt // CHUNK
    ri = lax.broadcasted_iota(jnp.int32, (pt, pt), 0)
    ci = lax.broadcasted_iota(jnp.int32, (pt, pt), 1)
    same = jnp.right_shift(ri, 6) == jnp.right_shift(ci, 6)
    incl = (same & (ri >= ci), same & (ri <= ci))
    strict = (same & (ri > ci), same & (ri < ci))
    first_chunk_lanes = ci < CHUNK

    def delta_blocks(jobs, side_jobs=None):
        def side_piece():
            if side_jobs is not None:
                next(side_jobs, None)

        gram = {}
        probs = {}
        for d, g, r0, key in jobs:
            rows = pl.ds(r0, bt)
            gates = gates_ref[g, rows, :]
            gates_t = jnp.transpose(gates)
            for h in range(DN_HEADS):
                c8 = d * DN_HEADS + h
                for p in range(n_pair):
                    ps = slice(p * pt, (p + 1) * pt)
                    prow = pl.ds(r0 + p * pt, pt)
                    qh = act_ref[g, prow, h * DN_DK:(h + 1) * DN_DK]
                    kh = act_ref[g, prow, DN_W + h * DN_DK:DN_W + (h + 1) * DN_DK]
                    vh = act_ref[g, prow, 2 * DN_W + h * DN_DK:2 * DN_W + (h + 1) * DN_DK]
                    if (key, h, p) not in gram:
                        khb = kh.astype(BF16)
                        gram[(key, h, p)] = (_dot_nt(jnp.concatenate([khb, qh.astype(BF16)], axis=0), khb),
                                             jnp.transpose(kh))
                    gr, kh_t = gram[(key, h, p)]
                    kk = gr[0:pt]
                    qk = gr[pt:2 * pt]
                    g_i = gates[ps, c8:c8 + 1]
                    g_j = gates_t[c8:c8 + 1, ps]
                    dec = jnp.exp(jnp.where(incl[d], g_i - g_j, -f32_inf))
                    b_i = gates[ps, GATE_BETA + c8:GATE_BETA + c8 + 1]
                    eg = jnp.exp(g_i)
                    kd_t = kh_t * jnp.exp(gates_t[GATE_TOT + c8:GATE_TOT + c8 + 1, ps] - g_j)
                    probs[(d, g, h, p)] = dict(
                        p=-(jnp.where(strict[d], kk * dec, 0.0) * b_i), qkm=(qk * dec).astype(BF16),
                        x=jnp.concatenate([vh * b_i, kh * (b_i * eg)], axis=-1), qd=qh * eg,
                        kd_t=[jnp.where(first_chunk_lanes, kd_t, 0.0).astype(BF16),
                              jnp.where(first_chunk_lanes, 0.0, kd_t).astype(BF16)],
                        tot=gates[ps, GATE_TOT + c8:GATE_TOT + c8 + 1])
        side_piece()
        for pr in probs.values():
            pb = pr["p"].astype(BF16)
            pr["q"] = pr["p"]
            pr["p"] = _dot(pb, pb)
        side_piece()
        for _ in range(4):
            for pr in probs.values():
                pb = pr["p"].astype(BF16)
                r = _dot(pb, jnp.concatenate([pb, pr["q"].astype(BF16)], axis=-1))
                pr["q"] = pr["q"] + pr["p"] + r[:, pt:2 * pt]
                pr["p"] = r[:, 0:pt]
            side_piece()
        for pr in probs.values():
            r = _dot(pr["p"].astype(BF16), pr["q"].astype(BF16))
            qm = pr["q"] + pr["p"] + r
            pr["x"] = pr["x"] + _dot(qm.astype(BF16), pr["x"].astype(BF16))
        side_piece()
        for pr in probs.values():
            xb = pr["x"].astype(BF16)
            qx = _dot(pr["qkm"], xb)
            pr["o0"] = qx[:, 0:DN_DK]
            pr["e"] = pr["qd"] - qx[:, DN_DK:2 * DN_DK]
            pr["kx"] = [_dot(kd_t, xb) for kd_t in pr["kd_t"]]
        side_piece()
        chains = [(d, g, h) for d, g, _, _ in jobs for h in range(DN_HEADS)]
        state = {(d, g, h): (None if state_is_zero else st_ref[g, d * DN_HEADS + h]) for d, g, h in chains}
        out_c = {}
        for step in range(n_chunk):
            res = {}
            for ch in chains:
                if state[ch] is None:
                    continue
                c = step if ch[0] == 0 else n_chunk - 1 - step
                pr = probs[ch + (c // 2,)]
                cs = slice((c % 2) * CHUNK, (c % 2 + 1) * CHUNK)
                lhs = jnp.concatenate([-pr["kx"][c % 2][:, DN_DK:2 * DN_DK], pr["e"][cs]], axis=0)
                res[ch] = _dot(lhs.astype(BF16), state[ch].astype(BF16))
            for ch in chains:
                c = step if ch[0] == 0 else n_chunk - 1 - step
                pr = probs[ch + (c // 2,)]
                cs = slice((c % 2) * CHUNK, (c % 2 + 1) * CHUNK)
                b_c = pr["kx"][c % 2][:, 0:DN_DK]
                if state[ch] is None:
                    out_c[ch + (c,)] = pr["o0"][cs]
                    state[ch] = b_c
                else:
                    g_last = jnp.exp(pr["tot"][(c % 2) * CHUNK:(c % 2) * CHUNK + 1])
                    out_c[ch + (c,)] = res[ch][DN_DK:DN_DK + CHUNK] + pr["o0"][cs]
                    state[ch] = state[ch] * g_last + res[ch][0:DN_DK] + b_c
            side_piece()
        outs = {}
        for ch in chains:
            d, g, h = ch
            st_ref[g, d * DN_HEADS + h] = state[ch]
            outs[ch] = jnp.concatenate([out_c[ch + (c,)] for c in range(n_chunk)], axis=0)
        return outs

    if n_blk == 1:
        outs = delta_blocks([(d, g, 0, g) for g in range(n_seq) for d in range(2)], side)
        for g in range(n_seq):
            for h in range(DN_HEADS):
                o_ref[g, :, h * DN_DK:(h + 1) * DN_DK] = outs[(0, g, h)] + outs[(1, g, h)]
    else:
        def scan_block(b, carry):
            r0 = (pl.multiple_of(b * bt, bt), pl.multiple_of((n_blk - 1 - b) * bt, bt))
            outs = delta_blocks([(d, g, r0[d], (g, d)) for g in range(n_seq) for d in range(2)])
            for (d, g, h), o in outs.items():
                o_ref[g, pl.ds(r0[d], bt), h * DN_DK:(h + 1) * DN_DK] += o
            return carry

        lax.fori_loop(0, n_blk, scan_block, 0)

    for _ in side:
        pass
    sfin_ref[...] = st_ref[...]


def _deltanet(act, gates, s0, n_seq, cast=()):
    nb, seq_len, _ = act.shape
    n_state = 2 * DN_HEADS
    zero_init = s0 is None
    n_steps = nb // n_seq
    seq = lambda w: pl.BlockSpec((n_seq, seq_len, w), lambda b: (b, 0, 0))
    state = pl.BlockSpec((n_seq, n_state, DN_DK, DN_DK), lambda b: (b, 0, 0, 0))
    cast_in, cast_out = [], []
    for w in cast:
        _, r, cols = w.shape
        hold = 1 if (r // n_steps) % BF16_ROWS == 0 else 2
        assert (r * hold) % (n_steps * BF16_ROWS) == 0 and cols % (CAST_PIECES * LANES) == 0
        rows = r * hold // n_steps
        cast_in.append(pl.BlockSpec((None, rows, cols), lambda b, hold=hold: (0, b // hold, 0)))
        cast_out.append(pl.BlockSpec((rows, cols), lambda b, hold=hold: (b // hold, 0)))
    operands = (act, gates) + (() if zero_init else (s0,)) + tuple(cast)
    return pl.pallas_call(
        functools.partial(_deltanet_kernel, seq_len=seq_len, n_seq=n_seq, zero_init=zero_init,
                          n_cast=len(cast)),
        grid=(n_steps,),
        in_specs=[seq(QKV_W), seq(LANES)] + ([] if zero_init else [state]) + cast_in,
        out_specs=[seq(DN_W), state] + cast_out,
        out_shape=[jax.ShapeDtypeStruct((nb, seq_len, DN_W), F32),
                   jax.ShapeDtypeStruct((nb, n_state, DN_DK, DN_DK), F32)]
                  + [jax.ShapeDtypeStruct(w.shape[1:], BF16) for w in cast],
        scratch_shapes=[pltpu.VMEM((n_seq, n_state, DN_DK, DN_DK), F32)],
        compiler_params=_params(1),
        name="deltanet",
    )(*operands)


def _layer(x, mod, s0, row_len, n_seq, p, w_tail):
    nb, seq_len, _ = x.shape
    n = nb * seq_len
    tiles_per_mod = (n // mod.shape[0]) // FFN_TILE
    x1, act, z, cv, gates = _ffn1_proj(x.reshape(n, D_MODEL), mod, p["norm_g"], p["wgu1"], p["wd1"],
                                       p["win"], p["dn_conv_w"], p["alog_row"], p["dtb_row"],
                                       p["cv_dw_w"], p["cv_dw_b"], p["cv_ln_g"], p["cv_ln_b"],
                                       tiles_per_mod, row_len)
    to_cast = tuple(w_tail) if w_tail[0].dtype != BF16 else ()
    o, s_fin, *cast = _deltanet(act.reshape(nb, seq_len, QKV_W), gates.reshape(nb, seq_len, LANES), s0,
                                n_seq, to_cast)
    wout, wgu2, wd2 = cast if to_cast else w_tail
    y = _out_ffn2(x1, o.reshape(n, DN_W), z, cv, mod, p["norm_g"], p["dn_norm_g"], wout, wgu2, wd2,
                  tiles_per_mod)
    return y.reshape(nb, seq_len, D_MODEL), s_fin, (wout, wgu2, wd2)


def kernel(x_prompt, x_sample, state_delta, c, c_ctx, w_mod, b_mod, norm_g, ffn1_w_in, ffn1_w_out, w_in,
           dn_conv_w, dn_a_log, dn_dt_bias, dn_norm_g, cv_dw_w, cv_dw_b, cv_ln_g, cv_ln_b, w_out,
           ffn2_w_in, ffn2_w_out):
    depth = w_mod.shape[0]
    assert depth == 1, "one trunk layer"
    batch, seq_len, _ = x_prompt.shape
    dec_batch, dec_seq, _ = x_sample.shape
    n_state = 2 * DN_HEADS

    assert 1 + dec_batch <= SUBLANES and dn_conv_w.shape[1] == SHORT_CONV and cv_dw_w.shape[1] == CONV_K
    cond = jnp.zeros((SUBLANES, D_MODEL), F32).at[0].set(c_ctx).at[1:1 + dec_batch].set(c)
    mod = _modulation(cond, w_mod[0], b_mod[0]).reshape(SUBLANES, N_MOD, D_MODEL)
    mod_ctx, mod_lat = mod[0:1], mod[1:1 + dec_batch]

    win = _reorder_w_in(jnp.transpose(w_in, (0, 2, 1)))
    pad8 = lambda v: jnp.zeros((1, LANES), F32).at[0, 0:n_state].set(v.reshape(n_state))
    p = dict(norm_g=norm_g[0], wgu1=ffn1_w_in[0].astype(BF16), wd1=ffn1_w_out[0].astype(BF16), win=win,
             dn_conv_w=dn_conv_w[0], alog_row=pad8(dn_a_log[0]), dtb_row=pad8(dn_dt_bias[0]),
             dn_norm_g=dn_norm_g[0].reshape(1, DN_DK), cv_dw_w=cv_dw_w[0],
             cv_dw_b=cv_dw_b[0].reshape(1, CV_W), cv_ln_g=cv_ln_g[0].reshape(1, CV_W),
             cv_ln_b=cv_ln_b[0].reshape(1, CV_W))

    y_p, s_ctx, w_tail = _layer(x_prompt, mod_ctx, None, seq_len, CTX_SEQS_PER_STEP, p,
                                (w_out, ffn2_w_in, ffn2_w_out))
    s_lat = state_delta[:, 0].reshape(dec_batch, n_state, DN_DK, DN_DK)
    y_s, _, _ = _layer(x_sample, mod_lat, s_lat, GRID_W, 1, p, w_tail)
    new_state = s_ctx.reshape(batch, 1, 2, DN_HEADS, DN_DK, DN_DK).astype(x_prompt.dtype)
    return (y_p, y_s, new_state)
```
